```python
import jax, jax.numpy as jnp
from jax import lax
import numpy as np

D_MODEL = 2048
BATCH = 2
SEQ = 4096
DEPTH = 1
DEC_BATCH = 32
DEC_SEQ = 16
PAST_LEN = 2048

CHUNK = 64
D_MIX = D_MODEL
D_A = D_MIX // 2
D_B = D_MIX - D_A
GMLP_CHUNK = 128
GMLP_GROUPS = 8
GMLP_GDIM = D_A // GMLP_GROUPS
LRU_HEADS = 8
LRU_HDIM = D_B // LRU_HEADS
LRU_CONV = 4
LRU_C = 8.0
N_MEM = 256
XA_HEADS = 4
XA_HDIM = D_MODEL // XA_HEADS
D_FF = 3 * D_MODEL
FFN_CONV = 3
D_IN = 2 * D_A + 2 * D_B
EPS = 1e-6

kernel_name = 'hymba_gmlp_rglru_streaming_step'


def rms_norm(x, g):
    xf = x.astype(jnp.float32)
    y = xf * lax.rsqrt(jnp.mean(xf * xf, axis=-1, keepdims=True) + EPS)
    return (y * g.astype(jnp.float32)).astype(x.dtype)


def causal_dwconv(x, state, w, b):
    width = w.shape[0]
    T = x.shape[1]
    xp = jnp.concatenate([state.astype(x.dtype), x], axis=1)
    y = xp[:, 0:T] * w[0]
    for k in range(1, width):
        y = y + xp[:, k:k + T] * w[k]
    return y + b, xp[:, T:]


def chunk_causal_mask(n):
    c = jnp.arange(n) // CHUNK
    return c[:, None] >= c[None, :]


def gmlp_spatial(v, w_s, b_s):
    n = v.shape[2]
    w = jnp.where(chunk_causal_mask(n)[None], w_s[:, :n, :n], 0.0).astype(v.dtype)
    out = jnp.einsum('gij,bnjgc->bnigc', w, v)
    return out + b_s[:, :n].T[None, None, :, :, None].astype(v.dtype)


def lin_combine(left, right):
    a_l, b_l = left
    a_r, b_r = right
    return a_l * a_r, a_r * b_l + b_r


def rglru(x, h0, w_a, b_a, w_x, b_x, lam):
    B, T, _ = x.shape
    xh = x.reshape(B, T, LRU_HEADS, LRU_HDIM)
    r = jax.nn.sigmoid((jnp.einsum('bthi,hij->bthj', xh, w_a).reshape(B, T, D_B) + b_a).astype(jnp.float32))
    i = jax.nn.sigmoid((jnp.einsum('bthi,hij->bthj', xh, w_x).reshape(B, T, D_B) + b_x).astype(jnp.float32))
    log_a = -LRU_C * r * jax.nn.softplus(-lam.astype(jnp.float32))
    a = jnp.exp(log_a)
    bterm = jnp.sqrt(-jnp.expm1(2.0 * log_a)) * (i * x.astype(jnp.float32))
    bterm = bterm.at[:, 0].add(a[:, 0] * h0.astype(jnp.float32))
    _, h = lax.associative_scan(lin_combine, (a, bterm), axis=1)
    return h.astype(x.dtype), h[:, -1].astype(h0.dtype)


def memory_kv(mem, g, w_kv):
    B, M, _ = mem.shape
    kv = rms_norm(mem, g) @ w_kv
    k = kv[..., :D_MODEL].reshape(B, M, XA_HEADS, XA_HDIM)
    v = kv[..., D_MODEL:].reshape(B, M, XA_HEADS, XA_HDIM)
    return k, v


def layer(x, mem_k, mem_v, lru_conv_state, lru_h0, ffn_state, lp):
    B, T, _ = x.shape
    h = rms_norm(x, lp['norm_mix'])
    z = h @ lp['w_in']
    uv = jax.nn.gelu(z[..., :2 * D_A])
    u = uv[..., :D_A]
    v = rms_norm(uv[..., D_A:], lp['g_v'])
    n = GMLP_CHUNK if T >= GMLP_CHUNK else T
    vc = v.reshape(B, T // n, n, GMLP_GROUPS, GMLP_GDIM)
    out_a = u * gmlp_spatial(vc, lp['gmlp_w'], lp['gmlp_b']).reshape(B, T, D_A)
    xr = z[..., 2 * D_A:2 * D_A + D_B]
    gate = z[..., 2 * D_A + D_B:]
    xr_c, lru_conv_new = causal_dwconv(xr, lru_conv_state, lp['lru_conv_w'], lp['lru_conv_b'])
    hseq, h_last = rglru(xr_c, lru_h0, lp['lru_wa'], lp['lru_ba'], lp['lru_wx'], lp['lru_bx'], lp['lru_lam'])
    out_b = hseq * jax.nn.gelu(gate)
    mix = jnp.concatenate([rms_norm(out_a, lp['g_a']), rms_norm(out_b, lp['g_b'])], axis=-1)
    x = x + mix @ lp['w_out']
    h = rms_norm(x, lp['norm_xa'])
    q = (h @ lp['w_q']).reshape(B, T, XA_HEADS, XA_HDIM)
    s = jnp.einsum('bthd,bmhd->bhtm', q, mem_k).astype(jnp.float32) * (XA_HDIM ** -0.5)
    p = jax.nn.softmax(s, axis=-1).astype(x.dtype)
    o = jnp.einsum('bhtm,bmhd->bthd', p, mem_v).reshape(B, T, D_MODEL)
    x = x + o @ lp['w_o']
    h = rms_norm(x, lp['norm_ffn'])
    up = h @ lp['w_up']
    up_c, ffn_new = causal_dwconv(up, ffn_state, lp['ffn_conv_w'], lp['ffn_conv_b'])
    x = x + (jax.nn.gelu(up_c[..., D_FF:]) * up_c[..., :D_FF]) @ lp['w_down']
    return x, v, lru_conv_new, h_last, ffn_new


def setup_inputs(seed: int = 0) -> dict:
    key = jax.random.key(seed)
    ks = iter(jax.random.split(key, 40))

    def nrm(shape, scale):
        return jax.random.normal(next(ks), shape, jnp.float32) * scale

    def gain(shape):
        return 1.0 + nrm(shape, 0.05)

    L = DEPTH
    a0 = jax.random.uniform(next(ks), (L, D_B), jnp.float32, 0.9, 0.999)
    return {
        'x_prompt': nrm((BATCH, SEQ, D_MODEL), 1.0),
        'x_sample': nrm((DEC_BATCH, DEC_SEQ, D_MODEL), 1.0),
        'mem_prompt': nrm((BATCH, N_MEM, D_MODEL), 1.0),
        'cache_mem_k': nrm((L, DEC_BATCH, N_MEM, XA_HEADS, XA_HDIM), 1.0),
        'cache_mem_v': nrm((L, DEC_BATCH, N_MEM, XA_HEADS, XA_HDIM), 1.0),
        'state_lru_h': nrm((L, DEC_BATCH, D_B), 0.5),
        'state_lru_conv': nrm((L, DEC_BATCH, LRU_CONV - 1, D_B), 1.0),
        'state_ffn_conv': nrm((L, DEC_BATCH, FFN_CONV - 1, 2 * D_FF), 1.0),
        'norm_mix': gain((L, D_MODEL)),
        'w_in': nrm((L, D_MODEL, D_IN), D_MODEL ** -0.5),
        'g_v': gain((L, D_A)),
        'gmlp_w': nrm((L, GMLP_GROUPS, GMLP_CHUNK, GMLP_CHUNK), GMLP_CHUNK ** -0.5),
        'gmlp_b': gain((L, GMLP_GROUPS, GMLP_CHUNK)),
        'lru_conv_w': nrm((L, LRU_CONV, D_B), LRU_CONV ** -0.5),
        'lru_conv_b': nrm((L, D_B), 0.02),
        'lru_wa': nrm((L, LRU_HEADS, LRU_HDIM, LRU_HDIM), LRU_HDIM ** -0.5),
        'lru_ba': nrm((L, D_B), 0.02),
        'lru_wx': nrm((L, LRU_HEADS, LRU_HDIM, LRU_HDIM), LRU_HDIM ** -0.5),
        'lru_bx': nrm((L, D_B), 0.02),
        'lru_lam': jnp.log(a0) - jnp.log1p(-a0),
        'g_a': gain((L, D_A)),
        'g_b': gain((L, D_B)),
        'w_out': nrm((L, D_MIX, D_MODEL), D_MIX ** -0.5),
        'norm_mem': gain((L, D_MODEL)),
        'w_kv': nrm((L, D_MODEL, 2 * D_MODEL), D_MODEL ** -0.5),
        'norm_xa': gain((L, D_MODEL)),
        'w_q': nrm((L, D_MODEL, D_MODEL), D_MODEL ** -0.5),
        'w_o': nrm((L, D_MODEL, D_MODEL), D_MODEL ** -0.5),
        'norm_ffn': gain((L, D_MODEL)),
        'w_up': nrm((L, D_MODEL, 2 * D_FF), D_MODEL ** -0.5),
        'ffn_conv_w': nrm((L, FFN_CONV, 2 * D_FF), FFN_CONV ** -0.5),
        'ffn_conv_b': nrm((L, 2 * D_FF), 0.02),
        'w_down': nrm((L, D_FF, D_MODEL), D_FF ** -0.5),
        'norm_final': gain((D_MODEL,)),
    }


def reference(x_prompt, x_sample, mem_prompt, cache_mem_k, cache_mem_v, state_lru_h, state_lru_conv,
              state_ffn_conv, norm_mix, w_in, g_v, gmlp_w, gmlp_b, lru_conv_w, lru_conv_b, lru_wa, lru_ba,
              lru_wx, lru_bx, lru_lam, g_a, g_b, w_out, norm_mem, w_kv, norm_xa, w_q, w_o, norm_ffn, w_up,
              ffn_conv_w, ffn_conv_b, w_down, norm_final):
    B = x_prompt.shape[0]
    dt = x_prompt.dtype
    xp, xs = x_prompt, x_sample
    mk_p, mv_p, h_p, c_p, f_p = [], [], [], [], []
    h_s, c_s, f_s, v_s = [], [], [], []
    for l in range(DEPTH):
        lp = {
            'norm_mix': norm_mix[l], 'w_in': w_in[l], 'g_v': g_v[l], 'gmlp_w': gmlp_w[l], 'gmlp_b': gmlp_b[l],
            'lru_conv_w': lru_conv_w[l], 'lru_conv_b': lru_conv_b[l], 'lru_wa': lru_wa[l], 'lru_ba': lru_ba[l],
            'lru_wx': lru_wx[l], 'lru_bx': lru_bx[l], 'lru_lam': lru_lam[l], 'g_a': g_a[l], 'g_b': g_b[l],
            'w_out': w_out[l], 'norm_xa': norm_xa[l], 'w_q': w_q[l], 'w_o': w_o[l], 'norm_ffn': norm_ffn[l],
            'w_up': w_up[l], 'ffn_conv_w': ffn_conv_w[l], 'ffn_conv_b': ffn_conv_b[l], 'w_down': w_down[l],
        }
        mk, mv = memory_kv(mem_prompt, norm_mem[l], w_kv[l])
        xp, _, c1, h1, f1 = layer(
            xp, mk, mv,
            jnp.zeros((B, LRU_CONV - 1, D_B), dt), jnp.zeros((B, D_B), dt),
            jnp.zeros((B, FFN_CONV - 1, 2 * D_FF), dt), lp)
        mk_p.append(mk); mv_p.append(mv); h_p.append(h1); c_p.append(c1); f_p.append(f1)
        xs, v2, c2, h2, f2 = layer(
            xs, cache_mem_k[l], cache_mem_v[l], state_lru_conv[l], state_lru_h[l], state_ffn_conv[l], lp)
        h_s.append(h2); c_s.append(c2); f_s.append(f2); v_s.append(v2)
    y_prompt = rms_norm(xp, norm_final)
    y_sample = rms_norm(xs, norm_final)
    mem_k_prompt = jnp.stack(mk_p)
    mem_v_prompt = jnp.stack(mv_p)
    lru_h_prompt = jnp.stack(h_p)
    lru_conv_prompt = jnp.stack(c_p)
    ffn_conv_prompt = jnp.stack(f_p)
    lru_h_sample = jnp.stack(h_s)
    lru_conv_sample = jnp.stack(c_s)
    ffn_conv_sample = jnp.stack(f_s)
    gmlp_v_sample = jnp.stack(v_s)
    return (y_prompt, y_sample, mem_k_prompt, mem_v_prompt, lru_h_prompt, lru_conv_prompt, ffn_conv_prompt,
            lru_h_sample, lru_conv_sample, ffn_conv_sample, gmlp_v_sample)
```

```python
import functools

import jax
import jax.numpy as jnp
from jax import lax
from jax.experimental import pallas as pl
from jax.experimental.pallas import tpu as pltpu

F32 = jnp.float32
BF = jnp.bfloat16

EPS = 1e-6
LRU_C = 8.0
CAUSAL_CHUNK = 64
HALO = 8
ROWS = 16
V7X_VMEM_BYTES = 64 * 1024 * 1024
VMEM_LIMIT = V7X_VMEM_BYTES - 8 * 1024 * 1024

MIX_ROWS = 256
ATT_ROWS = 512
FFN_ROWS = 512
FFN_COLS = 512
KV_COLS = 512
SAMPLE_MIX_STREAMS = 16
SAMPLE_ATT_STREAMS = 2


def _rms(x, g):
    return x * lax.rsqrt(jnp.mean(x * x, axis=-1, keepdims=True) + EPS) * g


def _gelu(x):
    return x * (0.5 * (1.0 + jnp.tanh(0.7978845608028654 * (x + 0.044715 * (x * x * x)))))


def _softplus(x):
    return jnp.maximum(x, 0.0) + jnp.log1p(jnp.exp(-jnp.abs(x)))


def _dot(a, b):
    return jnp.dot(a, b, preferred_element_type=F32)


def _rows_loop(n_rows, chunk, fn):
    def body(i, carry):
        fn(pl.multiple_of(i * chunk, chunk))
        return carry
    lax.fori_loop(0, n_rows // chunk, body, 0)


def _shifted_taps(ext, width):
    taps = []
    for k in range(width):
        shift = width - 1 - k
        src = ext if shift == 0 else pltpu.roll(ext, shift, axis=0)
        taps.append(src[HALO:, :])
    return taps


def _causal_conv(ext, w, b):
    taps = _shifted_taps(ext, w.shape[0])
    acc = taps[0] * w[0:1, :]
    for k in range(1, len(taps)):
        acc = acc + taps[k] * w[k:k + 1, :]
    return acc + b


def _halo_chunks(y_scr, n_streams, n_rows, fn):
    if n_streams == 1:
        _rows_loop(n_rows, ROWS, lambda r0: fn(y_scr[0, pl.ds(r0, ROWS + HALO), :], r0))
    else:
        assert n_rows == ROWS
        def body(s, carry):
            fn(y_scr[s], pl.multiple_of(s * ROWS, ROWS))
            return carry
        lax.fori_loop(0, n_streams, body, 0)


def _const_spec(shape):
    nd = len(shape)
    return pl.BlockSpec(shape, lambda *_: (0,) * nd, pipeline_mode=pl.Buffered(1))


def _params(n_grid):
    return pltpu.CompilerParams(dimension_semantics=("arbitrary",) * n_grid, vmem_limit_bytes=VMEM_LIMIT)


def _kv_body(mem_ref, g_ref, wk_ref, wv_ref, k_ref, v_ref, kb_ref, vb_ref, h_scr):
    @pl.when(pl.program_id(0) == 0)
    def _():
        g = g_ref[...]
        def norm(r0):
            rows = pl.ds(r0, ROWS)
            h_scr[rows, :] = _rms(mem_ref[rows, :], g).astype(BF)
        _rows_loop(mem_ref.shape[0], ROWS, norm)

    k = _dot(h_scr[...], wk_ref[...])
    k_ref[...] = k
    kb_ref[...] = k.astype(BF)
    v = _dot(h_scr[...], wv_ref[...])
    v_ref[...] = v
    vb_ref[...] = v.astype(BF)


def _memory_kv(mem, g, w_kv):
    m, d = mem.shape
    n_steps = d // KV_COLS
    col = pl.BlockSpec((m, KV_COLS), lambda j: (0, j))
    return pl.pallas_call(
        _kv_body,
        grid=(n_steps,),
        in_specs=[_const_spec((m, d)), _const_spec((1, d)),
                  pl.BlockSpec((d, KV_COLS), lambda j: (0, j)),
                  pl.BlockSpec((d, KV_COLS), lambda j: (0, j + n_steps))],
        out_specs=[col, col, col, col],
        out_shape=[jax.ShapeDtypeStruct((m, d), F32), jax.ShapeDtypeStruct((m, d), F32),
                   jax.ShapeDtypeStruct((m, d), BF), jax.ShapeDtypeStruct((m, d), BF)],
        scratch_shapes=[pltpu.VMEM((m, d), BF)],
        compiler_params=_params(1),
        name="memory_kv",
    )(mem, g, w_kv, w_kv)


def _scan_chunk(a, b, carry):
    n = a.shape[0]
    pos = lax.broadcasted_iota(jnp.int32, a.shape, 0)
    d = 1
    while d < n:
        keep = pos >= d
        a_prev = jnp.where(keep, pltpu.roll(a, d, axis=0), 1.0)
        b_prev = jnp.where(keep, pltpu.roll(b, d, axis=0), 0.0)
        b = a * b_prev + b
        a = a * a_prev
        d *= 2
    h = b + a * carry
    return h, h[n - 1:n, :]


def _mixer_body(streaming, n_streams, n_rows, *refs):
    S, L = n_streams, n_rows
    R = S * L
    refs = list(refs)
    x_ref = refs.pop(0)
    if streaming:
        pconv_ref, h0_ref = refs.pop(0), refs.pop(0)
    (nmix_ref, win_ref, gv_ref, gw_ref, gbias_ref, cw_ref, cb_ref, wa_ref, ba_ref, wx_ref, bx_ref,
     lam_ref, ga_ref, gb_ref, wout_ref) = refs[:15]
    refs = refs[15:]
    x1_ref = refs.pop(0)
    if streaming:
        v_ref = refs.pop(0)
    convnew_ref, hlast_ref = refs.pop(0), refs.pop(0)
    h_scr, z_scr, vb_scr, y_scr, xc_scr, xcb_scr, ra_scr, rx_scr, mix_scr, gwm_scr, carry_scr = refs

    DA = gv_ref.shape[1]
    DB = lam_ref.shape[1]
    G, NCH = gw_ref.shape[0], gw_ref.shape[1]
    GD = DA // G
    H, HD = wa_ref.shape[0], wa_ref.shape[1]
    W = cw_ref.shape[0]

    if streaming:
        first = pl.program_id(0) == 0
    else:
        t = pl.program_id(1)
        first = jnp.logical_and(pl.program_id(0) == 0, t == 0)

    @pl.when(first)
    def _():
        ri = lax.broadcasted_iota(jnp.int32, (NCH, NCH), 0)
        ci = lax.broadcasted_iota(jnp.int32, (NCH, NCH), 1)
        if streaming:
            sh = L.bit_length() - 1
            keep = lax.shift_right_logical(ri, sh) == lax.shift_right_logical(ci, sh)
        else:
            sh = CAUSAL_CHUNK.bit_length() - 1
            keep = lax.shift_right_logical(ri, sh) >= lax.shift_right_logical(ci, sh)
        for g in range(G):
            gwm_scr[g] = jnp.where(keep, gw_ref[g], 0.0).astype(BF)

    nmix = nmix_ref[...]
    def norm_in(r0):
        rows = pl.ds(r0, ROWS)
        h_scr[rows, :] = _rms(x_ref[rows, :], nmix).astype(BF)
    _rows_loop(R, ROWS, norm_in)

    z_scr[:, 0:DA] = _dot(h_scr[...], win_ref[:, 0:DA])
    z_scr[:, DA:2 * DA] = _dot(h_scr[...], win_ref[:, DA:2 * DA])
    y_scr[:, HALO:HALO + L, :] = _dot(h_scr[...], win_ref[:, 2 * DA:2 * DA + DB]).reshape(S, L, DB)
    z_scr[:, 2 * DA:2 * DA + DB] = _dot(h_scr[...], win_ref[:, 2 * DA + DB:2 * DA + 2 * DB])

    if streaming:
        y_scr[:, 0:HALO, :] = pconv_ref[...]
    else:
        @pl.when(t == 0)
        def _():
            y_scr[0, 0:HALO, :] = jnp.zeros((HALO, DB), F32)
            carry_scr[...] = jnp.zeros((1, DB), F32)

    gv = gv_ref[...]
    def act_uv(r0):
        rows = pl.ds(r0, ROWS)
        z_scr[rows, 0:DA] = _gelu(z_scr[rows, 0:DA])
        v = _rms(_gelu(z_scr[rows, DA:2 * DA]), gv)
        if streaming:
            v_ref[rows, :] = v
        vb_scr[rows, :] = v.astype(BF)
    _rows_loop(R, ROWS, act_uv)

    for c in range(R // NCH):
        rs = slice(c * NCH, (c + 1) * NCH)
        for g in range(G):
            cs = slice(g * GD, (g + 1) * GD)
            sp = _dot(gwm_scr[g], vb_scr[rs, cs])
            z_scr[rs, cs] = z_scr[rs, cs] * (sp + gbias_ref[:, cs])

    ga = ga_ref[...]
    def norm_a(r0):
        rows = pl.ds(r0, ROWS)
        mix_scr[rows, 0:DA] = _rms(z_scr[rows, 0:DA], ga).astype(BF)
    _rows_loop(R, ROWS, norm_a)

    cw, cb = cw_ref[...], cb_ref[...]
    def conv(ext, r0):
        rows = pl.ds(r0, ROWS)
        xc = _causal_conv(ext, cw, cb)
        xc_scr[rows, :] = xc
        xcb_scr[rows, :] = xc.astype(BF)
    _halo_chunks(y_scr, S, L, conv)

    tail = slice(HALO + L - (W - 1), HALO + L)
    if streaming:
        for s in range(S):
            convnew_ref[s] = y_scr[s, tail, :]
    else:
        convnew_ref[...] = y_scr[0, tail, :]
        y_scr[0, 0:HALO, :] = y_scr[0, L:L + HALO, :]

    for hh in range(H):
        cs = slice(hh * HD, (hh + 1) * HD)
        ra_scr[:, cs] = _dot(xcb_scr[:, cs], wa_ref[hh])
        rx_scr[:, cs] = _dot(xcb_scr[:, cs], wx_ref[hh])

    ba, bx = ba_ref[...], bx_ref[...]
    sp_lam = _softplus(-lam_ref[...])
    def gates(r0):
        rows = pl.ds(r0, ROWS)
        r = jax.nn.sigmoid(ra_scr[rows, :] + ba)
        i = jax.nn.sigmoid(rx_scr[rows, :] + bx)
        a = jnp.exp(-LRU_C * r * sp_lam)
        ra_scr[rows, :] = a
        rx_scr[rows, :] = jnp.sqrt(1.0 - a * a) * (i * xc_scr[rows, :])
    _rows_loop(R, ROWS, gates)

    if streaming:
        for s in range(S):
            rs = slice(s * L, (s + 1) * L)
            h, last = _scan_chunk(ra_scr[rs, :], rx_scr[rs, :], h0_ref[s])
            rx_scr[rs, :] = h
            hlast_ref[s] = last
    else:
        carry = carry_scr[...]
        for c in range(R // HALO):
            rs = slice(c * HALO, (c + 1) * HALO)
            h, carry = _scan_chunk(ra_scr[rs, :], rx_scr[rs, :], carry)
            rx_scr[rs, :] = h
        carry_scr[...] = carry
        hlast_ref[...] = carry

    gb = gb_ref[...]
    def norm_b(r0):
        rows = pl.ds(r0, ROWS)
        out_b = rx_scr[rows, :] * _gelu(z_scr[rows, 2 * DA:2 * DA + DB])
        mix_scr[rows, DA:DA + DB] = _rms(out_b, gb).astype(BF)
    _rows_loop(R, ROWS, norm_b)

    x1_ref[...] = x_ref[...] + _dot(mix_scr[...], wout_ref[...])


def _mixer(streaming, x, states, p):
    rows, D = x.shape
    DA, DB = p["g_v"].shape[1], p["lru_lam"].shape[1]
    weights = [p["norm_mix"], p["w_in"], p["g_v"], p["gmlp_w_eff"], p["gmlp_bias_rows"], p["lru_conv_w"],
               p["lru_conv_b"], p["lru_wa"], p["lru_ba"], p["lru_wx"], p["lru_bx"], p["lru_lam"], p["g_a"],
               p["g_b"], p["w_out"]]
    w_specs = [_const_spec(w.shape) for w in weights]
    W = p["lru_conv_w"].shape[0]
    if streaming:
        pconv, h0 = states
        N = h0.shape[0]
        L = rows // N
        S = SAMPLE_MIX_STREAMS
        R = S * L
        grid = (N // S,)
        row_spec = lambda c: pl.BlockSpec((R, c), lambda i: (i, 0))
        in_specs = [row_spec(D), pl.BlockSpec((S, HALO, DB), lambda i: (i, 0, 0)),
                    pl.BlockSpec((S, 1, DB), lambda i: (i, 0, 0))] + w_specs
        out_specs = [row_spec(D), row_spec(DA), pl.BlockSpec((S, W - 1, DB), lambda i: (i, 0, 0)),
                     pl.BlockSpec((S, 1, DB), lambda i: (i, 0, 0))]
        out_shape = [jax.ShapeDtypeStruct((rows, D), F32), jax.ShapeDtypeStruct((rows, DA), F32),
                     jax.ShapeDtypeStruct((N, W - 1, DB), F32), jax.ShapeDtypeStruct((N, 1, DB), F32)]
        args = [x, pconv, h0] + weights
    else:
        B, T = states
        S, L, R = 1, MIX_ROWS, MIX_ROWS
        nt = T // R
        grid = (B, nt)
        row_spec = lambda c: pl.BlockSpec((R, c), lambda b, t: (b * nt + t, 0))
        in_specs = [row_spec(D)] + w_specs
        out_specs = [row_spec(D), pl.BlockSpec((None, W - 1, DB), lambda b, t: (b, 0, 0)),
                     pl.BlockSpec((None, 1, DB), lambda b, t: (b, 0, 0))]
        out_shape = [jax.ShapeDtypeStruct((rows, D), F32), jax.ShapeDtypeStruct((B, W - 1, DB), F32),
                     jax.ShapeDtypeStruct((B, 1, DB), F32)]
        args = [x] + weights
    G, NCH = p["gmlp_w_eff"].shape[0], p["gmlp_w_eff"].shape[1]
    scratch = [pltpu.VMEM((R, D), BF),
               pltpu.VMEM((R, 2 * DA + DB), F32),
               pltpu.VMEM((R, DA), BF),
               pltpu.VMEM((S, L + HALO, DB), F32),
               pltpu.VMEM((R, DB), F32),
               pltpu.VMEM((R, DB), BF),
               pltpu.VMEM((R, DB), F32),
               pltpu.VMEM((R, DB), F32),
               pltpu.VMEM((R, DA + DB), BF),
               pltpu.VMEM((G, NCH, NCH), BF),
               pltpu.VMEM((1, DB), F32)]
    return pl.pallas_call(
        functools.partial(_mixer_body, streaming, S, L),
        grid=grid, in_specs=in_specs, out_specs=out_specs, out_shape=out_shape, scratch_shapes=scratch,
        compiler_params=_params(len(grid)),
        name="mixer_sample" if streaming else "mixer_prompt",
    )(*args)


def _attend(q_scr, o_scr, s_scr, row0, n_rows, k, v, n_heads):
    hd = k.shape[1] // n_heads
    scale = hd ** -0.5
    for h in range(n_heads):
        cs = slice(h * hd, (h + 1) * hd)
        q = q_scr[pl.ds(row0, n_rows), cs]
        s_scr[0:n_rows, :] = lax.dot_general(q, k[:, cs], (((1,), (1,)), ((), ())),
                                             preferred_element_type=F32) * scale
        def softmax(r0):
            rows = pl.ds(r0, ROWS)
            s = s_scr[rows, :]
            e = jnp.exp(s - jnp.max(s, axis=-1, keepdims=True))
            s_scr[rows, :] = e / jnp.sum(e, axis=-1, keepdims=True)
        _rows_loop(n_rows, ROWS, softmax)
        o_scr[pl.ds(row0, n_rows), cs] = _dot(s_scr[0:n_rows, :].astype(BF), v[:, cs]).astype(BF)


def _project_q(x_ref, nxa_ref, wq_ref, h_scr, q_scr):
    g = nxa_ref[...]
    def norm(r0):
        rows = pl.ds(r0, ROWS)
        h_scr[rows, :] = _rms(x_ref[rows, :], g).astype(BF)
    _rows_loop(x_ref.shape[0], ROWS, norm)
    q_scr[...] = _dot(h_scr[...], wq_ref[...]).astype(BF)


def _attn_prompt_body(n_heads, x_ref, nxa_ref, wq_ref, k_ref, v_ref, wo_ref, o_ref, h_scr, q_scr, a_scr, s_scr):
    _project_q(x_ref, nxa_ref, wq_ref, h_scr, q_scr)
    _attend(q_scr, a_scr, s_scr, 0, x_ref.shape[0], k_ref[...], v_ref[...], n_heads)
    o_ref[...] = x_ref[...] + _dot(a_scr[...], wo_ref[...])


def _attn_sample_body(n_heads, n_streams, n_rows, x_ref, nxa_ref, wq_ref, k_ref, v_ref, wo_ref, o_ref,
                      h_scr, q_scr, a_scr, s_scr):
    i = pl.program_id(0)

    @pl.when(i == 0)
    def _():
        _project_q(x_ref, nxa_ref, wq_ref, h_scr, q_scr)

    for s in range(n_streams):
        row0 = pl.multiple_of((i * n_streams + s) * n_rows, n_rows)
        _attend(q_scr, a_scr, s_scr, row0, n_rows, k_ref[s].astype(BF), v_ref[s].astype(BF), n_heads)

    @pl.when(i == pl.num_programs(0) - 1)
    def _():
        o_ref[...] = x_ref[...] + _dot(a_scr[...], wo_ref[...])


def _attn_prompt(x, k, v, p, n_heads, batch):
    rows, D = x.shape
    M = k.shape[0] // batch
    R = ATT_ROWS
    nt = rows // batch // R
    row_spec = pl.BlockSpec((R, D), lambda b, t: (b * nt + t, 0))
    mem_spec = pl.BlockSpec((M, D), lambda b, t: (b, 0))
    return pl.pallas_call(
        functools.partial(_attn_prompt_body, n_heads),
        grid=(batch, nt),
        in_specs=[row_spec, _const_spec((1, D)), _const_spec((D, D)), mem_spec, mem_spec, _const_spec((D, D))],
        out_specs=row_spec,
        out_shape=jax.ShapeDtypeStruct((rows, D), F32),
        scratch_shapes=[pltpu.VMEM((R, D), BF), pltpu.VMEM((R, D), BF), pltpu.VMEM((R, D), BF),
                        pltpu.VMEM((R, M), F32)],
        compiler_params=_params(2),
        name="attn_prompt",
    )(x, p["norm_xa"], p["w_q"], k, v, p["w_o"])


def _attn_sample(x, k, v, p, n_heads):
    rows, D = x.shape
    N, M = k.shape[0], k.shape[1]
    L = rows // N
    S = SAMPLE_ATT_STREAMS
    mem_spec = pl.BlockSpec((S, M, D), lambda i: (i, 0, 0))
    return pl.pallas_call(
        functools.partial(_attn_sample_body, n_heads, S, L),
        grid=(N // S,),
        in_specs=[_const_spec((rows, D)), _const_spec((1, D)), _const_spec((D, D)), mem_spec, mem_spec,
                  _const_spec((D, D))],
        out_specs=pl.BlockSpec((rows, D), lambda i: (0, 0)),
        out_shape=jax.ShapeDtypeStruct((rows, D), F32),
        scratch_shapes=[pltpu.VMEM((rows, D), BF), pltpu.VMEM((rows, D), BF), pltpu.VMEM((rows, D), BF),
                        pltpu.VMEM((L, M), F32)],
        compiler_params=_params(1),
        name="attn_sample",
    )(x, p["norm_xa"], p["w_q"], k, v, p["w_o"])


def _ffn_body(streaming, n_streams, n_rows, *refs):
    S, L = n_streams, n_rows
    R = S * L
    refs = list(refs)
    x_ref = refs.pop(0)
    if streaming:
        st_ref = refs.pop(0)
    (nffn_ref, wua_ref, wug_ref, cwa_ref, cwg_ref, cba_ref, cbg_ref, wd_ref, nfin_ref,
     y_ref, new_ref, h_scr, acc_scr, ya_scr, yg_scr, act_scr, carry_scr) = refs
    W = cwa_ref.shape[0]
    C = wua_ref.shape[1]

    if streaming:
        j = pl.program_id(1)
    else:
        t, j = pl.program_id(1), pl.program_id(2)
    nj = pl.num_programs(2 - int(streaming))

    @pl.when(j == 0)
    def _():
        g = nffn_ref[...]
        def norm(r0):
            rows = pl.ds(r0, ROWS)
            h_scr[rows, :] = _rms(x_ref[rows, :], g).astype(BF)
        _rows_loop(R, ROWS, norm)

    halves = ((ya_scr, wua_ref, 0), (yg_scr, wug_ref, 1))
    for y_scr, w_ref, half in halves:
        y_scr[:, HALO:HALO + L, :] = _dot(h_scr[...], w_ref[...]).reshape(S, L, C)
        if streaming:
            for s in range(S):
                y_scr[s, HALO - (W - 1):HALO, :] = st_ref[s, half]
        else:
            @pl.when(t == 0)
            def _():
                y_scr[0, 0:HALO, :] = jnp.zeros((HALO, C), F32)

            @pl.when(t > 0)
            def _():
                y_scr[0, 0:HALO, :] = carry_scr[j, half]
            carry_scr[j, half] = y_scr[0, L:L + HALO, :]

    tail = slice(HALO + L - (W - 1), HALO + L)
    if streaming:
        for s in range(S):
            new_ref[s, 0] = ya_scr[s, tail, :]
            new_ref[s, 1] = yg_scr[s, tail, :]
    else:
        new_ref[0] = ya_scr[0, tail, :]
        new_ref[1] = yg_scr[0, tail, :]

    cwa, cwg, cba, cbg = cwa_ref[...], cwg_ref[...], cba_ref[...], cbg_ref[...]
    if S == 1:
        def conv_act(r0):
            rows = pl.ds(r0, ROWS)
            lin = _causal_conv(ya_scr[0, pl.ds(r0, ROWS + HALO), :], cwa, cba)
            gated = _causal_conv(yg_scr[0, pl.ds(r0, ROWS + HALO), :], cwg, cbg)
            act_scr[rows, :] = (_gelu(gated) * lin).astype(BF)
        _rows_loop(R, ROWS, conv_act)
    else:
        assert L == ROWS
        def conv_act(s, carry):
            rows = pl.ds(pl.multiple_of(s * ROWS, ROWS), ROWS)
            lin = _causal_conv(ya_scr[s], cwa, cba)
            gated = _causal_conv(yg_scr[s], cwg, cbg)
            act_scr[rows, :] = (_gelu(gated) * lin).astype(BF)
            return carry
        lax.fori_loop(0, S, conv_act, 0)

    down = _dot(act_scr[...], wd_ref[...])

    @pl.when(j == 0)
    def _():
        acc_scr[...] = down

    @pl.when(j > 0)
    def _():
        acc_scr[...] += down

    @pl.when(j == nj - 1)
    def _():
        g = nfin_ref[...]
        def final(r0):
            rows = pl.ds(r0, ROWS)
            y_ref[rows, :] = _rms(x_ref[rows, :] + acc_scr[rows, :], g)
        _rows_loop(R, ROWS, final)


def _ffn(streaming, x, state, p):
    rows, D = x.shape
    DFF = p["w_down"].shape[0]
    W = p["ffn_conv_w"].shape[0]
    C = FFN_COLS
    nj = DFF // C
    if streaming:
        N = state.shape[0]
        L = rows // N
        R = FFN_ROWS
        S = R // L
        grid = (rows // R, nj)
        row_map = lambda i, j: (i, 0)
        colmap = lambda off: (lambda i, j: (0, j + off))
        wd_map = lambda i, j: (j, 0)
        new_spec = pl.BlockSpec((S, 2, W - 1, C), lambda i, j: (i, 0, 0, j))
        new_shape = jax.ShapeDtypeStruct((N, 2, W - 1, DFF), F32)
        in_specs = [pl.BlockSpec((R, D), row_map),
                    pl.BlockSpec((S, 2, W - 1, C), lambda i, j: (i, 0, 0, j))]
        args = [x, state]
    else:
        B, T = state
        S, L, R = 1, FFN_ROWS, FFN_ROWS
        nt = T // R
        grid = (B, nt, nj)
        row_map = lambda b, t, j: (b * nt + t, 0)
        colmap = lambda off: (lambda b, t, j: (0, j + off))
        wd_map = lambda b, t, j: (j, 0)
        new_spec = pl.BlockSpec((None, None, 2, W - 1, C), lambda b, t, j: (b, t, 0, 0, j))
        new_shape = jax.ShapeDtypeStruct((B, nt, 2, W - 1, DFF), F32)
        in_specs = [pl.BlockSpec((R, D), row_map)]
        args = [x]
    in_specs += [_const_spec((1, D)),
                 pl.BlockSpec((D, C), colmap(0)), pl.BlockSpec((D, C), colmap(nj)),
                 pl.BlockSpec((W, C), colmap(0)), pl.BlockSpec((W, C), colmap(nj)),
                 pl.BlockSpec((1, C), colmap(0)), pl.BlockSpec((1, C), colmap(nj)),
                 pl.BlockSpec((C, D), wd_map), _const_spec((1, D))]
    args += [p["norm_ffn"], p["w_up"], p["w_up"], p["ffn_conv_w"], p["ffn_conv_w"], p["ffn_conv_b"],
             p["ffn_conv_b"], p["w_down"], p["norm_final"]]
    scratch = [pltpu.VMEM((R, D), BF),
               pltpu.VMEM((R, D), F32),
               pltpu.VMEM((S, L + HALO, C), F32),
               pltpu.VMEM((S, L + HALO, C), F32),
               pltpu.VMEM((R, C), BF),
               pltpu.VMEM((nj, 2, HALO, C), F32)]
    return pl.pallas_call(
        functools.partial(_ffn_body, streaming, S, L),
        grid=grid, in_specs=in_specs,
        out_specs=[pl.BlockSpec((R, D), row_map), new_spec],
        out_shape=[jax.ShapeDtypeStruct((rows, D), F32), new_shape],
        scratch_shapes=scratch,
        compiler_params=_params(len(grid)),
        name="ffn_sample" if streaming else "ffn_prompt",
    )(*args)


def kernel(x_prompt, x_sample, mem_prompt, cache_mem_k, cache_mem_v, state_lru_h, state_lru_conv, state_ffn_conv, norm_mix, w_in, g_v, gmlp_w, gmlp_b, lru_conv_w, lru_conv_b, lru_wa, lru_ba, lru_wx, lru_bx, lru_lam, g_a, g_b, w_out, norm_mem, w_kv, norm_xa, w_q, w_o, norm_ffn, w_up, ffn_conv_w, ffn_conv_b, w_down, norm_final):
    depth = w_in.shape[0]
    assert depth == 1, "single-layer trunk"
    B, T, D = x_prompt.shape
    N, L, _ = x_sample.shape
    M = mem_prompt.shape[1]
    n_heads, head_dim = cache_mem_k.shape[3], cache_mem_k.shape[4]
    G, NCH = gmlp_w.shape[1], gmlp_w.shape[2]
    DA, DB = g_v.shape[1], lru_lam.shape[1]
    GD = DA // G
    DFF = w_down.shape[1]
    W_LRU, W_FFN = lru_conv_w.shape[1], ffn_conv_w.shape[1]
    assert L <= CAUSAL_CHUNK and NCH % L == 0 and L == ROWS
    row = lambda a: a.reshape(1, -1)

    shared = {
        "norm_mix": row(norm_mix[0]), "w_in": w_in[0].astype(BF), "g_v": row(g_v[0]),
        "lru_conv_w": lru_conv_w[0], "lru_conv_b": row(lru_conv_b[0]),
        "lru_wa": lru_wa[0].astype(BF), "lru_ba": row(lru_ba[0]),
        "lru_wx": lru_wx[0].astype(BF), "lru_bx": row(lru_bx[0]), "lru_lam": row(lru_lam[0]),
        "g_a": row(g_a[0]), "g_b": row(g_b[0]), "w_out": w_out[0].astype(BF),
        "norm_xa": row(norm_xa[0]), "w_q": w_q[0].astype(BF), "w_o": w_o[0].astype(BF),
        "norm_ffn": row(norm_ffn[0]), "w_up": w_up[0].astype(BF), "ffn_conv_w": ffn_conv_w[0],
        "ffn_conv_b": row(ffn_conv_b[0]), "w_down": w_down[0].astype(BF), "norm_final": row(norm_final),
    }
    prompt_p = dict(shared, gmlp_w_eff=gmlp_w[0],
                    gmlp_bias_rows=jnp.repeat(gmlp_b[0].T, GD, axis=1))
    reps = NCH // L
    sample_p = dict(shared, gmlp_w_eff=jnp.tile(gmlp_w[0][:, :L, :L], (1, reps, reps)),
                    gmlp_bias_rows=jnp.repeat(jnp.tile(gmlp_b[0][:, :L].T, (reps, 1)), GD, axis=1))

    mk, mv, mk_b, mv_b = _memory_kv(mem_prompt.reshape(B * M, D), row(norm_mem[0]), w_kv[0].astype(BF))
    xp = x_prompt.reshape(B * T, D)
    xp, conv_p, h_p = _mixer(False, xp, (B, T), prompt_p)
    xp = _attn_prompt(xp, mk_b, mv_b, prompt_p, n_heads, B)
    y_p, ffn_p = _ffn(False, xp, (B, T), prompt_p)

    xs = x_sample.reshape(N * L, D)
    pconv = jnp.pad(state_lru_conv[0], ((0, 0), (HALO - (W_LRU - 1), 0), (0, 0)))
    xs, v_s, conv_s, h_s = _mixer(True, xs, (pconv, state_lru_h[0].reshape(N, 1, DB)), sample_p)
    xs = _attn_sample(xs, cache_mem_k[0].reshape(N, M, D), cache_mem_v[0].reshape(N, M, D), sample_p, n_heads)
    ffn_prev = jnp.swapaxes(state_ffn_conv[0].reshape(N, W_FFN - 1, 2, DFF), 1, 2)
    y_s, ffn_s = _ffn(True, xs, ffn_prev, sample_p)

    def ffn_state(a):
        return jnp.swapaxes(a, 1, 2).reshape(1, a.shape[0], W_FFN - 1, 2 * DFF)

    return (y_p.reshape(B, T, D), y_s.reshape(N, L, D),
            mk.reshape(1, B, M, n_heads, head_dim), mv.reshape(1, B, M, n_heads, head_dim),
            h_p.reshape(1, B, DB), conv_p.reshape(1, B, W_LRU - 1, DB), ffn_state(ffn_p[:, -1]),
            h_s.reshape(1, N, DB), conv_s.reshape(1, N, W_LRU - 1, DB), ffn_state(ffn_s),
            v_s.reshape(1, N, L, DA))
```

```python
import functools

import jax
import jax.numpy as jnp
from jax import lax
from jax.experimental import pallas as pl
from jax.experimental.pallas import tpu as pltpu

F32 = jnp.float32
BF = jnp.bfloat16

EPS = 1e-6
LRU_C = 8.0
CAUSAL_CHUNK = 64
HALO = 8
ROWS = 16
LOOP_UNROLL = 4
SOFTMAX_ROWS = 64
V7X_VMEM_BYTES = 64 * 1024 * 1024
VMEM_LIMIT = V7X_VMEM_BYTES - 8 * 1024 * 1024

MIX_ROWS = 256
ATT_ROWS = 512
FFN_ROWS = 512
FFN_COLS = 512
FFN_SUB = 256
FFN_CHUNK = 64
KV_COLS = 512
SAMPLE_MIX_STREAMS = 16
SAMPLE_ATT_STREAMS = 2


def _rms(x, g):
    return x * lax.rsqrt(jnp.mean(x * x, axis=-1, keepdims=True) + EPS) * g


def _gelu(x):
    return x * (0.5 * (1.0 + jnp.tanh(0.7978845608028654 * (x + 0.044715 * (x * x * x)))))


def _softplus(x):
    return jnp.maximum(x, 0.0) + jnp.log1p(jnp.exp(-jnp.abs(x)))


def _dot(a, b):
    return jnp.dot(a, b, preferred_element_type=F32)


def _rows_loop(n_rows, chunk, fn, unroll=LOOP_UNROLL):
    def body(i, carry):
        fn(pl.multiple_of(i * chunk, chunk))
        return carry
    trips = n_rows // chunk
    lax.fori_loop(0, trips, body, 0, unroll=min(unroll, trips))


def _causal_conv(ext, taps, bias):
    width = len(taps)
    acc = None
    for k in range(width):
        shift = width - 1 - k
        src = ext if shift == 0 else pltpu.roll(ext, shift, axis=0)
        term = src[HALO:, :] * taps[k]
        acc = term if acc is None else acc + term
    return acc + bias


def _halo_chunks(y_scr, n_streams, n_rows, fn):
    if n_streams == 1:
        _rows_loop(n_rows, ROWS, lambda r0: fn(y_scr[0, pl.ds(r0, ROWS + HALO), :], r0))
    else:
        assert n_rows == ROWS
        def body(s, carry):
            fn(y_scr[s], pl.multiple_of(s * ROWS, ROWS))
            return carry
        lax.fori_loop(0, n_streams, body, 0)


def _const_spec(shape):
    nd = len(shape)
    return pl.BlockSpec(shape, lambda *_: (0,) * nd, pipeline_mode=pl.Buffered(1))


def _params(n_grid):
    return pltpu.CompilerParams(dimension_semantics=("arbitrary",) * n_grid, vmem_limit_bytes=VMEM_LIMIT)


def _kv_body(mem_ref, g_ref, wk_ref, wv_ref, k_ref, v_ref, kb_ref, vb_ref, h_scr):
    @pl.when(pl.program_id(0) == 0)
    def _():
        g = g_ref[...]
        def norm(r0):
            rows = pl.ds(r0, ROWS)
            h_scr[rows, :] = _rms(mem_ref[rows, :], g).astype(BF)
        _rows_loop(mem_ref.shape[0], ROWS, norm)

    k = _dot(h_scr[...], wk_ref[...])
    k_ref[...] = k
    kb_ref[...] = k.astype(BF)
    v = _dot(h_scr[...], wv_ref[...])
    v_ref[...] = v
    vb_ref[...] = v.astype(BF)


def _memory_kv(mem, g, w_kv):
    m, d = mem.shape
    n_steps = d // KV_COLS
    col = pl.BlockSpec((m, KV_COLS), lambda j: (0, j))
    return pl.pallas_call(
        _kv_body,
        grid=(n_steps,),
        in_specs=[_const_spec((m, d)), _const_spec((1, d)),
                  pl.BlockSpec((d, KV_COLS), lambda j: (0, j)),
                  pl.BlockSpec((d, KV_COLS), lambda j: (0, j + n_steps))],
        out_specs=[col, col, col, col],
        out_shape=[jax.ShapeDtypeStruct((m, d), F32), jax.ShapeDtypeStruct((m, d), F32),
                   jax.ShapeDtypeStruct((m, d), BF), jax.ShapeDtypeStruct((m, d), BF)],
        scratch_shapes=[pltpu.VMEM((m, d), BF)],
        compiler_params=_params(1),
        name="memory_kv",
    )(mem, g, w_kv, w_kv)


def _scan_chunk(a, b, carry):
    n = a.shape[0]
    pos = lax.broadcasted_iota(jnp.int32, a.shape, 0)
    d = 1
    while d < n:
        keep = pos >= d
        a_prev = jnp.where(keep, pltpu.roll(a, d, axis=0), 1.0)
        b_prev = jnp.where(keep, pltpu.roll(b, d, axis=0), 0.0)
        b = a * b_prev + b
        a = a * a_prev
        d *= 2
    h = b + a * carry
    return h, h[n - 1:n, :]


def _mixer_body(streaming, n_streams, n_rows, *refs):
    S, L = n_streams, n_rows
    R = S * L
    refs = list(refs)
    x_ref = refs.pop(0)
    if streaming:
        pconv_ref, h0_ref = refs.pop(0), refs.pop(0)
    (nmix_ref, win_ref, gv_ref, gw_ref, gbias_ref, cw_ref, cb_ref, wa_ref, ba_ref, wx_ref, bx_ref,
     lam_ref, ga_ref, gb_ref, wout_ref) = refs[:15]
    refs = refs[15:]
    x1_ref = refs.pop(0)
    if streaming:
        v_ref = refs.pop(0)
    convnew_ref, hlast_ref = refs.pop(0), refs.pop(0)
    h_scr, z_scr, vb_scr, y_scr, xc_scr, xcb_scr, ra_scr, rx_scr, mix_scr, gwm_scr, carry_scr = refs

    DA = gv_ref.shape[1]
    DB = lam_ref.shape[1]
    G, NCH = gw_ref.shape[0], gw_ref.shape[1]
    GD = DA // G
    H, HD = wa_ref.shape[0], wa_ref.shape[1]
    W = cw_ref.shape[0]

    if streaming:
        first = pl.program_id(0) == 0
    else:
        t = pl.program_id(1)
        first = jnp.logical_and(pl.program_id(0) == 0, t == 0)

    @pl.when(first)
    def _():
        ri = lax.broadcasted_iota(jnp.int32, (NCH, NCH), 0)
        ci = lax.broadcasted_iota(jnp.int32, (NCH, NCH), 1)
        if streaming:
            sh = L.bit_length() - 1
            keep = lax.shift_right_logical(ri, sh) == lax.shift_right_logical(ci, sh)
        else:
            sh = CAUSAL_CHUNK.bit_length() - 1
            keep = lax.shift_right_logical(ri, sh) >= lax.shift_right_logical(ci, sh)
        for g in range(G):
            gwm_scr[g] = jnp.where(keep, gw_ref[g], 0.0).astype(BF)

    nmix = nmix_ref[...]
    def norm_in(r0):
        rows = pl.ds(r0, ROWS)
        h_scr[rows, :] = _rms(x_ref[rows, :], nmix).astype(BF)
    _rows_loop(R, ROWS, norm_in)

    z_scr[:, 0:DA] = _dot(h_scr[...], win_ref[:, 0:DA])
    z_scr[:, DA:2 * DA] = _dot(h_scr[...], win_ref[:, DA:2 * DA])
    y_scr[:, HALO:HALO + L, :] = _dot(h_scr[...], win_ref[:, 2 * DA:2 * DA + DB]).reshape(S, L, DB)
    z_scr[:, 2 * DA:2 * DA + DB] = _dot(h_scr[...], win_ref[:, 2 * DA + DB:2 * DA + 2 * DB])

    if streaming:
        y_scr[:, 0:HALO, :] = pconv_ref[...]
    else:
        @pl.when(t == 0)
        def _():
            y_scr[0, 0:HALO, :] = jnp.zeros((HALO, DB), F32)
            carry_scr[...] = jnp.zeros((1, DB), F32)

    gv = gv_ref[...]
    def act_uv(r0):
        rows = pl.ds(r0, ROWS)
        z_scr[rows, 0:DA] = _gelu(z_scr[rows, 0:DA])
        v = _rms(_gelu(z_scr[rows, DA:2 * DA]), gv)
        if streaming:
            v_ref[rows, :] = v
        vb_scr[rows, :] = v.astype(BF)
    _rows_loop(R, ROWS, act_uv)

    for c in range(R // NCH):
        rs = slice(c * NCH, (c + 1) * NCH)
        for g in range(G):
            cs = slice(g * GD, (g + 1) * GD)
            sp = _dot(gwm_scr[g], vb_scr[rs, cs])
            z_scr[rs, cs] = z_scr[rs, cs] * (sp + gbias_ref[:, cs])

    ga = ga_ref[...]
    def norm_a(r0):
        rows = pl.ds(r0, ROWS)
        mix_scr[rows, 0:DA] = _rms(z_scr[rows, 0:DA], ga).astype(BF)
    _rows_loop(R, ROWS, norm_a)

    cw, cb = [cw_ref[k:k + 1, :] for k in range(W)], cb_ref[...]
    def conv(ext, r0):
        rows = pl.ds(r0, ROWS)
        xc = _causal_conv(ext, cw, cb)
        xc_scr[rows, :] = xc
        xcb_scr[rows, :] = xc.astype(BF)
    _halo_chunks(y_scr, S, L, conv)

    tail = slice(HALO + L - (W - 1), HALO + L)
    if streaming:
        for s in range(S):
            convnew_ref[s] = y_scr[s, tail, :]
    else:
        convnew_ref[...] = y_scr[0, tail, :]
        y_scr[0, 0:HALO, :] = y_scr[0, L:L + HALO, :]

    for hh in range(H):
        cs = slice(hh * HD, (hh + 1) * HD)
        ra_scr[:, cs] = _dot(xcb_scr[:, cs], wa_ref[hh])
        rx_scr[:, cs] = _dot(xcb_scr[:, cs], wx_ref[hh])

    ba, bx = ba_ref[...], bx_ref[...]
    sp_lam = _softplus(-lam_ref[...])
    def gates(r0):
        rows = pl.ds(r0, ROWS)
        r = jax.nn.sigmoid(ra_scr[rows, :] + ba)
        i = jax.nn.sigmoid(rx_scr[rows, :] + bx)
        a = jnp.exp(-LRU_C * r * sp_lam)
        ra_scr[rows, :] = a
        rx_scr[rows, :] = jnp.sqrt(1.0 - a * a) * (i * xc_scr[rows, :])
    _rows_loop(R, ROWS, gates)

    if streaming:
        for s in range(S):
            rs = slice(s * L, (s + 1) * L)
            h, last = _scan_chunk(ra_scr[rs, :], rx_scr[rs, :], h0_ref[s])
            rx_scr[rs, :] = h
            hlast_ref[s] = last
    else:
        carry = carry_scr[...]
        for c in range(R // HALO):
            rs = slice(c * HALO, (c + 1) * HALO)
            h, carry = _scan_chunk(ra_scr[rs, :], rx_scr[rs, :], carry)
            rx_scr[rs, :] = h
        carry_scr[...] = carry
        hlast_ref[...] = carry

    gb = gb_ref[...]
    def norm_b(r0):
        rows = pl.ds(r0, ROWS)
        out_b = rx_scr[rows, :] * _gelu(z_scr[rows, 2 * DA:2 * DA + DB])
        mix_scr[rows, DA:DA + DB] = _rms(out_b, gb).astype(BF)
    _rows_loop(R, ROWS, norm_b)

    x1_ref[...] = x_ref[...] + _dot(mix_scr[...], wout_ref[...])


def _mixer(streaming, x, states, p):
    rows, D = x.shape
    DA, DB = p["g_v"].shape[1], p["lru_lam"].shape[1]
    weights = [p["norm_mix"], p["w_in"], p["g_v"], p["gmlp_w_eff"], p["gmlp_bias_rows"], p["lru_conv_w"],
               p["lru_conv_b"], p["lru_wa"], p["lru_ba"], p["lru_wx"], p["lru_bx"], p["lru_lam"], p["g_a"],
               p["g_b"], p["w_out"]]
    w_specs = [_const_spec(w.shape) for w in weights]
    W = p["lru_conv_w"].shape[0]
    if streaming:
        pconv, h0 = states
        N = h0.shape[0]
        L = rows // N
        S = SAMPLE_MIX_STREAMS
        R = S * L
        grid = (N // S,)
        row_spec = lambda c: pl.BlockSpec((R, c), lambda i: (i, 0))
        in_specs = [row_spec(D), pl.BlockSpec((S, HALO, DB), lambda i: (i, 0, 0)),
                    pl.BlockSpec((S, 1, DB), lambda i: (i, 0, 0))] + w_specs
        out_specs = [row_spec(D), row_spec(DA), pl.BlockSpec((S, W - 1, DB), lambda i: (i, 0, 0)),
                     pl.BlockSpec((S, 1, DB), lambda i: (i, 0, 0))]
        out_shape = [jax.ShapeDtypeStruct((rows, D), F32), jax.ShapeDtypeStruct((rows, DA), F32),
                     jax.ShapeDtypeStruct((N, W - 1, DB), F32), jax.ShapeDtypeStruct((N, 1, DB), F32)]
        args = [x, pconv, h0] + weights
    else:
        B, T = states
        S, L, R = 1, MIX_ROWS, MIX_ROWS
        nt = T // R
        grid = (B, nt)
        row_spec = lambda c: pl.BlockSpec((R, c), lambda b, t: (b * nt + t, 0))
        in_specs = [row_spec(D)] + w_specs
        out_specs = [row_spec(D), pl.BlockSpec((None, W - 1, DB), lambda b, t: (b, 0, 0)),
                     pl.BlockSpec((None, 1, DB), lambda b, t: (b, 0, 0))]
        out_shape = [jax.ShapeDtypeStruct((rows, D), F32), jax.ShapeDtypeStruct((B, W - 1, DB), F32),
                     jax.ShapeDtypeStruct((B, 1, DB), F32)]
        args = [x] + weights
    G, NCH = p["gmlp_w_eff"].shape[0], p["gmlp_w_eff"].shape[1]
    scratch = [pltpu.VMEM((R, D), BF),
               pltpu.VMEM((R, 2 * DA + DB), F32),
               pltpu.VMEM((R, DA), BF),
               pltpu.VMEM((S, L + HALO, DB), F32),
               pltpu.VMEM((R, DB), F32),
               pltpu.VMEM((R, DB), BF),
               pltpu.VMEM((R, DB), F32),
               pltpu.VMEM((R, DB), F32),
               pltpu.VMEM((R, DA + DB), BF),
               pltpu.VMEM((G, NCH, NCH), BF),
               pltpu.VMEM((1, DB), F32)]
    return pl.pallas_call(
        functools.partial(_mixer_body, streaming, S, L),
        grid=grid, in_specs=in_specs, out_specs=out_specs, out_shape=out_shape, scratch_shapes=scratch,
        compiler_params=_params(len(grid)),
        name="mixer_sample" if streaming else "mixer_prompt",
    )(*args)


def _attend(q_scr, o_scr, s_scr, p_scr, row0, n_rows, keys, values, head_dim):
    n_heads = q_scr.shape[1] // head_dim
    scale = head_dim ** -0.5
    rows = pl.ds(row0, n_rows)
    n_chunk = min(n_rows, SOFTMAX_ROWS)
    heads = [slice(h * head_dim, (h + 1) * head_dim) for h in range(n_heads)]
    for h, cs in enumerate(heads):
        s_scr[h] = lax.dot_general(q_scr[rows, cs], keys(cs), (((1,), (1,)), ((), ())),
                                   preferred_element_type=F32) * scale
    for h in range(n_heads):
        for r0 in range(0, n_rows, n_chunk):
            s = s_scr[h, r0:r0 + n_chunk, :]
            e = jnp.exp(s - jnp.max(s, axis=-1, keepdims=True))
            p_scr[h, r0:r0 + n_chunk, :] = (e / jnp.sum(e, axis=-1, keepdims=True)).astype(BF)
    for h, cs in enumerate(heads):
        o_scr[rows, cs] = _dot(p_scr[h], values(cs)).astype(BF)


def _project_q(x_ref, nxa_ref, wq_ref, h_scr, q_scr):
    g = nxa_ref[...]
    def norm(r0):
        rows = pl.ds(r0, ROWS)
        h_scr[rows, :] = _rms(x_ref[rows, :], g).astype(BF)
    _rows_loop(x_ref.shape[0], ROWS, norm)
    q_scr[...] = _dot(h_scr[...], wq_ref[...]).astype(BF)


def _attn_prompt_body(head_dim, x_ref, nxa_ref, wq_ref, k_ref, v_ref, wo_ref, o_ref,
                      h_scr, q_scr, a_scr, s_scr, p_scr):
    _project_q(x_ref, nxa_ref, wq_ref, h_scr, q_scr)
    _attend(q_scr, a_scr, s_scr, p_scr, 0, x_ref.shape[0], lambda cs: k_ref[:, cs], lambda cs: v_ref[:, cs],
            head_dim)
    o_ref[...] = x_ref[...] + _dot(a_scr[...], wo_ref[...])


def _attn_sample_body(head_dim, n_streams, n_rows, x_ref, nxa_ref, wq_ref, k_ref, v_ref, wo_ref, o_ref,
                      h_scr, q_scr, a_scr, s_scr, p_scr):
    i = pl.program_id(0)

    @pl.when(i == 0)
    def _():
        _project_q(x_ref, nxa_ref, wq_ref, h_scr, q_scr)

    for s in range(n_streams):
        row0 = pl.multiple_of((i * n_streams + s) * n_rows, n_rows)
        _attend(q_scr, a_scr, s_scr.at[s], p_scr.at[s], row0, n_rows,
                lambda cs: k_ref[s, :, cs].astype(BF), lambda cs: v_ref[s, :, cs].astype(BF), head_dim)

    @pl.when(i == pl.num_programs(0) - 1)
    def _():
        o_ref[...] = x_ref[...] + _dot(a_scr[...], wo_ref[...])


def _attn_prompt(x, k, v, p, n_heads, batch):
    rows, D = x.shape
    M = k.shape[0] // batch
    R = ATT_ROWS
    nt = rows // batch // R
    row_spec = pl.BlockSpec((R, D), lambda b, t: (b * nt + t, 0))
    mem_spec = pl.BlockSpec((M, D), lambda b, t: (b, 0))
    return pl.pallas_call(
        functools.partial(_attn_prompt_body, D // n_heads),
        grid=(batch, nt),
        in_specs=[row_spec, _const_spec((1, D)), _const_spec((D, D)), mem_spec, mem_spec, _const_spec((D, D))],
        out_specs=row_spec,
        out_shape=jax.ShapeDtypeStruct((rows, D), F32),
        scratch_shapes=[pltpu.VMEM((R, D), BF), pltpu.VMEM((R, D), BF), pltpu.VMEM((R, D), BF),
                        pltpu.VMEM((n_heads, R, M), F32), pltpu.VMEM((n_heads, R, M), BF)],
        compiler_params=_params(2),
        name="attn_prompt",
    )(x, p["norm_xa"], p["w_q"], k, v, p["w_o"])


def _attn_sample(x, k, v, p, n_heads):
    rows, D = x.shape
    N, M = k.shape[0], k.shape[1]
    L = rows // N
    S = SAMPLE_ATT_STREAMS
    mem_spec = pl.BlockSpec((S, M, D), lambda i: (i, 0, 0))
    return pl.pallas_call(
        functools.partial(_attn_sample_body, D // n_heads, S, L),
        grid=(N // S,),
        in_specs=[_const_spec((rows, D)), _const_spec((1, D)), _const_spec((D, D)), mem_spec, mem_spec,
                  _const_spec((D, D))],
        out_specs=pl.BlockSpec((rows, D), lambda i: (0, 0)),
        out_shape=jax.ShapeDtypeStruct((rows, D), F32),
        scratch_shapes=[pltpu.VMEM((rows, D), BF), pltpu.VMEM((rows, D), BF), pltpu.VMEM((rows, D), BF),
                        pltpu.VMEM((S, n_heads, L, M), F32), pltpu.VMEM((S, n_heads, L, M), BF)],
        compiler_params=_params(1),
        name="attn_sample",
    )(x, p["norm_xa"], p["w_q"], k, v, p["w_o"])


def _ffn_body(streaming, n_streams, n_rows, *refs):
    S, L = n_streams, n_rows
    R = S * L
    refs = list(refs)
    x_ref = refs.pop(0)
    if streaming:
        st_ref = refs.pop(0)
    (nffn_ref, wua_ref, wug_ref, cwa_ref, cwg_ref, cba_ref, cbg_ref, wd_ref, nfin_ref,
     y_ref, new_ref, h_scr, acc_scr, ya_scr, yg_scr, act_scr, carry_scr) = refs
    W = cwa_ref.shape[0]
    C = wua_ref.shape[1]

    if streaming:
        j = pl.program_id(1)
    else:
        t, j = pl.program_id(1), pl.program_id(2)
    nj = pl.num_programs(2 - int(streaming))

    @pl.when(j == 0)
    def _():
        g = nffn_ref[...]
        def norm(r0):
            rows = pl.ds(r0, ROWS)
            h_scr[rows, :] = _rms(x_ref[rows, :], g).astype(BF)
        _rows_loop(R, ROWS, norm)
        acc_scr[...] = jnp.zeros(acc_scr.shape, F32)
        if not streaming:
            @pl.when(t == 0)
            def _():
                carry_scr[...] = jnp.zeros(carry_scr.shape, F32)

    tail = slice(HALO + L - (W - 1), HALO + L)
    n_chunk = min(L, FFN_CHUNK)
    for c in range(C // FFN_SUB):
        cs = slice(c * FFN_SUB, (c + 1) * FFN_SUB)
        for half, (y_scr, w_ref) in enumerate(((ya_scr, wua_ref), (yg_scr, wug_ref))):
            y_scr[:, HALO:HALO + L, cs] = _dot(h_scr[...], w_ref[:, cs]).reshape(S, L, FFN_SUB)
            if streaming:
                for s in range(S):
                    y_scr[s, HALO - (W - 1):HALO, cs] = st_ref[s, half, :, cs]
                    new_ref[s, half, :, cs] = y_scr[s, tail, cs]
            else:
                y_scr[0, 0:HALO, cs] = carry_scr[j, half, :, cs]
                carry_scr[j, half, :, cs] = y_scr[0, L:L + HALO, cs]
                new_ref[half, :, cs] = y_scr[0, tail, cs]

        taps_a = [jnp.broadcast_to(cwa_ref[k:k + 1, cs], (n_chunk, FFN_SUB)) for k in range(W)]
        taps_g = [jnp.broadcast_to(cwg_ref[k:k + 1, cs], (n_chunk, FFN_SUB)) for k in range(W)]
        bias_a = jnp.broadcast_to(cba_ref[:, cs], (n_chunk, FFN_SUB))
        bias_g = jnp.broadcast_to(cbg_ref[:, cs], (n_chunk, FFN_SUB))
        for s in range(S):
            for r0 in range(0, L, n_chunk):
                ext = slice(r0, r0 + n_chunk + HALO)
                lin = _causal_conv(ya_scr[s, ext, cs], taps_a, bias_a)
                gated = _causal_conv(yg_scr[s, ext, cs], taps_g, bias_g)
                out0 = s * L + r0
                act_scr[out0:out0 + n_chunk, cs] = (_gelu(gated) * lin).astype(BF)

    acc_scr[...] += _dot(act_scr[...], wd_ref[...])

    @pl.when(j == nj - 1)
    def _():
        g = nfin_ref[...]
        def final(r0):
            rows = pl.ds(r0, ROWS)
            y_ref[rows, :] = _rms(x_ref[rows, :] + acc_scr[rows, :], g)
        _rows_loop(R, ROWS, final)


def _ffn(streaming, x, state, p):
    rows, D = x.shape
    DFF = p["w_down"].shape[0]
    W = p["ffn_conv_w"].shape[0]
    C = FFN_COLS
    nj = DFF // C
    if streaming:
        N = state.shape[0]
        L = rows // N
        R = FFN_ROWS
        S = R // L
        grid = (rows // R, nj)
        row_map = lambda i, j: (i, 0)
        colmap = lambda off: (lambda i, j: (0, j + off))
        wd_map = lambda i, j: (j, 0)
        new_spec = pl.BlockSpec((S, 2, W - 1, C), lambda i, j: (i, 0, 0, j))
        new_shape = jax.ShapeDtypeStruct((N, 2, W - 1, DFF), F32)
        in_specs = [pl.BlockSpec((R, D), row_map),
                    pl.BlockSpec((S, 2, W - 1, C), lambda i, j: (i, 0, 0, j))]
        args = [x, state]
    else:
        B, T = state
        S, L, R = 1, FFN_ROWS, FFN_ROWS
        nt = T // R
        grid = (B, nt, nj)
        row_map = lambda b, t, j: (b * nt + t, 0)
        colmap = lambda off: (lambda b, t, j: (0, j + off))
        wd_map = lambda b, t, j: (j, 0)
        new_spec = pl.BlockSpec((None, None, 2, W - 1, C), lambda b, t, j: (b, t, 0, 0, j))
        new_shape = jax.ShapeDtypeStruct((B, nt, 2, W - 1, DFF), F32)
        in_specs = [pl.BlockSpec((R, D), row_map)]
        args = [x]
    in_specs += [_const_spec((1, D)),
                 pl.BlockSpec((D, C), colmap(0)), pl.BlockSpec((D, C), colmap(nj)),
                 pl.BlockSpec((W, C), colmap(0)), pl.BlockSpec((W, C), colmap(nj)),
                 pl.BlockSpec((1, C), colmap(0)), pl.BlockSpec((1, C), colmap(nj)),
                 pl.BlockSpec((C, D), wd_map), _const_spec((1, D))]
    args += [p["norm_ffn"], p["w_up"], p["w_up"], p["ffn_conv_w"], p["ffn_conv_w"], p["ffn_conv_b"],
             p["ffn_conv_b"], p["w_down"], p["norm_final"]]
    scratch = [pltpu.VMEM((R, D), BF),
               pltpu.VMEM((R, D), F32),
               pltpu.VMEM((S, L + HALO, C), F32),
               pltpu.VMEM((S, L + HALO, C), F32),
               pltpu.VMEM((R, C), BF),
               pltpu.VMEM((nj, 2, HALO, C), F32)]
    return pl.pallas_call(
        functools.partial(_ffn_body, streaming, S, L),
        grid=grid, in_specs=in_specs,
        out_specs=[pl.BlockSpec((R, D), row_map), new_spec],
        out_shape=[jax.ShapeDtypeStruct((rows, D), F32), new_shape],
        scratch_shapes=scratch,
        compiler_params=_params(len(grid)),
        name="ffn_sample" if streaming else "ffn_prompt",
    )(*args)


def kernel(x_prompt, x_sample, mem_prompt, cache_mem_k, cache_mem_v, state_lru_h, state_lru_conv, state_ffn_conv, norm_mix, w_in, g_v, gmlp_w, gmlp_b, lru_conv_w, lru_conv_b, lru_wa, lru_ba, lru_wx, lru_bx, lru_lam, g_a, g_b, w_out, norm_mem, w_kv, norm_xa, w_q, w_o, norm_ffn, w_up, ffn_conv_w, ffn_conv_b, w_down, norm_final):
    depth = w_in.shape[0]
    assert depth == 1, "single-layer trunk"
    B, T, D = x_prompt.shape
    N, L, _ = x_sample.shape
    M = mem_prompt.shape[1]
    n_heads, head_dim = cache_mem_k.shape[3], cache_mem_k.shape[4]
    G, NCH = gmlp_w.shape[1], gmlp_w.shape[2]
    DA, DB = g_v.shape[1], lru_lam.shape[1]
    GD = DA // G
    DFF = w_down.shape[1]
    W_LRU, W_FFN = lru_conv_w.shape[1], ffn_conv_w.shape[1]
    assert L <= CAUSAL_CHUNK and NCH % L == 0 and L == ROWS
    row = lambda a: a.reshape(1, -1)

    shared = {
        "norm_mix": row(norm_mix[0]), "w_in": w_in[0].astype(BF), "g_v": row(g_v[0]),
        "lru_conv_w": lru_conv_w[0], "lru_conv_b": row(lru_conv_b[0]),
        "lru_wa": lru_wa[0].astype(BF), "lru_ba": row(lru_ba[0]),
        "lru_wx": lru_wx[0].astype(BF), "lru_bx": row(lru_bx[0]), "lru_lam": row(lru_lam[0]),
        "g_a": row(g_a[0]), "g_b": row(g_b[0]), "w_out": w_out[0].astype(BF),
        "norm_xa": row(norm_xa[0]), "w_q": w_q[0].astype(BF), "w_o": w_o[0].astype(BF),
        "norm_ffn": row(norm_ffn[0]), "w_up": w_up[0].astype(BF), "ffn_conv_w": ffn_conv_w[0],
        "ffn_conv_b": row(ffn_conv_b[0]), "w_down": w_down[0].astype(BF), "norm_final": row(norm_final),
    }
    prompt_p = dict(shared, gmlp_w_eff=gmlp_w[0],
                    gmlp_bias_rows=jnp.repeat(gmlp_b[0].T, GD, axis=1))
    reps = NCH // L
    sample_p = dict(shared, gmlp_w_eff=jnp.tile(gmlp_w[0][:, :L, :L], (1, reps, reps)),
                    gmlp_bias_rows=jnp.repeat(jnp.tile(gmlp_b[0][:, :L].T, (reps, 1)), GD, axis=1))

    mk, mv, mk_b, mv_b = _memory_kv(mem_prompt.reshape(B * M, D), row(norm_mem[0]), w_kv[0].astype(BF))
    xp = x_prompt.reshape(B * T, D)
    xp, conv_p, h_p = _mixer(False, xp, (B, T), prompt_p)
    xp = _attn_prompt(xp, mk_b, mv_b, prompt_p, n_heads, B)
    y_p, ffn_p = _ffn(False, xp, (B, T), prompt_p)

    xs = x_sample.reshape(N * L, D)
    pconv = jnp.pad(state_lru_conv[0], ((0, 0), (HALO - (W_LRU - 1), 0), (0, 0)))
    xs, v_s, conv_s, h_s = _mixer(True, xs, (pconv, state_lru_h[0].reshape(N, 1, DB)), sample_p)
    xs = _attn_sample(xs, cache_mem_k[0].reshape(N, M, D), cache_mem_v[0].reshape(N, M, D), sample_p, n_heads)
    ffn_prev = jnp.swapaxes(state_ffn_conv[0].reshape(N, W_FFN - 1, 2, DFF), 1, 2)
    y_s, ffn_s = _ffn(True, xs, ffn_prev, sample_p)

    def ffn_state(a):
        return jnp.swapaxes(a, 1, 2).reshape(1, a.shape[0], W_FFN - 1, 2 * DFF)

    return (y_p.reshape(B, T, D), y_s.reshape(N, L, D),
            mk.reshape(1, B, M, n_heads, head_dim), mv.reshape(1, B, M, n_heads, head_dim),
            h_p.reshape(1, B, DB), conv_p.reshape(1, B, W_LRU - 1, DB), ffn_state(ffn_p[:, -1]),
            h_s.reshape(1, N, DB), conv_s.reshape(1, N, W_LRU - 1, DB), ffn_state(ffn_s),
            v_s.reshape(1, N, L, DA))
```

```python
import functools

import jax
import jax.numpy as jnp
from jax import lax
from jax.experimental import pallas as pl
from jax.experimental.pallas import tpu as pltpu

F32 = jnp.float32
BF = jnp.bfloat16

EPS = 1e-6
LRU_C = 8.0
CAUSAL_CHUNK = 64
HALO = 8
ROWS = 16
LOOP_UNROLL = 4
SOFTMAX_ROWS = 64
V7X_VMEM_BYTES = 64 * 1024 * 1024
VMEM_LIMIT = V7X_VMEM_BYTES - 8 * 1024 * 1024

MIX_ROWS = 256
ATT_ROWS = 512
FFN_ROWS = 512
FFN_COLS = 1024
FFN_SUB = 256
FFN_CHUNK = 64
KV_COLS = 512
SAMPLE_MIX_STREAMS = 16
SAMPLE_ATT_STREAMS = 2


def _rms(x, g):
    return x * lax.rsqrt(jnp.mean(x * x, axis=-1, keepdims=True) + EPS) * g


def _gelu(x):
    return x * (0.5 * (1.0 + jnp.tanh(0.7978845608028654 * (x + 0.044715 * (x * x * x)))))


def _softplus(x):
    return jnp.maximum(x, 0.0) + jnp.log1p(jnp.exp(-jnp.abs(x)))


def _dot(a, b):
    return jnp.dot(a, b, preferred_element_type=F32)


def _rows_loop(n_rows, chunk, fn, unroll=LOOP_UNROLL):
    def body(i, carry):
        fn(pl.multiple_of(i * chunk, chunk))
        return carry
    trips = n_rows // chunk
    lax.fori_loop(0, trips, body, 0, unroll=min(unroll, trips))


def _causal_conv(ext, taps, bias):
    width = len(taps)
    acc = None
    for k in range(width):
        shift = width - 1 - k
        src = ext if shift == 0 else pltpu.roll(ext, shift, axis=0)
        term = src[HALO:, :] * taps[k]
        acc = term if acc is None else acc + term
    return acc + bias


def _halo_chunks(y_scr, n_streams, n_rows, fn):
    if n_streams == 1:
        _rows_loop(n_rows, ROWS, lambda r0: fn(y_scr[0, pl.ds(r0, ROWS + HALO), :], r0))
    else:
        assert n_rows == ROWS
        def body(s, carry):
            fn(y_scr[s], pl.multiple_of(s * ROWS, ROWS))
            return carry
        lax.fori_loop(0, n_streams, body, 0)


def _const_spec(shape):
    nd = len(shape)
    return pl.BlockSpec(shape, lambda *_: (0,) * nd, pipeline_mode=pl.Buffered(1))


def _params(n_grid):
    return pltpu.CompilerParams(dimension_semantics=("arbitrary",) * n_grid, vmem_limit_bytes=VMEM_LIMIT)


def _kv_body(mem_ref, g_ref, wk_ref, wv_ref, k_ref, v_ref, kb_ref, vb_ref, h_scr):
    @pl.when(pl.program_id(0) == 0)
    def _():
        g = g_ref[...]
        def norm(r0):
            rows = pl.ds(r0, ROWS)
            h_scr[rows, :] = _rms(mem_ref[rows, :], g).astype(BF)
        _rows_loop(mem_ref.shape[0], ROWS, norm)

    k = _dot(h_scr[...], wk_ref[...])
    k_ref[...] = k
    kb_ref[...] = k.astype(BF)
    v = _dot(h_scr[...], wv_ref[...])
    v_ref[...] = v
    vb_ref[...] = v.astype(BF)


def _memory_kv(mem, g, w_kv):
    m, d = mem.shape
    n_steps = d // KV_COLS
    col = pl.BlockSpec((m, KV_COLS), lambda j: (0, j))
    return pl.pallas_call(
        _kv_body,
        grid=(n_steps,),
        in_specs=[_const_spec((m, d)), _const_spec((1, d)),
                  pl.BlockSpec((d, KV_COLS), lambda j: (0, j)),
                  pl.BlockSpec((d, KV_COLS), lambda j: (0, j + n_steps))],
        out_specs=[col, col, col, col],
        out_shape=[jax.ShapeDtypeStruct((m, d), F32), jax.ShapeDtypeStruct((m, d), F32),
                   jax.ShapeDtypeStruct((m, d), BF), jax.ShapeDtypeStruct((m, d), BF)],
        scratch_shapes=[pltpu.VMEM((m, d), BF)],
        compiler_params=_params(1),
        name="memory_kv",
    )(mem, g, w_kv, w_kv)


def _scan_chunk(a, b, carry):
    n = a.shape[0]
    pos = lax.broadcasted_iota(jnp.int32, a.shape, 0)
    d = 1
    while d < n:
        keep = pos >= d
        a_prev = jnp.where(keep, pltpu.roll(a, d, axis=0), 1.0)
        b_prev = jnp.where(keep, pltpu.roll(b, d, axis=0), 0.0)
        b = a * b_prev + b
        a = a * a_prev
        d *= 2
    h = b + a * carry
    return h, h[n - 1:n, :]


def _mixer_body(streaming, n_streams, n_rows, *refs):
    S, L = n_streams, n_rows
    R = S * L
    refs = list(refs)
    x_ref = refs.pop(0)
    if streaming:
        pconv_ref, h0_ref = refs.pop(0), refs.pop(0)
    (nmix_ref, win_ref, gv_ref, gw_ref, gbias_ref, cw_ref, cb_ref, wa_ref, ba_ref, wx_ref, bx_ref,
     lam_ref, ga_ref, gb_ref, wout_ref) = refs[:15]
    refs = refs[15:]
    x1_ref = refs.pop(0)
    if streaming:
        v_ref = refs.pop(0)
    convnew_ref, hlast_ref = refs.pop(0), refs.pop(0)
    h_scr, z_scr, vb_scr, y_scr, xc_scr, xcb_scr, ra_scr, rx_scr, mix_scr, gwm_scr, carry_scr = refs

    DA = gv_ref.shape[1]
    DB = lam_ref.shape[1]
    G, NCH = gw_ref.shape[0], gw_ref.shape[1]
    GD = DA // G
    H, HD = wa_ref.shape[0], wa_ref.shape[1]
    W = cw_ref.shape[0]

    if streaming:
        first = pl.program_id(0) == 0
    else:
        t = pl.program_id(1)
        first = jnp.logical_and(pl.program_id(0) == 0, t == 0)

    @pl.when(first)
    def _():
        ri = lax.broadcasted_iota(jnp.int32, (NCH, NCH), 0)
        ci = lax.broadcasted_iota(jnp.int32, (NCH, NCH), 1)
        if streaming:
            sh = L.bit_length() - 1
            keep = lax.shift_right_logical(ri, sh) == lax.shift_right_logical(ci, sh)
        else:
            sh = CAUSAL_CHUNK.bit_length() - 1
            keep = lax.shift_right_logical(ri, sh) >= lax.shift_right_logical(ci, sh)
        for g in range(G):
            gwm_scr[g] = jnp.where(keep, gw_ref[g], 0.0).astype(BF)

    nmix = nmix_ref[...]
    def norm_in(r0):
        rows = pl.ds(r0, ROWS)
        h_scr[rows, :] = _rms(x_ref[rows, :], nmix).astype(BF)
    _rows_loop(R, ROWS, norm_in)

    z_scr[:, 0:DA] = _dot(h_scr[...], win_ref[:, 0:DA])
    z_scr[:, DA:2 * DA] = _dot(h_scr[...], win_ref[:, DA:2 * DA])
    y_scr[:, HALO:HALO + L, :] = _dot(h_scr[...], win_ref[:, 2 * DA:2 * DA + DB]).reshape(S, L, DB)
    z_scr[:, 2 * DA:2 * DA + DB] = _dot(h_scr[...], win_ref[:, 2 * DA + DB:2 * DA + 2 * DB])

    if streaming:
        y_scr[:, 0:HALO, :] = pconv_ref[...]
    else:
        @pl.when(t == 0)
        def _():
            y_scr[0, 0:HALO, :] = jnp.zeros((HALO, DB), F32)
            carry_scr[...] = jnp.zeros((1, DB), F32)

    gv = gv_ref[...]
    def act_uv(r0):
        rows = pl.ds(r0, ROWS)
        z_scr[rows, 0:DA] = _gelu(z_scr[rows, 0:DA])
        v = _rms(_gelu(z_scr[rows, DA:2 * DA]), gv)
        if streaming:
            v_ref[rows, :] = v
        vb_scr[rows, :] = v.astype(BF)
    _rows_loop(R, ROWS, act_uv)

    for c in range(R // NCH):
        rs = slice(c * NCH, (c + 1) * NCH)
        for g in range(G):
            cs = slice(g * GD, (g + 1) * GD)
            sp = _dot(gwm_scr[g], vb_scr[rs, cs])
            z_scr[rs, cs] = z_scr[rs, cs] * (sp + gbias_ref[:, cs])

    ga = ga_ref[...]
    def norm_a(r0):
        rows = pl.ds(r0, ROWS)
        mix_scr[rows, 0:DA] = _rms(z_scr[rows, 0:DA], ga).astype(BF)
    _rows_loop(R, ROWS, norm_a)

    cw, cb = [cw_ref[k:k + 1, :] for k in range(W)], cb_ref[...]
    def conv(ext, r0):
        rows = pl.ds(r0, ROWS)
        xc = _causal_conv(ext, cw, cb)
        xc_scr[rows, :] = xc
        xcb_scr[rows, :] = xc.astype(BF)
    _halo_chunks(y_scr, S, L, conv)

    tail = slice(HALO + L - (W - 1), HALO + L)
    if streaming:
        for s in range(S):
            convnew_ref[s] = y_scr[s, tail, :]
    else:
        convnew_ref[...] = y_scr[0, tail, :]
        y_scr[0, 0:HALO, :] = y_scr[0, L:L + HALO, :]

    for hh in range(H):
        cs = slice(hh * HD, (hh + 1) * HD)
        ra_scr[:, cs] = _dot(xcb_scr[:, cs], wa_ref[hh])
        rx_scr[:, cs] = _dot(xcb_scr[:, cs], wx_ref[hh])

    ba, bx = ba_ref[...], bx_ref[...]
    sp_lam = _softplus(-lam_ref[...])
    def gates(r0):
        rows = pl.ds(r0, ROWS)
        r = jax.nn.sigmoid(ra_scr[rows, :] + ba)
        i = jax.nn.sigmoid(rx_scr[rows, :] + bx)
        a = jnp.exp(-LRU_C * r * sp_lam)
        ra_scr[rows, :] = a
        rx_scr[rows, :] = jnp.sqrt(1.0 - a * a) * (i * xc_scr[rows, :])
    _rows_loop(R, ROWS, gates)

    if streaming:
        for s in range(S):
            rs = slice(s * L, (s + 1) * L)
            h, last = _scan_chunk(ra_scr[rs, :], rx_scr[rs, :], h0_ref[s])
            rx_scr[rs, :] = h
            hlast_ref[s] = last
    else:
        carry = carry_scr[...]
        for c in range(R // HALO):
            rs = slice(c * HALO, (c + 1) * HALO)
            h, carry = _scan_chunk(ra_scr[rs, :], rx_scr[rs, :], carry)
            rx_scr[rs, :] = h
        carry_scr[...] = carry
        hlast_ref[...] = carry

    gb = gb_ref[...]
    def norm_b(r0):
        rows = pl.ds(r0, ROWS)
        out_b = rx_scr[rows, :] * _gelu(z_scr[rows, 2 * DA:2 * DA + DB])
        mix_scr[rows, DA:DA + DB] = _rms(out_b, gb).astype(BF)
    _rows_loop(R, ROWS, norm_b)

    x1_ref[...] = x_ref[...] + _dot(mix_scr[...], wout_ref[...])


def _mixer(streaming, x, states, p):
    rows, D = x.shape
    DA, DB = p["g_v"].shape[1], p["lru_lam"].shape[1]
    weights = [p["norm_mix"], p["w_in"], p["g_v"], p["gmlp_w_eff"], p["gmlp_bias_rows"], p["lru_conv_w"],
               p["lru_conv_b"], p["lru_wa"], p["lru_ba"], p["lru_wx"], p["lru_bx"], p["lru_lam"], p["g_a"],
               p["g_b"], p["w_out"]]
    w_specs = [_const_spec(w.shape) for w in weights]
    W = p["lru_conv_w"].shape[0]
    if streaming:
        pconv, h0 = states
        N = h0.shape[0]
        L = rows // N
        S = SAMPLE_MIX_STREAMS
        R = S * L
        grid = (N // S,)
        row_spec = lambda c: pl.BlockSpec((R, c), lambda i: (i, 0))
        in_specs = [row_spec(D), pl.BlockSpec((S, HALO, DB), lambda i: (i, 0, 0)),
                    pl.BlockSpec((S, 1, DB), lambda i: (i, 0, 0))] + w_specs
        out_specs = [row_spec(D), row_spec(DA), pl.BlockSpec((S, W - 1, DB), lambda i: (i, 0, 0)),
                     pl.BlockSpec((S, 1, DB), lambda i: (i, 0, 0))]
        out_shape = [jax.ShapeDtypeStruct((rows, D), F32), jax.ShapeDtypeStruct((rows, DA), F32),
                     jax.ShapeDtypeStruct((N, W - 1, DB), F32), jax.ShapeDtypeStruct((N, 1, DB), F32)]
        args = [x, pconv, h0] + weights
    else:
        B, T = states
        S, L, R = 1, MIX_ROWS, MIX_ROWS
        nt = T // R
        grid = (B, nt)
        row_spec = lambda c: pl.BlockSpec((R, c), lambda b, t: (b * nt + t, 0))
        in_specs = [row_spec(D)] + w_specs
        out_specs = [row_spec(D), pl.BlockSpec((None, W - 1, DB), lambda b, t: (b, 0, 0)),
                     pl.BlockSpec((None, 1, DB), lambda b, t: (b, 0, 0))]
        out_shape = [jax.ShapeDtypeStruct((rows, D), F32), jax.ShapeDtypeStruct((B, W - 1, DB), F32),
                     jax.ShapeDtypeStruct((B, 1, DB), F32)]
        args = [x] + weights
    G, NCH = p["gmlp_w_eff"].shape[0], p["gmlp_w_eff"].shape[1]
    scratch = [pltpu.VMEM((R, D), BF),
               pltpu.VMEM((R, 2 * DA + DB), F32),
               pltpu.VMEM((R, DA), BF),
               pltpu.VMEM((S, L + HALO, DB), F32),
               pltpu.VMEM((R, DB), F32),
               pltpu.VMEM((R, DB), BF),
               pltpu.VMEM((R, DB), F32),
               pltpu.VMEM((R, DB), F32),
               pltpu.VMEM((R, DA + DB), BF),
               pltpu.VMEM((G, NCH, NCH), BF),
               pltpu.VMEM((1, DB), F32)]
    return pl.pallas_call(
        functools.partial(_mixer_body, streaming, S, L),
        grid=grid, in_specs=in_specs, out_specs=out_specs, out_shape=out_shape, scratch_shapes=scratch,
        compiler_params=_params(len(grid)),
        name="mixer_sample" if streaming else "mixer_prompt",
    )(*args)


def _attend(q_scr, o_scr, s_scr, p_scr, row0, n_rows, keys, values, head_dim):
    n_heads = q_scr.shape[1] // head_dim
    scale = head_dim ** -0.5
    rows = pl.ds(row0, n_rows)
    n_chunk = min(n_rows, SOFTMAX_ROWS)
    heads = [slice(h * head_dim, (h + 1) * head_dim) for h in range(n_heads)]
    for h, cs in enumerate(heads):
        s_scr[h] = lax.dot_general(q_scr[rows, cs], keys(h), (((1,), (1,)), ((), ())),
                                   preferred_element_type=F32) * scale
    for h in range(n_heads):
        for r0 in range(0, n_rows, n_chunk):
            s = s_scr[h, r0:r0 + n_chunk, :]
            e = jnp.exp(s - jnp.max(s, axis=-1, keepdims=True))
            p_scr[h, r0:r0 + n_chunk, :] = (e / jnp.sum(e, axis=-1, keepdims=True)).astype(BF)
    for h, cs in enumerate(heads):
        o_scr[rows, cs] = _dot(p_scr[h], values(h)).astype(BF)


def _project_q(x_ref, nxa_ref, wq_ref, h_scr, q_scr):
    g = nxa_ref[...]
    def norm(r0):
        rows = pl.ds(r0, ROWS)
        h_scr[rows, :] = _rms(x_ref[rows, :], g).astype(BF)
    _rows_loop(x_ref.shape[0], ROWS, norm)
    q_scr[...] = _dot(h_scr[...], wq_ref[...]).astype(BF)


def _attn_prompt_body(head_dim, x_ref, nxa_ref, wq_ref, k_ref, v_ref, wo_ref, o_ref,
                      h_scr, q_scr, a_scr, s_scr, p_scr):
    _project_q(x_ref, nxa_ref, wq_ref, h_scr, q_scr)
    head = lambda h: slice(h * head_dim, (h + 1) * head_dim)
    _attend(q_scr, a_scr, s_scr, p_scr, 0, x_ref.shape[0], lambda h: k_ref[:, head(h)],
            lambda h: v_ref[:, head(h)], head_dim)
    o_ref[...] = x_ref[...] + _dot(a_scr[...], wo_ref[...])


def _attn_sample_body(head_dim, n_streams, n_rows, x_ref, nxa_ref, wq_ref, k_hbm, v_hbm, wo_ref, o_ref,
                      h_scr, q_scr, a_scr, s_scr, p_scr, kv_buf, kv_sem):
    i = pl.program_id(0)
    n_steps = pl.num_programs(0)
    n_heads = q_scr.shape[1] // head_dim
    slot = lax.rem(i, 2)

    def copies(step, dst_slot):
        out = []
        for which, hbm in enumerate((k_hbm, v_hbm)):
            for s in range(n_streams):
                for h in range(n_heads):
                    out.append(pltpu.make_async_copy(hbm.at[step * n_streams + s, :, h, :],
                                                     kv_buf.at[dst_slot, which, s, h],
                                                     kv_sem.at[dst_slot, which, s, h]))
        return out

    @pl.when(i == 0)
    def _():
        for cp in copies(0, 0):
            cp.start()

    @pl.when(i + 1 < n_steps)
    def _():
        for cp in copies(i + 1, 1 - slot):
            cp.start()

    @pl.when(i == 0)
    def _():
        _project_q(x_ref, nxa_ref, wq_ref, h_scr, q_scr)

    for cp in copies(i, slot):
        cp.wait()

    for s in range(n_streams):
        row0 = pl.multiple_of((i * n_streams + s) * n_rows, n_rows)
        _attend(q_scr, a_scr, s_scr.at[s], p_scr.at[s], row0, n_rows,
                lambda h: kv_buf[slot, 0, s, h].astype(BF), lambda h: kv_buf[slot, 1, s, h].astype(BF), head_dim)

    @pl.when(i == n_steps - 1)
    def _():
        o_ref[...] = x_ref[...] + _dot(a_scr[...], wo_ref[...])


def _attn_prompt(x, k, v, p, n_heads, batch):
    rows, D = x.shape
    M = k.shape[0] // batch
    R = ATT_ROWS
    nt = rows // batch // R
    row_spec = pl.BlockSpec((R, D), lambda b, t: (b * nt + t, 0))
    mem_spec = pl.BlockSpec((M, D), lambda b, t: (b, 0))
    return pl.pallas_call(
        functools.partial(_attn_prompt_body, D // n_heads),
        grid=(batch, nt),
        in_specs=[row_spec, _const_spec((1, D)), _const_spec((D, D)), mem_spec, mem_spec, _const_spec((D, D))],
        out_specs=row_spec,
        out_shape=jax.ShapeDtypeStruct((rows, D), F32),
        scratch_shapes=[pltpu.VMEM((R, D), BF), pltpu.VMEM((R, D), BF), pltpu.VMEM((R, D), BF),
                        pltpu.VMEM((n_heads, R, M), F32), pltpu.VMEM((n_heads, R, M), BF)],
        compiler_params=_params(2),
        name="attn_prompt",
    )(x, p["norm_xa"], p["w_q"], k, v, p["w_o"])


def _attn_sample(x, k, v, p):
    rows, D = x.shape
    N, M, n_heads, head_dim = k.shape
    L = rows // N
    S = SAMPLE_ATT_STREAMS
    hbm = pl.BlockSpec(memory_space=pl.ANY)
    return pl.pallas_call(
        functools.partial(_attn_sample_body, head_dim, S, L),
        grid=(N // S,),
        in_specs=[_const_spec((rows, D)), _const_spec((1, D)), _const_spec((D, D)), hbm, hbm,
                  _const_spec((D, D))],
        out_specs=pl.BlockSpec((rows, D), lambda i: (0, 0)),
        out_shape=jax.ShapeDtypeStruct((rows, D), F32),
        scratch_shapes=[pltpu.VMEM((rows, D), BF), pltpu.VMEM((rows, D), BF), pltpu.VMEM((rows, D), BF),
                        pltpu.VMEM((S, n_heads, L, M), F32), pltpu.VMEM((S, n_heads, L, M), BF),
                        pltpu.VMEM((2, 2, S, n_heads, M, head_dim), F32),
                        pltpu.SemaphoreType.DMA((2, 2, S, n_heads))],
        compiler_params=_params(1),
        name="attn_sample",
    )(x, p["norm_xa"], p["w_q"], k, v, p["w_o"])


def _ffn_body(streaming, n_streams, n_rows, *refs):
    S, L = n_streams, n_rows
    R = S * L
    refs = list(refs)
    x_ref = refs.pop(0)
    if streaming:
        st_ref = refs.pop(0)
    (nffn_ref, wua_ref, wug_ref, cwa_ref, cwg_ref, cba_ref, cbg_ref, wd_ref, nfin_ref,
     y_ref, new_ref, h_scr, acc_scr, ya_scr, yg_scr, act_scr, carry_scr) = refs
    W = cwa_ref.shape[0]
    C = wua_ref.shape[1]

    if streaming:
        j = pl.program_id(1)
    else:
        t, j = pl.program_id(1), pl.program_id(2)
    nj = pl.num_programs(2 - int(streaming))

    @pl.when(j == 0)
    def _():
        g = nffn_ref[...]
        def norm(r0):
            rows = pl.ds(r0, ROWS)
            h_scr[rows, :] = _rms(x_ref[rows, :], g).astype(BF)
        _rows_loop(R, ROWS, norm)
        acc_scr[...] = jnp.zeros(acc_scr.shape, F32)
        if not streaming:
            @pl.when(t == 0)
            def _():
                carry_scr[...] = jnp.zeros(carry_scr.shape, F32)

    tail = slice(HALO + L - (W - 1), HALO + L)
    n_chunk = min(L, FFN_CHUNK)
    for c in range(C // FFN_SUB):
        cs = slice(c * FFN_SUB, (c + 1) * FFN_SUB)
        for half, (y_scr, w_ref) in enumerate(((ya_scr, wua_ref), (yg_scr, wug_ref))):
            y_scr[:, HALO:HALO + L, cs] = _dot(h_scr[...], w_ref[:, cs]).reshape(S, L, FFN_SUB)
            if streaming:
                for s in range(S):
                    y_scr[s, HALO - (W - 1):HALO, cs] = st_ref[s, half, :, cs]
                    new_ref[s, half, :, cs] = y_scr[s, tail, cs]
            else:
                y_scr[0, 0:HALO, cs] = carry_scr[j, half, :, cs]
                carry_scr[j, half, :, cs] = y_scr[0, L:L + HALO, cs]
                new_ref[half, :, cs] = y_scr[0, tail, cs]

        taps_a = [jnp.broadcast_to(cwa_ref[k:k + 1, cs], (n_chunk, FFN_SUB)) for k in range(W)]
        taps_g = [jnp.broadcast_to(cwg_ref[k:k + 1, cs], (n_chunk, FFN_SUB)) for k in range(W)]
        bias_a = jnp.broadcast_to(cba_ref[:, cs], (n_chunk, FFN_SUB))
        bias_g = jnp.broadcast_to(cbg_ref[:, cs], (n_chunk, FFN_SUB))
        for s in range(S):
            for r0 in range(0, L, n_chunk):
                ext = slice(r0, r0 + n_chunk + HALO)
                lin = _causal_conv(ya_scr[s, ext, cs], taps_a, bias_a)
                gated = _causal_conv(yg_scr[s, ext, cs], taps_g, bias_g)
                out0 = s * L + r0
                act_scr[out0:out0 + n_chunk, cs] = (_gelu(gated) * lin).astype(BF)

    acc_scr[...] += _dot(act_scr[...], wd_ref[...])

    @pl.when(j == nj - 1)
    def _():
        g = nfin_ref[...]
        def final(r0):
            rows = pl.ds(r0, ROWS)
            y_ref[rows, :] = _rms(x_ref[rows, :] + acc_scr[rows, :], g)
        _rows_loop(R, ROWS, final)


def _ffn(streaming, x, state, p):
    rows, D = x.shape
    DFF = p["w_down"].shape[0]
    W = p["ffn_conv_w"].shape[0]
    C = FFN_COLS
    nj = DFF // C
    if streaming:
        N = state.shape[0]
        L = rows // N
        R = FFN_ROWS
        S = R // L
        grid = (rows // R, nj)
        row_map = lambda i, j: (i, 0)
        colmap = lambda off: (lambda i, j: (0, j + off))
        wd_map = lambda i, j: (j, 0)
        new_spec = pl.BlockSpec((S, 2, W - 1, C), lambda i, j: (i, 0, 0, j))
        new_shape = jax.ShapeDtypeStruct((N, 2, W - 1, DFF), F32)
        in_specs = [pl.BlockSpec((R, D), row_map),
                    pl.BlockSpec((S, 2, W - 1, C), lambda i, j: (i, 0, 0, j))]
        args = [x, state]
    else:
        B, T = state
        S, L, R = 1, FFN_ROWS, FFN_ROWS
        nt = T // R
        grid = (B, nt, nj)
        row_map = lambda b, t, j: (b * nt + t, 0)
        colmap = lambda off: (lambda b, t, j: (0, j + off))
        wd_map = lambda b, t, j: (j, 0)
        new_spec = pl.BlockSpec((None, None, 2, W - 1, C), lambda b, t, j: (b, t, 0, 0, j))
        new_shape = jax.ShapeDtypeStruct((B, nt, 2, W - 1, DFF), F32)
        in_specs = [pl.BlockSpec((R, D), row_map)]
        args = [x]
    in_specs += [_const_spec((1, D)),
                 pl.BlockSpec((D, C), colmap(0)), pl.BlockSpec((D, C), colmap(nj)),
                 pl.BlockSpec((W, C), colmap(0)), pl.BlockSpec((W, C), colmap(nj)),
                 pl.BlockSpec((1, C), colmap(0)), pl.BlockSpec((1, C), colmap(nj)),
                 pl.BlockSpec((C, D), wd_map), _const_spec((1, D))]
    args += [p["norm_ffn"], p["w_up"], p["w_up"], p["ffn_conv_w"], p["ffn_conv_w"], p["ffn_conv_b"],
             p["ffn_conv_b"], p["w_down"], p["norm_final"]]
    scratch = [pltpu.VMEM((R, D), BF),
               pltpu.VMEM((R, D), F32),
               pltpu.VMEM((S, L + HALO, C), F32),
               pltpu.VMEM((S, L + HALO, C), F32),
               pltpu.VMEM((R, C), BF),
               pltpu.VMEM((nj, 2, HALO, C), F32)]
    return pl.pallas_call(
        functools.partial(_ffn_body, streaming, S, L),
        grid=grid, in_specs=in_specs,
        out_specs=[pl.BlockSpec((R, D), row_map), new_spec],
        out_shape=[jax.ShapeDtypeStruct((rows, D), F32), new_shape],
        scratch_shapes=scratch,
        compiler_params=_params(len(grid)),
        name="ffn_sample" if streaming else "ffn_prompt",
    )(*args)


def kernel(x_prompt, x_sample, mem_prompt, cache_mem_k, cache_mem_v, state_lru_h, state_lru_conv, state_ffn_conv, norm_mix, w_in, g_v, gmlp_w, gmlp_b, lru_conv_w, lru_conv_b, lru_wa, lru_ba, lru_wx, lru_bx, lru_lam, g_a, g_b, w_out, norm_mem, w_kv, norm_xa, w_q, w_o, norm_ffn, w_up, ffn_conv_w, ffn_conv_b, w_down, norm_final):
    depth = w_in.shape[0]
    assert depth == 1, "single-layer trunk"
    B, T, D = x_prompt.shape
    N, L, _ = x_sample.shape
    M = mem_prompt.shape[1]
    n_heads, head_dim = cache_mem_k.shape[3], cache_mem_k.shape[4]
    G, NCH = gmlp_w.shape[1], gmlp_w.shape[2]
    DA, DB = g_v.shape[1], lru_lam.shape[1]
    GD = DA // G
    DFF = w_down.shape[1]
    W_LRU, W_FFN = lru_conv_w.shape[1], ffn_conv_w.shape[1]
    assert L <= CAUSAL_CHUNK and NCH % L == 0 and L == ROWS
    row = lambda a: a.reshape(1, -1)

    shared = {
        "norm_mix": row(norm_mix[0]), "w_in": w_in[0].astype(BF), "g_v": row(g_v[0]),
        "lru_conv_w": lru_conv_w[0], "lru_conv_b": row(lru_conv_b[0]),
        "lru_wa": lru_wa[0].astype(BF), "lru_ba": row(lru_ba[0]),
        "lru_wx": lru_wx[0].astype(BF), "lru_bx": row(lru_bx[0]), "lru_lam": row(lru_lam[0]),
        "g_a": row(g_a[0]), "g_b": row(g_b[0]), "w_out": w_out[0].astype(BF),
        "norm_xa": row(norm_xa[0]), "w_q": w_q[0].astype(BF), "w_o": w_o[0].astype(BF),
        "norm_ffn": row(norm_ffn[0]), "w_up": w_up[0].astype(BF), "ffn_conv_w": ffn_conv_w[0],
        "ffn_conv_b": row(ffn_conv_b[0]), "w_down": w_down[0].astype(BF), "norm_final": row(norm_final),
    }
    prompt_p = dict(shared, gmlp_w_eff=gmlp_w[0],
                    gmlp_bias_rows=jnp.repeat(gmlp_b[0].T, GD, axis=1))
    reps = NCH // L
    sample_p = dict(shared, gmlp_w_eff=jnp.tile(gmlp_w[0][:, :L, :L], (1, reps, reps)),
                    gmlp_bias_rows=jnp.repeat(jnp.tile(gmlp_b[0][:, :L].T, (reps, 1)), GD, axis=1))

    mk, mv, mk_b, mv_b = _memory_kv(mem_prompt.reshape(B * M, D), row(norm_mem[0]), w_kv[0].astype(BF))
    xp = x_prompt.reshape(B * T, D)
    xp, conv_p, h_p = _mixer(False, xp, (B, T), prompt_p)
    xp = _attn_prompt(xp, mk_b, mv_b, prompt_p, n_heads, B)
    y_p, ffn_p = _ffn(False, xp, (B, T), prompt_p)

    xs = x_sample.reshape(N * L, D)
    pconv = jnp.pad(state_lru_conv[0], ((0, 0), (HALO - (W_LRU - 1), 0), (0, 0)))
    xs, v_s, conv_s, h_s = _mixer(True, xs, (pconv, state_lru_h[0].reshape(N, 1, DB)), sample_p)
    xs = _attn_sample(xs, cache_mem_k[0], cache_mem_v[0], sample_p)
    ffn_prev = jnp.swapaxes(state_ffn_conv[0].reshape(N, W_FFN - 1, 2, DFF), 1, 2)
    y_s, ffn_s = _ffn(True, xs, ffn_prev, sample_p)

    def ffn_state(a):
        return jnp.swapaxes(a, 1, 2).reshape(1, a.shape[0], W_FFN - 1, 2 * DFF)

    return (y_p.reshape(B, T, D), y_s.reshape(N, L, D),
            mk.reshape(1, B, M, n_heads, head_dim), mv.reshape(1, B, M, n_heads, head_dim),
            h_p.reshape(1, B, DB), conv_p.reshape(1, B, W_LRU - 1, DB), ffn_state(ffn_p[:, -1]),
            h_s.reshape(1, N, DB), conv_s.reshape(1, N, W_LRU - 1, DB), ffn_state(ffn_s),
            v_s.reshape(1, N, L, DA))
```

```python
import functools

import jax
import jax.numpy as jnp
from jax import lax
from jax.experimental import pallas as pl
from jax.experimental.pallas import tpu as pltpu

F32 = jnp.float32
BF = jnp.bfloat16

EPS = 1e-6
LRU_C = 8.0
CAUSAL_CHUNK = 64
HALO = 8
ROWS = 16
LOOP_UNROLL = 4
SOFTMAX_ROWS = 64
MIX_CHUNK = 32
MIX_GROUP = 256
V7X_VMEM_BYTES = 64 * 1024 * 1024
VMEM_LIMIT = V7X_VMEM_BYTES - 8 * 1024 * 1024

MIX_ROWS = 256
ATT_ROWS = 512
FFN_ROWS = 512
FFN_COLS = 1024
FFN_SUB = 256
FFN_CHUNK = 64
KV_COLS = 512
SAMPLE_MIX_STREAMS = 16
SAMPLE_ATT_STREAMS = 2


def _rms(x, g):
    return x * lax.rsqrt(jnp.mean(x * x, axis=-1, keepdims=True) + EPS) * g


def _gelu(x):
    return x * (0.5 * (1.0 + jnp.tanh(0.7978845608028654 * (x + 0.044715 * (x * x * x)))))


def _softplus(x):
    return jnp.maximum(x, 0.0) + jnp.log1p(jnp.exp(-jnp.abs(x)))


def _dot(a, b):
    return jnp.dot(a, b, preferred_element_type=F32)


def _rows_loop(n_rows, chunk, fn, unroll=LOOP_UNROLL):
    def body(i, carry):
        fn(pl.multiple_of(i * chunk, chunk))
        return carry
    trips = n_rows // chunk
    lax.fori_loop(0, trips, body, 0, unroll=min(unroll, trips))


def _causal_conv(ext, taps, bias):
    width = len(taps)
    acc = None
    for k in range(width):
        shift = width - 1 - k
        src = ext if shift == 0 else pltpu.roll(ext, shift, axis=0)
        term = src[HALO:, :] * taps[k]
        acc = term if acc is None else acc + term
    return acc + bias


def _const_spec(shape):
    nd = len(shape)
    return pl.BlockSpec(shape, lambda *_: (0,) * nd, pipeline_mode=pl.Buffered(1))


def _params(n_grid):
    return pltpu.CompilerParams(dimension_semantics=("arbitrary",) * n_grid, vmem_limit_bytes=VMEM_LIMIT)


def _kv_body(mem_ref, g_ref, wk_ref, wv_ref, k_ref, v_ref, kb_ref, vb_ref, h_scr):
    @pl.when(pl.program_id(0) == 0)
    def _():
        g = g_ref[...]
        def norm(r0):
            rows = pl.ds(r0, ROWS)
            h_scr[rows, :] = _rms(mem_ref[rows, :], g).astype(BF)
        _rows_loop(mem_ref.shape[0], ROWS, norm)

    k = _dot(h_scr[...], wk_ref[...])
    k_ref[...] = k
    kb_ref[...] = k.astype(BF)
    v = _dot(h_scr[...], wv_ref[...])
    v_ref[...] = v
    vb_ref[...] = v.astype(BF)


def _memory_kv(mem, g, w_kv):
    m, d = mem.shape
    n_steps = d // KV_COLS
    col = pl.BlockSpec((m, KV_COLS), lambda j: (0, j))
    return pl.pallas_call(
        _kv_body,
        grid=(n_steps,),
        in_specs=[_const_spec((m, d)), _const_spec((1, d)),
                  pl.BlockSpec((d, KV_COLS), lambda j: (0, j)),
                  pl.BlockSpec((d, KV_COLS), lambda j: (0, j + n_steps))],
        out_specs=[col, col, col, col],
        out_shape=[jax.ShapeDtypeStruct((m, d), F32), jax.ShapeDtypeStruct((m, d), F32),
                   jax.ShapeDtypeStruct((m, d), BF), jax.ShapeDtypeStruct((m, d), BF)],
        scratch_shapes=[pltpu.VMEM((m, d), BF)],
        compiler_params=_params(1),
        name="memory_kv",
    )(mem, g, w_kv, w_kv)


def _scan_chunk(a, b, carry):
    n = a.shape[0]
    pos = lax.broadcasted_iota(jnp.int32, a.shape, 0)
    d = 1
    while d < n:
        keep = pos >= d
        a_prev = jnp.where(keep, pltpu.roll(a, d, axis=0), 1.0)
        b_prev = jnp.where(keep, pltpu.roll(b, d, axis=0), 0.0)
        b = a * b_prev + b
        a = a * a_prev
        d *= 2
    h = b + a * carry
    return h, h[n - 1:n, :]


def _mixer_body(streaming, n_streams, n_rows, *refs):
    S, L = n_streams, n_rows
    R = S * L
    refs = list(refs)
    x_ref = refs.pop(0)
    if streaming:
        pconv_ref, h0_ref = refs.pop(0), refs.pop(0)
    (nmix_ref, win_ref, gv_ref, gw_ref, gbias_ref, cw_ref, cb_ref, wa_ref, ba_ref, wx_ref, bx_ref,
     lam_ref, ga_ref, gb_ref, wout_ref) = refs[:15]
    refs = refs[15:]
    x1_ref = refs.pop(0)
    if streaming:
        v_ref = refs.pop(0)
    convnew_ref, hlast_ref = refs.pop(0), refs.pop(0)
    h_scr, z_scr, vb_scr, y_scr, xc_scr, xcb_scr, ra_scr, rx_scr, mix_scr, gwm_scr, carry_scr = refs

    DA = gv_ref.shape[1]
    DB = lam_ref.shape[1]
    G, NCH = gw_ref.shape[0], gw_ref.shape[1]
    GD = DA // G
    H, HD = wa_ref.shape[0], wa_ref.shape[1]
    W = cw_ref.shape[0]

    if streaming:
        first = pl.program_id(0) == 0
    else:
        t = pl.program_id(1)
        first = jnp.logical_and(pl.program_id(0) == 0, t == 0)

    @pl.when(first)
    def _():
        ri = lax.broadcasted_iota(jnp.int32, (NCH, NCH), 0)
        ci = lax.broadcasted_iota(jnp.int32, (NCH, NCH), 1)
        if streaming:
            sh = L.bit_length() - 1
            keep = lax.shift_right_logical(ri, sh) == lax.shift_right_logical(ci, sh)
        else:
            sh = CAUSAL_CHUNK.bit_length() - 1
            keep = lax.shift_right_logical(ri, sh) >= lax.shift_right_logical(ci, sh)
        for g in range(G):
            gwm_scr[g] = jnp.where(keep, gw_ref[g], 0.0).astype(BF)

    if streaming:
        y_scr[:, 0:HALO, :] = pconv_ref[...]
    else:
        @pl.when(t == 0)
        def _():
            y_scr[0, 0:HALO, :] = jnp.zeros((HALO, DB), F32)
            carry_scr[...] = jnp.zeros((1, DB), F32)

    nmix, gv, ga, gb = nmix_ref[...], gv_ref[...], ga_ref[...], gb_ref[...]
    cw, cb = [cw_ref[k:k + 1, :] for k in range(W)], cb_ref[...]
    ba, bx = ba_ref[...], bx_ref[...]
    sp_lam = _softplus(-lam_ref[...])
    carry = None if streaming else carry_scr[...]

    GRP = min(R, MIX_GROUP)
    per_group = GRP // L if streaming else 1
    for grp in range(R // GRP):
        r0 = grp * GRP
        rs = slice(r0, r0 + GRP)
        chunks = [slice(c0, c0 + MIX_CHUNK) for c0 in range(r0, r0 + GRP, MIX_CHUNK)]
        streams = range(grp * per_group, (grp + 1) * per_group)

        for c in chunks:
            h_scr[c, :] = _rms(x_ref[c, :], nmix).astype(BF)
        xr = _dot(h_scr[rs, :], win_ref[:, 2 * DA:2 * DA + DB])
        if streaming:
            y_scr[streams.start:streams.stop, HALO:HALO + L, :] = xr.reshape(per_group, L, DB)
        else:
            y_scr[0, HALO + r0:HALO + r0 + GRP, :] = xr
        z_scr[rs, DA:2 * DA] = _dot(h_scr[rs, :], win_ref[:, DA:2 * DA])

        if streaming:
            conv_in = [(y_scr[s], slice(s * L, (s + 1) * L)) for s in streams]
        else:
            conv_in = [(y_scr[0, c.start:c.stop + HALO, :], c) for c in chunks]
        for ext, c in conv_in:
            xc = _causal_conv(ext, cw, cb)
            xc_scr[c, :] = xc
            xcb_scr[c, :] = xc.astype(BF)
        z_scr[rs, 2 * DA:2 * DA + DB] = _dot(h_scr[rs, :], win_ref[:, 2 * DA + DB:2 * DA + 2 * DB])
        for hh in range(H):
            cs = slice(hh * HD, (hh + 1) * HD)
            ra_scr[rs, cs] = _dot(xcb_scr[rs, cs], wa_ref[hh])
            rx_scr[rs, cs] = _dot(xcb_scr[rs, cs], wx_ref[hh])

        for c in chunks:
            v = _rms(_gelu(z_scr[c, DA:2 * DA]), gv)
            if streaming:
                v_ref[c, :] = v
            vb_scr[c, :] = v.astype(BF)
        z_scr[rs, 0:DA] = _dot(h_scr[rs, :], win_ref[:, 0:DA])

        for c in chunks:
            r = jax.nn.sigmoid(ra_scr[c, :] + ba)
            i = jax.nn.sigmoid(rx_scr[c, :] + bx)
            a = jnp.exp(-LRU_C * r * sp_lam)
            ra_scr[c, :] = a
            rx_scr[c, :] = jnp.sqrt(1.0 - a * a) * (i * xc_scr[c, :])

        for n0 in range(r0, r0 + GRP, NCH):
            ns = slice(n0, n0 + NCH)
            for g in range(G):
                cs = slice(g * GD, (g + 1) * GD)
                sp = _dot(gwm_scr[g], vb_scr[ns, cs])
                z_scr[ns, cs] = _gelu(z_scr[ns, cs]) * (sp + gbias_ref[:, cs])
        for c in chunks:
            mix_scr[c, 0:DA] = _rms(z_scr[c, 0:DA], ga).astype(BF)
        x1_ref[rs, :] = x_ref[rs, :] + _dot(mix_scr[rs, 0:DA], wout_ref[0:DA, :])

        if streaming:
            for s in streams:
                c = slice(s * L, (s + 1) * L)
                h, last = _scan_chunk(ra_scr[c, :], rx_scr[c, :], h0_ref[s])
                rx_scr[c, :] = h
                hlast_ref[s] = last
        else:
            for c0 in range(r0, r0 + GRP, HALO):
                c = slice(c0, c0 + HALO)
                h, carry = _scan_chunk(ra_scr[c, :], rx_scr[c, :], carry)
                rx_scr[c, :] = h
        for c in chunks:
            out_b = rx_scr[c, :] * _gelu(z_scr[c, 2 * DA:2 * DA + DB])
            mix_scr[c, DA:DA + DB] = _rms(out_b, gb).astype(BF)
        x1_ref[rs, :] += _dot(mix_scr[rs, DA:DA + DB], wout_ref[DA:DA + DB, :])

    tail = slice(HALO + L - (W - 1), HALO + L)
    if streaming:
        for s in range(S):
            convnew_ref[s] = y_scr[s, tail, :]
    else:
        convnew_ref[...] = y_scr[0, tail, :]
        y_scr[0, 0:HALO, :] = y_scr[0, L:L + HALO, :]
        carry_scr[...] = carry
        hlast_ref[...] = carry


def _mixer(streaming, x, states, p):
    rows, D = x.shape
    DA, DB = p["g_v"].shape[1], p["lru_lam"].shape[1]
    weights = [p["norm_mix"], p["w_in"], p["g_v"], p["gmlp_w_eff"], p["gmlp_bias_rows"], p["lru_conv_w"],
               p["lru_conv_b"], p["lru_wa"], p["lru_ba"], p["lru_wx"], p["lru_bx"], p["lru_lam"], p["g_a"],
               p["g_b"], p["w_out"]]
    w_specs = [_const_spec(w.shape) for w in weights]
    W = p["lru_conv_w"].shape[0]
    if streaming:
        pconv, h0 = states
        N = h0.shape[0]
        L = rows // N
        S = SAMPLE_MIX_STREAMS
        R = S * L
        grid = (N // S,)
        row_spec = lambda c: pl.BlockSpec((R, c), lambda i: (i, 0))
        in_specs = [row_spec(D), pl.BlockSpec((S, HALO, DB), lambda i: (i, 0, 0)),
                    pl.BlockSpec((S, 1, DB), lambda i: (i, 0, 0))] + w_specs
        out_specs = [row_spec(D), row_spec(DA), pl.BlockSpec((S, W - 1, DB), lambda i: (i, 0, 0)),
                     pl.BlockSpec((S, 1, DB), lambda i: (i, 0, 0))]
        out_shape = [jax.ShapeDtypeStruct((rows, D), F32), jax.ShapeDtypeStruct((rows, DA), F32),
                     jax.ShapeDtypeStruct((N, W - 1, DB), F32), jax.ShapeDtypeStruct((N, 1, DB), F32)]
        args = [x, pconv, h0] + weights
    else:
        B, T = states
        S, L, R = 1, MIX_ROWS, MIX_ROWS
        nt = T // R
        grid = (B, nt)
        row_spec = lambda c: pl.BlockSpec((R, c), lambda b, t: (b * nt + t, 0))
        in_specs = [row_spec(D)] + w_specs
        out_specs = [row_spec(D), pl.BlockSpec((None, W - 1, DB), lambda b, t: (b, 0, 0)),
                     pl.BlockSpec((None, 1, DB), lambda b, t: (b, 0, 0))]
        out_shape = [jax.ShapeDtypeStruct((rows, D), F32), jax.ShapeDtypeStruct((B, W - 1, DB), F32),
                     jax.ShapeDtypeStruct((B, 1, DB), F32)]
        args = [x] + weights
    G, NCH = p["gmlp_w_eff"].shape[0], p["gmlp_w_eff"].shape[1]
    scratch = [pltpu.VMEM((R, D), BF),
               pltpu.VMEM((R, 2 * DA + DB), F32),
               pltpu.VMEM((R, DA), BF),
               pltpu.VMEM((S, L + HALO, DB), F32),
               pltpu.VMEM((R, DB), F32),
               pltpu.VMEM((R, DB), BF),
               pltpu.VMEM((R, DB), F32),
               pltpu.VMEM((R, DB), F32),
               pltpu.VMEM((R, DA + DB), BF),
               pltpu.VMEM((G, NCH, NCH), BF),
               pltpu.VMEM((1, DB), F32)]
    return pl.pallas_call(
        functools.partial(_mixer_body, streaming, S, L),
        grid=grid, in_specs=in_specs, out_specs=out_specs, out_shape=out_shape, scratch_shapes=scratch,
        compiler_params=_params(len(grid)),
        name="mixer_sample" if streaming else "mixer_prompt",
    )(*args)


def _attend(q_scr, o_scr, s_scr, p_scr, row0, n_rows, keys, values, head_dim):
    n_heads = q_scr.shape[1] // head_dim
    scale = head_dim ** -0.5
    rows = pl.ds(row0, n_rows)
    n_chunk = min(n_rows, SOFTMAX_ROWS)
    heads = [slice(h * head_dim, (h + 1) * head_dim) for h in range(n_heads)]
    for h, cs in enumerate(heads):
        s_scr[h] = lax.dot_general(q_scr[rows, cs], keys(h), (((1,), (1,)), ((), ())),
                                   preferred_element_type=F32) * scale
    for h in range(n_heads):
        for r0 in range(0, n_rows, n_chunk):
            s = s_scr[h, r0:r0 + n_chunk, :]
            e = jnp.exp(s - jnp.max(s, axis=-1, keepdims=True))
            p_scr[h, r0:r0 + n_chunk, :] = (e / jnp.sum(e, axis=-1, keepdims=True)).astype(BF)
    for h, cs in enumerate(heads):
        o_scr[rows, cs] = _dot(p_scr[h], values(h)).astype(BF)


def _project_q(x_ref, nxa_ref, wq_ref, h_scr, q_scr):
    g = nxa_ref[...]
    def norm(r0):
        rows = pl.ds(r0, ROWS)
        h_scr[rows, :] = _rms(x_ref[rows, :], g).astype(BF)
    _rows_loop(x_ref.shape[0], ROWS, norm)
    q_scr[...] = _dot(h_scr[...], wq_ref[...]).astype(BF)


def _attn_prompt_body(head_dim, x_ref, nxa_ref, wq_ref, k_ref, v_ref, wo_ref, o_ref,
                      h_scr, q_scr, a_scr, s_scr, p_scr):
    _project_q(x_ref, nxa_ref, wq_ref, h_scr, q_scr)
    head = lambda h: slice(h * head_dim, (h + 1) * head_dim)
    _attend(q_scr, a_scr, s_scr, p_scr, 0, x_ref.shape[0], lambda h: k_ref[:, head(h)],
            lambda h: v_ref[:, head(h)], head_dim)
    o_ref[...] = x_ref[...] + _dot(a_scr[...], wo_ref[...])


def _attn_sample_body(head_dim, n_streams, n_rows, x_ref, nxa_ref, wq_ref, k_hbm, v_hbm, wo_ref, o_ref,
                      h_scr, q_scr, a_scr, s_scr, p_scr, kv_buf, kv_sem):
    i = pl.program_id(0)
    n_steps = pl.num_programs(0)
    n_heads = q_scr.shape[1] // head_dim
    slot = lax.rem(i, 2)

    def copies(step, dst_slot):
        out = []
        for which, hbm in enumerate((k_hbm, v_hbm)):
            for s in range(n_streams):
                for h in range(n_heads):
                    out.append(pltpu.make_async_copy(hbm.at[step * n_streams + s, :, h, :],
                                                     kv_buf.at[dst_slot, which, s, h],
                                                     kv_sem.at[dst_slot, which, s, h]))
        return out

    @pl.when(i == 0)
    def _():
        for cp in copies(0, 0):
            cp.start()

    @pl.when(i + 1 < n_steps)
    def _():
        for cp in copies(i + 1, 1 - slot):
            cp.start()

    @pl.when(i == 0)
    def _():
        _project_q(x_ref, nxa_ref, wq_ref, h_scr, q_scr)

    for cp in copies(i, slot):
        cp.wait()

    for s in range(n_streams):
        row0 = pl.multiple_of((i * n_streams + s) * n_rows, n_rows)
        _attend(q_scr, a_scr, s_scr.at[s], p_scr.at[s], row0, n_rows,
                lambda h: kv_buf[slot, 0, s, h].astype(BF), lambda h: kv_buf[slot, 1, s, h].astype(BF), head_dim)

    @pl.when(i == n_steps - 1)
    def _():
        o_ref[...] = x_ref[...] + _dot(a_scr[...], wo_ref[...])


def _attn_prompt(x, k, v, p, n_heads, batch):
    rows, D = x.shape
    M = k.shape[0] // batch
    R = ATT_ROWS
    nt = rows // batch // R
    row_spec = pl.BlockSpec((R, D), lambda b, t: (b * nt + t, 0))
    mem_spec = pl.BlockSpec((M, D), lambda b, t: (b, 0))
    return pl.pallas_call(
        functools.partial(_attn_prompt_body, D // n_heads),
        grid=(batch, nt),
        in_specs=[row_spec, _const_spec((1, D)), _const_spec((D, D)), mem_spec, mem_spec, _const_spec((D, D))],
        out_specs=row_spec,
        out_shape=jax.ShapeDtypeStruct((rows, D), F32),
        scratch_shapes=[pltpu.VMEM((R, D), BF), pltpu.VMEM((R, D), BF), pltpu.VMEM((R, D), BF),
                        pltpu.VMEM((n_heads, R, M), F32), pltpu.VMEM((n_heads, R, M), BF)],
        compiler_params=_params(2),
        name="attn_prompt",
    )(x, p["norm_xa"], p["w_q"], k, v, p["w_o"])


def _attn_sample(x, k, v, p):
    rows, D = x.shape
    N, M, n_heads, head_dim = k.shape
    L = rows // N
    S = SAMPLE_ATT_STREAMS
    hbm = pl.BlockSpec(memory_space=pl.ANY)
    return pl.pallas_call(
        functools.partial(_attn_sample_body, head_dim, S, L),
        grid=(N // S,),
        in_specs=[_const_spec((rows, D)), _const_spec((1, D)), _const_spec((D, D)), hbm, hbm,
                  _const_spec((D, D))],
        out_specs=pl.BlockSpec((rows, D), lambda i: (0, 0)),
        out_shape=jax.ShapeDtypeStruct((rows, D), F32),
        scratch_shapes=[pltpu.VMEM((rows, D), BF), pltpu.VMEM((rows, D), BF), pltpu.VMEM((rows, D), BF),
                        pltpu.VMEM((S, n_heads, L, M), F32), pltpu.VMEM((S, n_heads, L, M), BF),
                        pltpu.VMEM((2, 2, S, n_heads, M, head_dim), F32),
                        pltpu.SemaphoreType.DMA((2, 2, S, n_heads))],
        compiler_params=_params(1),
        name="attn_sample",
    )(x, p["norm_xa"], p["w_q"], k, v, p["w_o"])


def _ffn_body(streaming, n_streams, n_rows, *refs):
    S, L = n_streams, n_rows
    R = S * L
    refs = list(refs)
    x_ref = refs.pop(0)
    if streaming:
        st_ref = refs.pop(0)
    (nffn_ref, wua_ref, wug_ref, cwa_ref, cwg_ref, cba_ref, cbg_ref, wd_ref, nfin_ref,
     y_ref, new_ref, h_scr, acc_scr, ya_scr, yg_scr, act_scr, carry_scr) = refs
    W = cwa_ref.shape[0]
    C = wua_ref.shape[1]

    if streaming:
        j = pl.program_id(1)
    else:
        t, j = pl.program_id(1), pl.program_id(2)
    nj = pl.num_programs(2 - int(streaming))

    @pl.when(j == 0)
    def _():
        g = nffn_ref[...]
        def norm(r0):
            rows = pl.ds(r0, ROWS)
            h_scr[rows, :] = _rms(x_ref[rows, :], g).astype(BF)
        _rows_loop(R, ROWS, norm)
        acc_scr[...] = jnp.zeros(acc_scr.shape, F32)
        if not streaming:
            @pl.when(t == 0)
            def _():
                carry_scr[...] = jnp.zeros(carry_scr.shape, F32)

    tail = slice(HALO + L - (W - 1), HALO + L)
    n_chunk = min(L, FFN_CHUNK)
    for c in range(C // FFN_SUB):
        cs = slice(c * FFN_SUB, (c + 1) * FFN_SUB)
        for half, (y_scr, w_ref) in enumerate(((ya_scr, wua_ref), (yg_scr, wug_ref))):
            y_scr[:, HALO:HALO + L, cs] = _dot(h_scr[...], w_ref[:, cs]).reshape(S, L, FFN_SUB)
            if streaming:
                for s in range(S):
                    y_scr[s, HALO - (W - 1):HALO, cs] = st_ref[s, half, :, cs]
                    new_ref[s, half, :, cs] = y_scr[s, tail, cs]
            else:
                y_scr[0, 0:HALO, cs] = carry_scr[j, half, :, cs]
                carry_scr[j, half, :, cs] = y_scr[0, L:L + HALO, cs]
                new_ref[half, :, cs] = y_scr[0, tail, cs]

        taps_a = [jnp.broadcast_to(cwa_ref[k:k + 1, cs], (n_chunk, FFN_SUB)) for k in range(W)]
        taps_g = [jnp.broadcast_to(cwg_ref[k:k + 1, cs], (n_chunk, FFN_SUB)) for k in range(W)]
        bias_a = jnp.broadcast_to(cba_ref[:, cs], (n_chunk, FFN_SUB))
        bias_g = jnp.broadcast_to(cbg_ref[:, cs], (n_chunk, FFN_SUB))
        for s in range(S):
            for r0 in range(0, L, n_chunk):
                ext = slice(r0, r0 + n_chunk + HALO)
                lin = _causal_conv(ya_scr[s, ext, cs], taps_a, bias_a)
                gated = _causal_conv(yg_scr[s, ext, cs], taps_g, bias_g)
                out0 = s * L + r0
                act_scr[out0:out0 + n_chunk, cs] = (_gelu(gated) * lin).astype(BF)

    acc_scr[...] += _dot(act_scr[...], wd_ref[...])

    @pl.when(j == nj - 1)
    def _():
        g = nfin_ref[...]
        def final(r0):
            rows = pl.ds(r0, ROWS)
            y_ref[rows, :] = _rms(x_ref[rows, :] + acc_scr[rows, :], g)
        _rows_loop(R, ROWS, final)


def _ffn(streaming, x, state, p):
    rows, D = x.shape
    DFF = p["w_down"].shape[0]
    W = p["ffn_conv_w"].shape[0]
    C = FFN_COLS
    nj = DFF // C
    if streaming:
        N = state.shape[0]
        L = rows // N
        R = FFN_ROWS
        S = R // L
        grid = (rows // R, nj)
        row_map = lambda i, j: (i, 0)
        colmap = lambda off: (lambda i, j: (0, j + off))
        wd_map = lambda i, j: (j, 0)
        new_spec = pl.BlockSpec((S, 2, W - 1, C), lambda i, j: (i, 0, 0, j))
        new_shape = jax.ShapeDtypeStruct((N, 2, W - 1, DFF), F32)
        in_specs = [pl.BlockSpec((R, D), row_map),
                    pl.BlockSpec((S, 2, W - 1, C), lambda i, j: (i, 0, 0, j))]
        args = [x, state]
    else:
        B, T = state
        S, L, R = 1, FFN_ROWS, FFN_ROWS
        nt = T // R
        grid = (B, nt, nj)
        row_map = lambda b, t, j: (b * nt + t, 0)
        colmap = lambda off: (lambda b, t, j: (0, j + off))
        wd_map = lambda b, t, j: (j, 0)
        new_spec = pl.BlockSpec((None, None, 2, W - 1, C), lambda b, t, j: (b, t, 0, 0, j))
        new_shape = jax.ShapeDtypeStruct((B, nt, 2, W - 1, DFF), F32)
        in_specs = [pl.BlockSpec((R, D), row_map)]
        args = [x]
    in_specs += [_const_spec((1, D)),
                 pl.BlockSpec((D, C), colmap(0)), pl.BlockSpec((D, C), colmap(nj)),
                 pl.BlockSpec((W, C), colmap(0)), pl.BlockSpec((W, C), colmap(nj)),
                 pl.BlockSpec((1, C), colmap(0)), pl.BlockSpec((1, C), colmap(nj)),
                 pl.BlockSpec((C, D), wd_map), _const_spec((1, D))]
    args += [p["norm_ffn"], p["w_up"], p["w_up"], p["ffn_conv_w"], p["ffn_conv_w"], p["ffn_conv_b"],
             p["ffn_conv_b"], p["w_down"], p["norm_final"]]
    scratch = [pltpu.VMEM((R, D), BF),
               pltpu.VMEM((R, D), F32),
               pltpu.VMEM((S, L + HALO, C), F32),
               pltpu.VMEM((S, L + HALO, C), F32),
               pltpu.VMEM((R, C), BF),
               pltpu.VMEM((nj, 2, HALO, C), F32)]
    return pl.pallas_call(
        functools.partial(_ffn_body, streaming, S, L),
        grid=grid, in_specs=in_specs,
        out_specs=[pl.BlockSpec((R, D), row_map), new_spec],
        out_shape=[jax.ShapeDtypeStruct((rows, D), F32), new_shape],
        scratch_shapes=scratch,
        compiler_params=_params(len(grid)),
        name="ffn_sample" if streaming else "ffn_prompt",
    )(*args)


def kernel(x_prompt, x_sample, mem_prompt, cache_mem_k, cache_mem_v, state_lru_h, state_lru_conv, state_ffn_conv, norm_mix, w_in, g_v, gmlp_w, gmlp_b, lru_conv_w, lru_conv_b, lru_wa, lru_ba, lru_wx, lru_bx, lru_lam, g_a, g_b, w_out, norm_mem, w_kv, norm_xa, w_q, w_o, norm_ffn, w_up, ffn_conv_w, ffn_conv_b, w_down, norm_final):
    depth = w_in.shape[0]
    assert depth == 1, "single-layer trunk"
    B, T, D = x_prompt.shape
    N, L, _ = x_sample.shape
    M = mem_prompt.shape[1]
    n_heads, head_dim = cache_mem_k.shape[3], cache_mem_k.shape[4]
    G, NCH = gmlp_w.shape[1], gmlp_w.shape[2]
    DA, DB = g_v.shape[1], lru_lam.shape[1]
    GD = DA // G
    DFF = w_down.shape[1]
    W_LRU, W_FFN = lru_conv_w.shape[1], ffn_conv_w.shape[1]
    assert L <= CAUSAL_CHUNK and NCH % L == 0 and L == ROWS
    row = lambda a: a.reshape(1, -1)

    shared = {
        "norm_mix": row(norm_mix[0]), "w_in": w_in[0].astype(BF), "g_v": row(g_v[0]),
        "lru_conv_w": lru_conv_w[0], "lru_conv_b": row(lru_conv_b[0]),
        "lru_wa": lru_wa[0].astype(BF), "lru_ba": row(lru_ba[0]),
        "lru_wx": lru_wx[0].astype(BF), "lru_bx": row(lru_bx[0]), "lru_lam": row(lru_lam[0]),
        "g_a": row(g_a[0]), "g_b": row(g_b[0]), "w_out": w_out[0].astype(BF),
        "norm_xa": row(norm_xa[0]), "w_q": w_q[0].astype(BF), "w_o": w_o[0].astype(BF),
        "norm_ffn": row(norm_ffn[0]), "w_up": w_up[0].astype(BF), "ffn_conv_w": ffn_conv_w[0],
        "ffn_conv_b": row(ffn_conv_b[0]), "w_down": w_down[0].astype(BF), "norm_final": row(norm_final),
    }
    prompt_p = dict(shared, gmlp_w_eff=gmlp_w[0],
                    gmlp_bias_rows=jnp.repeat(gmlp_b[0].T, GD, axis=1))
    reps = NCH // L
    sample_p = dict(shared, gmlp_w_eff=jnp.tile(gmlp_w[0][:, :L, :L], (1, reps, reps)),
                    gmlp_bias_rows=jnp.repeat(jnp.tile(gmlp_b[0][:, :L].T, (reps, 1)), GD, axis=1))

    mk, mv, mk_b, mv_b = _memory_kv(mem_prompt.reshape(B * M, D), row(norm_mem[0]), w_kv[0].astype(BF))
    xp = x_prompt.reshape(B * T, D)
    xp, conv_p, h_p = _mixer(False, xp, (B, T), prompt_p)
    xp = _attn_prompt(xp, mk_b, mv_b, prompt_p, n_heads, B)
    y_p, ffn_p = _ffn(False, xp, (B, T), prompt_p)

    xs = x_sample.reshape(N * L, D)
    pconv = jnp.pad(state_lru_conv[0], ((0, 0), (HALO - (W_LRU - 1), 0), (0, 0)))
    xs, v_s, conv_s, h_s = _mixer(True, xs, (pconv, state_lru_h[0].reshape(N, 1, DB)), sample_p)
    xs = _attn_sample(xs, cache_mem_k[0], cache_mem_v[0], sample_p)
    ffn_prev = jnp.swapaxes(state_ffn_conv[0].reshape(N, W_FFN - 1, 2, DFF), 1, 2)
    y_s, ffn_s = _ffn(True, xs, ffn_prev, sample_p)

    def ffn_state(a):
        return jnp.swapaxes(a, 1, 2).reshape(1, a.shape[0], W_FFN - 1, 2 * DFF)

    return (y_p.reshape(B, T, D), y_s.reshape(N, L, D),
            mk.reshape(1, B, M, n_heads, head_dim), mv.reshape(1, B, M, n_heads, head_dim),
            h_p.reshape(1, B, DB), conv_p.reshape(1, B, W_LRU - 1, DB), ffn_state(ffn_p[:, -1]),
            h_s.reshape(1, N, DB), conv_s.reshape(1, N, W_LRU - 1, DB), ffn_state(ffn_s),
            v_s.reshape(1, N, L, DA))
```

```python
import functools

import jax
import jax.numpy as jnp
from jax import lax
from jax.experimental import pallas as pl
from jax.experimental.pallas import tpu as pltpu

F32 = jnp.float32
BF = jnp.bfloat16

EPS = 1e-6
LRU_C = 8.0
CAUSAL_CHUNK = 64
HALO = 8
ROWS = 16
LOOP_UNROLL = 4
SOFTMAX_ROWS = 64
MIX_CHUNK = 32
MIX_GROUP = 256
V7X_VMEM_BYTES = 64 * 1024 * 1024
VMEM_LIMIT = V7X_VMEM_BYTES - 8 * 1024 * 1024

MIX_ROWS = 256
ATT_ROWS = 512
FFN_ROWS = 512
FFN_COLS = 1024
FFN_SAMPLE_COLS = 256
FFN_SUB = 256
FFN_CHUNK = 64
KV_COLS = 256
SAMPLE_MIX_STREAMS = 16
SAMPLE_ATT_STREAMS = 2


def _rms(x, g):
    return x * lax.rsqrt(jnp.mean(x * x, axis=-1, keepdims=True) + EPS) * g


def _gelu(x):
    return x * (0.5 * (1.0 + jnp.tanh(0.7978845608028654 * (x + 0.044715 * (x * x * x)))))


def _softplus(x):
    return jnp.maximum(x, 0.0) + jnp.log1p(jnp.exp(-jnp.abs(x)))


def _dot(a, b):
    return jnp.dot(a, b, preferred_element_type=F32)


def _rows_loop(n_rows, chunk, fn, unroll=LOOP_UNROLL):
    def body(i, carry):
        fn(pl.multiple_of(i * chunk, chunk))
        return carry
    trips = n_rows // chunk
    lax.fori_loop(0, trips, body, 0, unroll=min(unroll, trips))


def _causal_conv(ext, taps, bias):
    width = len(taps)
    acc = None
    for k in range(width):
        shift = width - 1 - k
        src = ext if shift == 0 else pltpu.roll(ext, shift, axis=0)
        term = src[HALO:, :] * taps[k]
        acc = term if acc is None else acc + term
    return acc + bias


def _const_spec(shape):
    nd = len(shape)
    return pl.BlockSpec(shape, lambda *_: (0,) * nd, pipeline_mode=pl.Buffered(1))


def _params(n_grid):
    return pltpu.CompilerParams(dimension_semantics=("arbitrary",) * n_grid, vmem_limit_bytes=VMEM_LIMIT)


def _kv_body(n_casts, mem_ref, g_ref, wk_ref, wv_ref, *refs):
    cast_in, refs = refs[:n_casts], refs[n_casts:]
    k_ref, v_ref, kb_ref, vb_ref = refs[:4]
    cast_out, h_scr = refs[4:4 + n_casts], refs[4 + n_casts]

    @pl.when(pl.program_id(0) == 0)
    def _():
        g = g_ref[...]
        def norm(r0):
            rows = pl.ds(r0, ROWS)
            h_scr[rows, :] = _rms(mem_ref[rows, :], g).astype(BF)
        _rows_loop(mem_ref.shape[0], ROWS, norm)

    k = _dot(h_scr[...], wk_ref[...].astype(BF))
    k_ref[...] = k
    kb_ref[...] = k.astype(BF)
    v = _dot(h_scr[...], wv_ref[...].astype(BF))
    v_ref[...] = v
    vb_ref[...] = v.astype(BF)
    for src, dst in zip(cast_in, cast_out):
        dst[...] = src[...].astype(BF)


def _memory_kv(mem, g, w_kv, resident):
    m, d = mem.shape
    n_steps = d // KV_COLS
    col = pl.BlockSpec((m, KV_COLS), lambda j: (0, j))
    cast_specs = [pl.BlockSpec((w.shape[0] // n_steps, w.shape[1]), lambda j: (j, 0)) for w in resident]
    outs = pl.pallas_call(
        functools.partial(_kv_body, len(resident)),
        grid=(n_steps,),
        in_specs=[_const_spec((m, d)), _const_spec((1, d)),
                  pl.BlockSpec((d, KV_COLS), lambda j: (0, j)),
                  pl.BlockSpec((d, KV_COLS), lambda j: (0, j + n_steps))] + cast_specs,
        out_specs=[col, col, col, col] + cast_specs,
        out_shape=[jax.ShapeDtypeStruct((m, d), F32), jax.ShapeDtypeStruct((m, d), F32),
                   jax.ShapeDtypeStruct((m, d), BF), jax.ShapeDtypeStruct((m, d), BF)]
                  + [jax.ShapeDtypeStruct(w.shape, BF) for w in resident],
        scratch_shapes=[pltpu.VMEM((m, d), BF)],
        compiler_params=_params(1),
        name="memory_kv",
    )(mem, g, w_kv, w_kv, *resident)
    return outs[:4], outs[4:]


def _scan_chunk(a, b, carry):
    n = a.shape[0]
    pos = lax.broadcasted_iota(jnp.int32, a.shape, 0)
    d = 1
    while d < n:
        keep = pos >= d
        a_prev = jnp.where(keep, pltpu.roll(a, d, axis=0), 1.0)
        b_prev = jnp.where(keep, pltpu.roll(b, d, axis=0), 0.0)
        b = a * b_prev + b
        a = a * a_prev
        d *= 2
    h = b + a * carry
    return h, h[n - 1:n, :]


def _mixer_body(streaming, n_streams, n_rows, *refs):
    S, L = n_streams, n_rows
    R = S * L
    refs = list(refs)
    x_ref = refs.pop(0)
    if streaming:
        pconv_ref, h0_ref = refs.pop(0), refs.pop(0)
    (nmix_ref, win_ref, gv_ref, gw_ref, gbias_ref, cw_ref, cb_ref, wa_ref, ba_ref, wx_ref, bx_ref,
     lam_ref, ga_ref, gb_ref, wout_ref) = refs[:15]
    refs = refs[15:]
    x1_ref = refs.pop(0)
    if streaming:
        v_ref = refs.pop(0)
    convnew_ref, hlast_ref = refs.pop(0), refs.pop(0)
    h_scr, z_scr, vb_scr, y_scr, xc_scr, xcb_scr, ra_scr, rx_scr, mix_scr, gwm_scr, carry_scr = refs

    DA = gv_ref.shape[1]
    DB = lam_ref.shape[1]
    G, NCH = gw_ref.shape[0], gw_ref.shape[1]
    GD = DA // G
    H, HD = wa_ref.shape[0], wa_ref.shape[1]
    W = cw_ref.shape[0]

    if streaming:
        first = pl.program_id(0) == 0
    else:
        t = pl.program_id(1)
        first = jnp.logical_and(pl.program_id(0) == 0, t == 0)

    @pl.when(first)
    def _():
        ri = lax.broadcasted_iota(jnp.int32, (NCH, NCH), 0)
        ci = lax.broadcasted_iota(jnp.int32, (NCH, NCH), 1)
        if streaming:
            sh = L.bit_length() - 1
            keep = lax.shift_right_logical(ri, sh) == lax.shift_right_logical(ci, sh)
        else:
            sh = CAUSAL_CHUNK.bit_length() - 1
            keep = lax.shift_right_logical(ri, sh) >= lax.shift_right_logical(ci, sh)
        for g in range(G):
            gwm_scr[g] = jnp.where(keep, gw_ref[g], 0.0).astype(BF)

    if streaming:
        y_scr[:, 0:HALO, :] = pconv_ref[...]
    else:
        @pl.when(t == 0)
        def _():
            y_scr[0, 0:HALO, :] = jnp.zeros((HALO, DB), F32)
            carry_scr[...] = jnp.zeros((1, DB), F32)

    nmix, gv, ga, gb = nmix_ref[...], gv_ref[...], ga_ref[...], gb_ref[...]
    cw, cb = [cw_ref[k:k + 1, :] for k in range(W)], cb_ref[...]
    ba, bx = ba_ref[...], bx_ref[...]
    sp_lam = _softplus(-lam_ref[...])
    carry = None if streaming else carry_scr[...]

    GRP = min(R, MIX_GROUP)
    per_group = GRP // L if streaming else 1
    for grp in range(R // GRP):
        r0 = grp * GRP
        rs = slice(r0, r0 + GRP)
        chunks = [slice(c0, c0 + MIX_CHUNK) for c0 in range(r0, r0 + GRP, MIX_CHUNK)]
        streams = range(grp * per_group, (grp + 1) * per_group)

        for c in chunks:
            h_scr[c, :] = _rms(x_ref[c, :], nmix).astype(BF)
        xr = _dot(h_scr[rs, :], win_ref[:, 2 * DA:2 * DA + DB])
        if streaming:
            y_scr[streams.start:streams.stop, HALO:HALO + L, :] = xr.reshape(per_group, L, DB)
        else:
            y_scr[0, HALO + r0:HALO + r0 + GRP, :] = xr
        z_scr[rs, DA:2 * DA] = _dot(h_scr[rs, :], win_ref[:, DA:2 * DA])

        if streaming:
            conv_in = [(y_scr[s], slice(s * L, (s + 1) * L)) for s in streams]
        else:
            conv_in = [(y_scr[0, c.start:c.stop + HALO, :], c) for c in chunks]
        for ext, c in conv_in:
            xc = _causal_conv(ext, cw, cb)
            xc_scr[c, :] = xc
            xcb_scr[c, :] = xc.astype(BF)
        z_scr[rs, 2 * DA:2 * DA + DB] = _dot(h_scr[rs, :], win_ref[:, 2 * DA + DB:2 * DA + 2 * DB])
        for hh in range(H):
            cs = slice(hh * HD, (hh + 1) * HD)
            ra_scr[rs, cs] = _dot(xcb_scr[rs, cs], wa_ref[hh].astype(BF))
            rx_scr[rs, cs] = _dot(xcb_scr[rs, cs], wx_ref[hh].astype(BF))

        for c in chunks:
            v = _rms(_gelu(z_scr[c, DA:2 * DA]), gv)
            if streaming:
                v_ref[c, :] = v
            vb_scr[c, :] = v.astype(BF)
        z_scr[rs, 0:DA] = _dot(h_scr[rs, :], win_ref[:, 0:DA])

        for c in chunks:
            r = jax.nn.sigmoid(ra_scr[c, :] + ba)
            i = jax.nn.sigmoid(rx_scr[c, :] + bx)
            a = jnp.exp(-LRU_C * r * sp_lam)
            ra_scr[c, :] = a
            rx_scr[c, :] = jnp.sqrt(1.0 - a * a) * (i * xc_scr[c, :])

        for n0 in range(r0, r0 + GRP, NCH):
            ns = slice(n0, n0 + NCH)
            for g in range(G):
                cs = slice(g * GD, (g + 1) * GD)
                sp = _dot(gwm_scr[g], vb_scr[ns, cs])
                z_scr[ns, cs] = _gelu(z_scr[ns, cs]) * (sp + gbias_ref[:, cs])
        for c in chunks:
            mix_scr[c, 0:DA] = _rms(z_scr[c, 0:DA], ga).astype(BF)
        x1_ref[rs, :] = x_ref[rs, :] + _dot(mix_scr[rs, 0:DA], wout_ref[0:DA, :])

        if streaming:
            for s in streams:
                c = slice(s * L, (s + 1) * L)
                h, last = _scan_chunk(ra_scr[c, :], rx_scr[c, :], h0_ref[s])
                rx_scr[c, :] = h
                hlast_ref[s] = last
        else:
            for c0 in range(r0, r0 + GRP, HALO):
                c = slice(c0, c0 + HALO)
                h, carry = _scan_chunk(ra_scr[c, :], rx_scr[c, :], carry)
                rx_scr[c, :] = h
        for c in chunks:
            out_b = rx_scr[c, :] * _gelu(z_scr[c, 2 * DA:2 * DA + DB])
            mix_scr[c, DA:DA + DB] = _rms(out_b, gb).astype(BF)
        x1_ref[rs, :] += _dot(mix_scr[rs, DA:DA + DB], wout_ref[DA:DA + DB, :])

    tail = slice(HALO + L - (W - 1), HALO + L)
    if streaming:
        for s in range(S):
            convnew_ref[s] = y_scr[s, tail, :]
    else:
        convnew_ref[...] = y_scr[0, tail, :]
        y_scr[0, 0:HALO, :] = y_scr[0, L:L + HALO, :]
        carry_scr[...] = carry
        hlast_ref[...] = carry


def _mixer(streaming, x, states, p):
    rows, D = x.shape
    DA, DB = p["g_v"].shape[1], p["lru_lam"].shape[1]
    weights = [p["norm_mix"], p["w_in"], p["g_v"], p["gmlp_w_eff"], p["gmlp_bias_rows"], p["lru_conv_w"],
               p["lru_conv_b"], p["lru_wa"], p["lru_ba"], p["lru_wx"], p["lru_bx"], p["lru_lam"], p["g_a"],
               p["g_b"], p["w_out"]]
    w_specs = [_const_spec(w.shape) for w in weights]
    W = p["lru_conv_w"].shape[0]
    if streaming:
        pconv, h0 = states
        N = h0.shape[0]
        L = rows // N
        S = SAMPLE_MIX_STREAMS
        R = S * L
        grid = (N // S,)
        row_spec = lambda c: pl.BlockSpec((R, c), lambda i: (i, 0))
        in_specs = [row_spec(D), pl.BlockSpec((S, HALO, DB), lambda i: (i, 0, 0)),
                    pl.BlockSpec((S, 1, DB), lambda i: (i, 0, 0))] + w_specs
        out_specs = [row_spec(D), row_spec(DA), pl.BlockSpec((S, W - 1, DB), lambda i: (i, 0, 0)),
                     pl.BlockSpec((S, 1, DB), lambda i: (i, 0, 0))]
        out_shape = [jax.ShapeDtypeStruct((rows, D), F32), jax.ShapeDtypeStruct((rows, DA), F32),
                     jax.ShapeDtypeStruct((N, W - 1, DB), F32), jax.ShapeDtypeStruct((N, 1, DB), F32)]
        args = [x, pconv, h0] + weights
    else:
        B, T = states
        S, L, R = 1, MIX_ROWS, MIX_ROWS
        nt = T // R
        grid = (B, nt)
        row_spec = lambda c: pl.BlockSpec((R, c), lambda b, t: (b * nt + t, 0))
        in_specs = [row_spec(D)] + w_specs
        out_specs = [row_spec(D), pl.BlockSpec((None, W - 1, DB), lambda b, t: (b, 0, 0)),
                     pl.BlockSpec((None, 1, DB), lambda b, t: (b, 0, 0))]
        out_shape = [jax.ShapeDtypeStruct((rows, D), F32), jax.ShapeDtypeStruct((B, W - 1, DB), F32),
                     jax.ShapeDtypeStruct((B, 1, DB), F32)]
        args = [x] + weights
    G, NCH = p["gmlp_w_eff"].shape[0], p["gmlp_w_eff"].shape[1]
    scratch = [pltpu.VMEM((R, D), BF),
               pltpu.VMEM((R, 2 * DA + DB), F32),
               pltpu.VMEM((R, DA), BF),
               pltpu.VMEM((S, L + HALO, DB), F32),
               pltpu.VMEM((R, DB), F32),
               pltpu.VMEM((R, DB), BF),
               pltpu.VMEM((R, DB), F32),
               pltpu.VMEM((R, DB), F32),
               pltpu.VMEM((R, DA + DB), BF),
               pltpu.VMEM((G, NCH, NCH), BF),
               pltpu.VMEM((1, DB), F32)]
    return pl.pallas_call(
        functools.partial(_mixer_body, streaming, S, L),
        grid=grid, in_specs=in_specs, out_specs=out_specs, out_shape=out_shape, scratch_shapes=scratch,
        compiler_params=_params(len(grid)),
        name="mixer_sample" if streaming else "mixer_prompt",
    )(*args)


def _attend(q_scr, o_scr, s_scr, p_scr, row0, n_rows, keys, values, head_dim):
    n_heads = q_scr.shape[1] // head_dim
    scale = head_dim ** -0.5
    rows = pl.ds(row0, n_rows)
    n_chunk = min(n_rows, SOFTMAX_ROWS)
    heads = [slice(h * head_dim, (h + 1) * head_dim) for h in range(n_heads)]
    for h, cs in enumerate(heads):
        s_scr[h] = lax.dot_general(q_scr[rows, cs], keys(h), (((1,), (1,)), ((), ())),
                                   preferred_element_type=F32) * scale
    for h in range(n_heads):
        for r0 in range(0, n_rows, n_chunk):
            s = s_scr[h, r0:r0 + n_chunk, :]
            e = jnp.exp(s - jnp.max(s, axis=-1, keepdims=True))
            p_scr[h, r0:r0 + n_chunk, :] = (e / jnp.sum(e, axis=-1, keepdims=True)).astype(BF)
    for h, cs in enumerate(heads):
        o_scr[rows, cs] = _dot(p_scr[h], values(h)).astype(BF)


def _project_q(x_ref, nxa_ref, wq_ref, h_scr, q_scr):
    g = nxa_ref[...]
    def norm(r0):
        rows = pl.ds(r0, ROWS)
        h_scr[rows, :] = _rms(x_ref[rows, :], g).astype(BF)
    _rows_loop(x_ref.shape[0], ROWS, norm)
    q_scr[...] = _dot(h_scr[...], wq_ref[...]).astype(BF)


def _attn_prompt_body(head_dim, x_ref, nxa_ref, wq_ref, k_ref, v_ref, wo_ref, o_ref,
                      h_scr, q_scr, a_scr, s_scr, p_scr):
    _project_q(x_ref, nxa_ref, wq_ref, h_scr, q_scr)
    head = lambda h: slice(h * head_dim, (h + 1) * head_dim)
    _attend(q_scr, a_scr, s_scr, p_scr, 0, x_ref.shape[0], lambda h: k_ref[:, head(h)],
            lambda h: v_ref[:, head(h)], head_dim)
    o_ref[...] = x_ref[...] + _dot(a_scr[...], wo_ref[...])


def _attn_sample_body(head_dim, n_streams, n_rows, x_ref, nxa_ref, wq_ref, k_hbm, v_hbm, wo_ref, o_ref,
                      h_scr, q_scr, a_scr, s_scr, p_scr, kv_buf, kv_sem):
    i = pl.program_id(0)
    n_steps = pl.num_programs(0)
    n_heads = q_scr.shape[1] // head_dim
    slot = lax.rem(i, 2)

    def copies(step, dst_slot):
        out = []
        for which, hbm in enumerate((k_hbm, v_hbm)):
            for s in range(n_streams):
                for h in range(n_heads):
                    out.append(pltpu.make_async_copy(hbm.at[step * n_streams + s, :, h, :],
                                                     kv_buf.at[dst_slot, which, s, h],
                                                     kv_sem.at[dst_slot, which, s, h]))
        return out

    @pl.when(i == 0)
    def _():
        for cp in copies(0, 0):
            cp.start()

    @pl.when(i + 1 < n_steps)
    def _():
        for cp in copies(i + 1, 1 - slot):
            cp.start()

    @pl.when(i == 0)
    def _():
        _project_q(x_ref, nxa_ref, wq_ref, h_scr, q_scr)

    for cp in copies(i, slot):
        cp.wait()

    for s in range(n_streams):
        row0 = pl.multiple_of((i * n_streams + s) * n_rows, n_rows)
        _attend(q_scr, a_scr, s_scr.at[s], p_scr.at[s], row0, n_rows,
                lambda h: kv_buf[slot, 0, s, h].astype(BF), lambda h: kv_buf[slot, 1, s, h].astype(BF), head_dim)

    @pl.when(i == n_steps - 1)
    def _():
        o_ref[...] = x_ref[...] + _dot(a_scr[...], wo_ref[...])


def _attn_prompt(x, k, v, p, n_heads, batch):
    rows, D = x.shape
    M = k.shape[0] // batch
    R = ATT_ROWS
    nt = rows // batch // R
    row_spec = pl.BlockSpec((R, D), lambda b, t: (b * nt + t, 0))
    mem_spec = pl.BlockSpec((M, D), lambda b, t: (b, 0))
    return pl.pallas_call(
        functools.partial(_attn_prompt_body, D // n_heads),
        grid=(batch, nt),
        in_specs=[row_spec, _const_spec((1, D)), _const_spec((D, D)), mem_spec, mem_spec, _const_spec((D, D))],
        out_specs=row_spec,
        out_shape=jax.ShapeDtypeStruct((rows, D), F32),
        scratch_shapes=[pltpu.VMEM((R, D), BF), pltpu.VMEM((R, D), BF), pltpu.VMEM((R, D), BF),
                        pltpu.VMEM((n_heads, R, M), F32), pltpu.VMEM((n_heads, R, M), BF)],
        compiler_params=_params(2),
        name="attn_prompt",
    )(x, p["norm_xa"], p["w_q"], k, v, p["w_o"])


def _attn_sample(x, k, v, p):
    rows, D = x.shape
    N, M, n_heads, head_dim = k.shape
    L = rows // N
    S = SAMPLE_ATT_STREAMS
    hbm = pl.BlockSpec(memory_space=pl.ANY)
    return pl.pallas_call(
        functools.partial(_attn_sample_body, head_dim, S, L),
        grid=(N // S,),
        in_specs=[_const_spec((rows, D)), _const_spec((1, D)), _const_spec((D, D)), hbm, hbm,
                  _const_spec((D, D))],
        out_specs=pl.BlockSpec((rows, D), lambda i: (0, 0)),
        out_shape=jax.ShapeDtypeStruct((rows, D), F32),
        scratch_shapes=[pltpu.VMEM((rows, D), BF), pltpu.VMEM((rows, D), BF), pltpu.VMEM((rows, D), BF),
                        pltpu.VMEM((S, n_heads, L, M), F32), pltpu.VMEM((S, n_heads, L, M), BF),
                        pltpu.VMEM((2, 2, S, n_heads, M, head_dim), F32),
                        pltpu.SemaphoreType.DMA((2, 2, S, n_heads))],
        compiler_params=_params(1),
        name="attn_sample",
    )(x, p["norm_xa"], p["w_q"], k, v, p["w_o"])


def _ffn_body(streaming, n_streams, n_rows, *refs):
    S, L = n_streams, n_rows
    R = S * L
    refs = list(refs)
    x_ref = refs.pop(0)
    if streaming:
        st_ref = refs.pop(0)
    (nffn_ref, wua_ref, wug_ref, cwa_ref, cwg_ref, cba_ref, cbg_ref, wd_ref, nfin_ref) = refs[:9]
    refs = refs[9:]
    y_ref, new_ref = refs.pop(0), refs.pop(0)
    if streaming:
        casts = ((wua_ref, refs.pop(0)), (wug_ref, refs.pop(0)), (wd_ref, refs.pop(0)))
    h_scr, acc_scr, ya_scr, yg_scr, act_scr, carry_scr = refs
    W = cwa_ref.shape[0]
    C = wua_ref.shape[1]

    if streaming:
        j = pl.program_id(1)
    else:
        t, j = pl.program_id(1), pl.program_id(2)
    nj = pl.num_programs(2 - int(streaming))

    @pl.when(j == 0)
    def _():
        g = nffn_ref[...]
        def norm(r0):
            rows = pl.ds(r0, ROWS)
            h_scr[rows, :] = _rms(x_ref[rows, :], g).astype(BF)
        _rows_loop(R, ROWS, norm)
        acc_scr[...] = jnp.zeros(acc_scr.shape, F32)
        if not streaming:
            @pl.when(t == 0)
            def _():
                carry_scr[...] = jnp.zeros(carry_scr.shape, F32)

    if streaming:
        for src, dst in casts:
            dst[...] = src[...].astype(BF)
        (_, wua_ref), (_, wug_ref), (_, wd_ref) = casts

    tail = slice(HALO + L - (W - 1), HALO + L)
    n_chunk = min(L, FFN_CHUNK)
    for c in range(C // FFN_SUB):
        cs = slice(c * FFN_SUB, (c + 1) * FFN_SUB)
        for half, (y_scr, w_ref) in enumerate(((ya_scr, wua_ref), (yg_scr, wug_ref))):
            y_scr[:, HALO:HALO + L, cs] = _dot(h_scr[...], w_ref[:, cs]).reshape(S, L, FFN_SUB)
            if streaming:
                for s in range(S):
                    y_scr[s, HALO - (W - 1):HALO, cs] = st_ref[s, half, :, cs]
                    new_ref[s, half, :, cs] = y_scr[s, tail, cs]
            else:
                y_scr[0, 0:HALO, cs] = carry_scr[j, half, :, cs]
                carry_scr[j, half, :, cs] = y_scr[0, L:L + HALO, cs]
                new_ref[half, :, cs] = y_scr[0, tail, cs]

        taps_a = [jnp.broadcast_to(cwa_ref[k:k + 1, cs], (n_chunk, FFN_SUB)) for k in range(W)]
        taps_g = [jnp.broadcast_to(cwg_ref[k:k + 1, cs], (n_chunk, FFN_SUB)) for k in range(W)]
        bias_a = jnp.broadcast_to(cba_ref[:, cs], (n_chunk, FFN_SUB))
        bias_g = jnp.broadcast_to(cbg_ref[:, cs], (n_chunk, FFN_SUB))
        for s in range(S):
            for r0 in range(0, L, n_chunk):
                ext = slice(r0, r0 + n_chunk + HALO)
                lin = _causal_conv(ya_scr[s, ext, cs], taps_a, bias_a)
                gated = _causal_conv(yg_scr[s, ext, cs], taps_g, bias_g)
                out0 = s * L + r0
                act_scr[out0:out0 + n_chunk, cs] = (_gelu(gated) * lin).astype(BF)

    acc_scr[...] += _dot(act_scr[...], wd_ref[...])

    @pl.when(j == nj - 1)
    def _():
        g = nfin_ref[...]
        def final(r0):
            rows = pl.ds(r0, ROWS)
            y_ref[rows, :] = _rms(x_ref[rows, :] + acc_scr[rows, :], g)
        _rows_loop(R, ROWS, final)


def _ffn(streaming, x, state, p, w_up_lin, w_up_gated, w_down):
    rows, D = x.shape
    DFF = w_down.shape[0]
    W = p["ffn_conv_w"].shape[0]
    C = FFN_SAMPLE_COLS if streaming else FFN_COLS
    nj = DFF // C
    (wua, off_a), (wug, off_g) = w_up_lin, w_up_gated
    if streaming:
        N = state.shape[0]
        L = rows // N
        R = FFN_ROWS
        S = R // L
        grid = (rows // R, nj)
        row_map = lambda i, j: (i, 0)
        colmap = lambda off: (lambda i, j: (0, j + off))
        wd_map = lambda i, j: (j, 0)
        new_spec = pl.BlockSpec((S, 2, W - 1, C), lambda i, j: (i, 0, 0, j))
        new_shape = jax.ShapeDtypeStruct((N, 2, W - 1, DFF), F32)
        in_specs = [pl.BlockSpec((R, D), row_map),
                    pl.BlockSpec((S, 2, W - 1, C), lambda i, j: (i, 0, 0, j))]
        args = [x, state]
    else:
        B, T = state
        S, L, R = 1, FFN_ROWS, FFN_ROWS
        nt = T // R
        grid = (B, nt, nj)
        row_map = lambda b, t, j: (b * nt + t, 0)
        colmap = lambda off: (lambda b, t, j: (0, j + off))
        wd_map = lambda b, t, j: (j, 0)
        new_spec = pl.BlockSpec((None, None, 2, W - 1, C), lambda b, t, j: (b, t, 0, 0, j))
        new_shape = jax.ShapeDtypeStruct((B, nt, 2, W - 1, DFF), F32)
        in_specs = [pl.BlockSpec((R, D), row_map)]
        args = [x]
    in_specs += [_const_spec((1, D)),
                 pl.BlockSpec((D, C), colmap(off_a // C)), pl.BlockSpec((D, C), colmap(off_g // C)),
                 pl.BlockSpec((W, C), colmap(0)), pl.BlockSpec((W, C), colmap(nj)),
                 pl.BlockSpec((1, C), colmap(0)), pl.BlockSpec((1, C), colmap(nj)),
                 pl.BlockSpec((C, D), wd_map), _const_spec((1, D))]
    args += [p["norm_ffn"], wua, wug, p["ffn_conv_w"], p["ffn_conv_w"], p["ffn_conv_b"],
             p["ffn_conv_b"], w_down, p["norm_final"]]
    out_specs = [pl.BlockSpec((R, D), row_map), new_spec]
    out_shape = [jax.ShapeDtypeStruct((rows, D), F32), new_shape]
    if streaming:
        out_specs += [pl.BlockSpec((D, C), colmap(0)), pl.BlockSpec((D, C), colmap(0)), pl.BlockSpec((C, D), wd_map)]
        out_shape += [jax.ShapeDtypeStruct((D, DFF), BF), jax.ShapeDtypeStruct((D, DFF), BF),
                      jax.ShapeDtypeStruct((DFF, D), BF)]
    scratch = [pltpu.VMEM((R, D), BF),
               pltpu.VMEM((R, D), F32),
               pltpu.VMEM((S, L + HALO, C), F32),
               pltpu.VMEM((S, L + HALO, C), F32),
               pltpu.VMEM((R, C), BF),
               pltpu.VMEM((nj, 2, HALO, C), F32)]
    return pl.pallas_call(
        functools.partial(_ffn_body, streaming, S, L),
        grid=grid, in_specs=in_specs, out_specs=out_specs, out_shape=out_shape,
        scratch_shapes=scratch,
        compiler_params=_params(len(grid)),
        name="ffn_sample" if streaming else "ffn_prompt",
    )(*args)


def kernel(x_prompt, x_sample, mem_prompt, cache_mem_k, cache_mem_v, state_lru_h, state_lru_conv, state_ffn_conv, norm_mix, w_in, g_v, gmlp_w, gmlp_b, lru_conv_w, lru_conv_b, lru_wa, lru_ba, lru_wx, lru_bx, lru_lam, g_a, g_b, w_out, norm_mem, w_kv, norm_xa, w_q, w_o, norm_ffn, w_up, ffn_conv_w, ffn_conv_b, w_down, norm_final):
    depth = w_in.shape[0]
    assert depth == 1, "single-layer trunk"
    B, T, D = x_prompt.shape
    N, L, _ = x_sample.shape
    M = mem_prompt.shape[1]
    n_heads, head_dim = cache_mem_k.shape[3], cache_mem_k.shape[4]
    G, NCH = gmlp_w.shape[1], gmlp_w.shape[2]
    DA, DB = g_v.shape[1], lru_lam.shape[1]
    GD = DA // G
    DFF = w_down.shape[1]
    W_LRU, W_FFN = lru_conv_w.shape[1], ffn_conv_w.shape[1]
    assert L <= CAUSAL_CHUNK and NCH % L == 0 and L == ROWS
    row = lambda a: a.reshape(1, -1)

    shared = {
        "norm_mix": row(norm_mix[0]), "g_v": row(g_v[0]),
        "lru_conv_w": lru_conv_w[0], "lru_conv_b": row(lru_conv_b[0]),
        "lru_wa": lru_wa[0], "lru_ba": row(lru_ba[0]),
        "lru_wx": lru_wx[0], "lru_bx": row(lru_bx[0]), "lru_lam": row(lru_lam[0]),
        "g_a": row(g_a[0]), "g_b": row(g_b[0]),
        "norm_xa": row(norm_xa[0]),
        "norm_ffn": row(norm_ffn[0]), "ffn_conv_w": ffn_conv_w[0],
        "ffn_conv_b": row(ffn_conv_b[0]), "norm_final": row(norm_final),
    }
    (mk, mv, mk_b, mv_b), resident = _memory_kv(mem_prompt.reshape(B * M, D), row(norm_mem[0]), w_kv[0],
                                                [w_in[0], w_out[0], w_q[0], w_o[0]])
    shared.update(zip(("w_in", "w_out", "w_q", "w_o"), resident))
    prompt_p = dict(shared, gmlp_w_eff=gmlp_w[0],
                    gmlp_bias_rows=jnp.repeat(gmlp_b[0].T, GD, axis=1))
    reps = NCH // L
    sample_p = dict(shared, gmlp_w_eff=jnp.tile(gmlp_w[0][:, :L, :L], (1, reps, reps)),
                    gmlp_bias_rows=jnp.repeat(jnp.tile(gmlp_b[0][:, :L].T, (reps, 1)), GD, axis=1))

    xs = x_sample.reshape(N * L, D)
    pconv = jnp.pad(state_lru_conv[0], ((0, 0), (HALO - (W_LRU - 1), 0), (0, 0)))
    xs, v_s, conv_s, h_s = _mixer(True, xs, (pconv, state_lru_h[0].reshape(N, 1, DB)), sample_p)
    xs = _attn_sample(xs, cache_mem_k[0], cache_mem_v[0], sample_p)
    ffn_prev = jnp.swapaxes(state_ffn_conv[0].reshape(N, W_FFN - 1, 2, DFF), 1, 2)
    y_s, ffn_s, w_up_lin, w_up_gated, w_down_b = _ffn(True, xs, ffn_prev, sample_p,
                                                      (w_up[0], 0), (w_up[0], DFF), w_down[0])

    xp = x_prompt.reshape(B * T, D)
    xp, conv_p, h_p = _mixer(False, xp, (B, T), prompt_p)
    xp = _attn_prompt(xp, mk_b, mv_b, prompt_p, n_heads, B)
    y_p, ffn_p = _ffn(False, xp, (B, T), prompt_p, (w_up_lin, 0), (w_up_gated, 0), w_down_b)

    def ffn_state(a):
        return jnp.swapaxes(a, 1, 2).reshape(1, a.shape[0], W_FFN - 1, 2 * DFF)

    return (y_p.reshape(B, T, D), y_s.reshape(N, L, D),
            mk.reshape(1, B, M, n_heads, head_dim), mv.reshape(1, B, M, n_heads, head_dim),
            h_p.reshape(1, B, DB), conv_p.reshape(1, B, W_LRU - 1, DB), ffn_state(ffn_p[:, -1]),
            h_s.reshape(1, N, DB), conv_s.reshape(1, N, W_LRU - 1, DB), ffn_state(ffn_s),
            v_s.reshape(1, N, L, DA))
```

```python
import functools

import jax
import jax.numpy as jnp
from jax import lax
from jax.experimental import pallas as pl
from jax.experimental.pallas import tpu as pltpu

F32 = jnp.float32
BF = jnp.bfloat16

EPS = 1e-6
LRU_C = 8.0
CAUSAL_CHUNK = 64
HALO = 8
ROWS = 16
LOOP_UNROLL = 4
SOFTMAX_ROWS = 64
Q_GROUP = 128
MIX_CHUNK = 32
MIX_GROUP = 256
V7X_VMEM_BYTES = 64 * 1024 * 1024
VMEM_LIMIT = V7X_VMEM_BYTES - 8 * 1024 * 1024

MIX_ROWS = 256
ATT_ROWS = 512
FFN_ROWS = 512
FFN_COLS = 1024
FFN_SAMPLE_COLS = 256
FFN_SUB = 256
FFN_CHUNK = 64
KV_COLS = 256
SAMPLE_MIX_STREAMS = 16
SAMPLE_ATT_STREAMS = 2


def _rms(x, g):
    return x * lax.rsqrt(jnp.mean(x * x, axis=-1, keepdims=True) + EPS) * g


def _gelu(x):
    return x * (0.5 * (1.0 + jnp.tanh(0.7978845608028654 * (x + 0.044715 * (x * x * x)))))


def _softplus(x):
    return jnp.maximum(x, 0.0) + jnp.log1p(jnp.exp(-jnp.abs(x)))


def _dot(a, b):
    return jnp.dot(a, b, preferred_element_type=F32)


def _rows_loop(n_rows, chunk, fn, unroll=LOOP_UNROLL):
    def body(i, carry):
        fn(pl.multiple_of(i * chunk, chunk))
        return carry
    trips = n_rows // chunk
    lax.fori_loop(0, trips, body, 0, unroll=min(unroll, trips))


def _causal_conv(ext, taps, bias):
    width = len(taps)
    acc = None
    for k in range(width):
        shift = width - 1 - k
        src = ext if shift == 0 else pltpu.roll(ext, shift, axis=0)
        term = src[HALO:, :] * taps[k]
        acc = term if acc is None else acc + term
    return acc + bias


def _const_spec(shape):
    nd = len(shape)
    return pl.BlockSpec(shape, lambda *_: (0,) * nd, pipeline_mode=pl.Buffered(1))


def _params(n_grid):
    return pltpu.CompilerParams(dimension_semantics=("arbitrary",) * n_grid, vmem_limit_bytes=VMEM_LIMIT)


def _kv_body(n_casts, head_dim, mem_ref, g_ref, wk_ref, wv_ref, *refs):
    cast_in, refs = refs[:n_casts], refs[n_casts:]
    k_hbm, v_hbm, kb_ref, vb_ref = refs[:4]
    cast_out = refs[4:4 + n_casts]
    h_scr, stage, sem = refs[4 + n_casts:]
    j, n_steps = pl.program_id(0), pl.num_programs(0)
    slot = lax.rem(j, 2)
    cols = wk_ref.shape[1]
    per_head = head_dim // cols

    def stores(step, slot):
        head = lax.div(step, per_head)
        c0 = pl.multiple_of(lax.rem(step, per_head) * cols, cols)
        return [pltpu.make_async_copy(stage.at[slot, which], hbm.at[:, head, pl.ds(c0, cols)], sem.at[slot, which])
                for which, hbm in enumerate((k_hbm, v_hbm))]

    @pl.when(j == 0)
    def _():
        g = g_ref[...]
        def norm(r0):
            rows = pl.ds(r0, ROWS)
            h_scr[rows, :] = _rms(mem_ref[rows, :], g).astype(BF)
        _rows_loop(mem_ref.shape[0], ROWS, norm)

    @pl.when(j >= 2)
    def _():
        for cp in stores(j - 2, slot):
            cp.wait()

    k = _dot(h_scr[...], wk_ref[...].astype(BF))
    stage[slot, 0] = k
    kb_ref[...] = k.astype(BF)
    v = _dot(h_scr[...], wv_ref[...].astype(BF))
    stage[slot, 1] = v
    vb_ref[...] = v.astype(BF)
    for cp in stores(j, slot):
        cp.start()
    for src, dst in zip(cast_in, cast_out):
        dst[...] = src[...].astype(BF)

    @pl.when(j == n_steps - 1)
    def _():
        for cp in stores(j - 1, 1 - slot) + stores(j, slot):
            cp.wait()


def _memory_kv(mem, g, w_kv, resident, head_dim):
    m, d = mem.shape
    n_steps = d // KV_COLS
    assert n_steps >= 2 and head_dim % KV_COLS == 0
    col = pl.BlockSpec((m, KV_COLS), lambda j: (0, j))
    hbm = pl.BlockSpec(memory_space=pl.ANY)
    cast_specs = [pl.BlockSpec((w.shape[0] // n_steps, w.shape[1]), lambda j: (j, 0)) for w in resident]
    kv_shape = jax.ShapeDtypeStruct((m, d // head_dim, head_dim), F32)
    outs = pl.pallas_call(
        functools.partial(_kv_body, len(resident), head_dim),
        grid=(n_steps,),
        in_specs=[_const_spec((m, d)), _const_spec((1, d)),
                  pl.BlockSpec((d, KV_COLS), lambda j: (0, j)),
                  pl.BlockSpec((d, KV_COLS), lambda j: (0, j + n_steps))] + cast_specs,
        out_specs=[hbm, hbm, col, col] + cast_specs,
        out_shape=[kv_shape, kv_shape, jax.ShapeDtypeStruct((m, d), BF), jax.ShapeDtypeStruct((m, d), BF)]
                  + [jax.ShapeDtypeStruct(w.shape, BF) for w in resident],
        scratch_shapes=[pltpu.VMEM((m, d), BF), pltpu.VMEM((2, 2, m, KV_COLS), F32),
                        pltpu.SemaphoreType.DMA((2, 2))],
        compiler_params=_params(1),
        name="memory_kv",
    )(mem, g, w_kv, w_kv, *resident)
    return outs[:4], outs[4:]


def _scan_chunk(a, b, carry):
    n = a.shape[0]
    pos = lax.broadcasted_iota(jnp.int32, a.shape, 0)
    d = 1
    while d < n:
        keep = pos >= d
        a_prev = jnp.where(keep, pltpu.roll(a, d, axis=0), 1.0)
        b_prev = jnp.where(keep, pltpu.roll(b, d, axis=0), 0.0)
        b = a * b_prev + b
        a = a * a_prev
        d *= 2
    h = b + a * carry
    return h, h[n - 1:n, :]


def _mixer_body(streaming, n_streams, n_rows, *refs):
    S, L = n_streams, n_rows
    R = S * L
    refs = list(refs)
    x_ref = refs.pop(0)
    if streaming:
        pconv_ref, h0_ref = refs.pop(0), refs.pop(0)
    (nmix_ref, win_ref, gv_ref, gw_ref, gbias_ref, cw_ref, cb_ref, wa_ref, ba_ref, wx_ref, bx_ref,
     lam_ref, ga_ref, gb_ref, wout_ref) = refs[:15]
    refs = refs[15:]
    x1_ref = refs.pop(0)
    if streaming:
        v_ref = refs.pop(0)
    convnew_ref, hlast_ref = refs.pop(0), refs.pop(0)
    h_scr, z_scr, vb_scr, y_scr, xc_scr, xcb_scr, ra_scr, rx_scr, mix_scr, gwm_scr, carry_scr = refs

    DA = gv_ref.shape[1]
    DB = lam_ref.shape[1]
    G, NCH = gw_ref.shape[0], gw_ref.shape[1]
    GD = DA // G
    H, HD = wa_ref.shape[0], wa_ref.shape[1]
    W = cw_ref.shape[0]

    if streaming:
        first = pl.program_id(0) == 0
    else:
        t = pl.program_id(1)
        first = jnp.logical_and(pl.program_id(0) == 0, t == 0)

    @pl.when(first)
    def _():
        ri = lax.broadcasted_iota(jnp.int32, (NCH, NCH), 0)
        ci = lax.broadcasted_iota(jnp.int32, (NCH, NCH), 1)
        if streaming:
            sh = L.bit_length() - 1
            keep = lax.shift_right_logical(ri, sh) == lax.shift_right_logical(ci, sh)
        else:
            sh = CAUSAL_CHUNK.bit_length() - 1
            keep = lax.shift_right_logical(ri, sh) >= lax.shift_right_logical(ci, sh)
        for g in range(G):
            gwm_scr[g] = jnp.where(keep, gw_ref[g], 0.0).astype(BF)

    if streaming:
        y_scr[:, 0:HALO, :] = pconv_ref[...]
    else:
        @pl.when(t == 0)
        def _():
            y_scr[0, 0:HALO, :] = jnp.zeros((HALO, DB), F32)
            carry_scr[...] = jnp.zeros((1, DB), F32)

    nmix, gv, ga, gb = nmix_ref[...], gv_ref[...], ga_ref[...], gb_ref[...]
    cw, cb = [cw_ref[k:k + 1, :] for k in range(W)], cb_ref[...]
    ba, bx = ba_ref[...], bx_ref[...]
    sp_lam = _softplus(-lam_ref[...])
    carry = None if streaming else carry_scr[...]

    GRP = min(R, MIX_GROUP)
    per_group = GRP // L if streaming else 1
    for grp in range(R // GRP):
        r0 = grp * GRP
        rs = slice(r0, r0 + GRP)
        chunks = [slice(c0, c0 + MIX_CHUNK) for c0 in range(r0, r0 + GRP, MIX_CHUNK)]
        streams = range(grp * per_group, (grp + 1) * per_group)

        for c in chunks:
            h_scr[c, :] = _rms(x_ref[c, :], nmix).astype(BF)
        xr = _dot(h_scr[rs, :], win_ref[:, 2 * DA:2 * DA + DB])
        if streaming:
            y_scr[streams.start:streams.stop, HALO:HALO + L, :] = xr.reshape(per_group, L, DB)
        else:
            y_scr[0, HALO + r0:HALO + r0 + GRP, :] = xr
        z_scr[rs, DA:2 * DA] = _dot(h_scr[rs, :], win_ref[:, DA:2 * DA])

        if streaming:
            conv_in = [(y_scr[s], slice(s * L, (s + 1) * L)) for s in streams]
        else:
            conv_in = [(y_scr[0, c.start:c.stop + HALO, :], c) for c in chunks]
        for ext, c in conv_in:
            xc = _causal_conv(ext, cw, cb)
            xc_scr[c, :] = xc
            xcb_scr[c, :] = xc.astype(BF)
        z_scr[rs, 2 * DA:2 * DA + DB] = _dot(h_scr[rs, :], win_ref[:, 2 * DA + DB:2 * DA + 2 * DB])
        for hh in range(H):
            cs = slice(hh * HD, (hh + 1) * HD)
            ra_scr[rs, cs] = _dot(xcb_scr[rs, cs], wa_ref[hh].astype(BF))
            rx_scr[rs, cs] = _dot(xcb_scr[rs, cs], wx_ref[hh].astype(BF))

        for c in chunks:
            v = _rms(_gelu(z_scr[c, DA:2 * DA]), gv)
            if streaming:
                v_ref[c, :] = v
            vb_scr[c, :] = v.astype(BF)
        z_scr[rs, 0:DA] = _dot(h_scr[rs, :], win_ref[:, 0:DA])

        for c in chunks:
            r = jax.nn.sigmoid(ra_scr[c, :] + ba)
            i = jax.nn.sigmoid(rx_scr[c, :] + bx)
            a = jnp.exp(-LRU_C * r * sp_lam)
            ra_scr[c, :] = a
            rx_scr[c, :] = jnp.sqrt(1.0 - a * a) * (i * xc_scr[c, :])

        for n0 in range(r0, r0 + GRP, NCH):
            ns = slice(n0, n0 + NCH)
            for g in range(G):
                cs = slice(g * GD, (g + 1) * GD)
                sp = _dot(gwm_scr[g], vb_scr[ns, cs])
                z_scr[ns, cs] = _gelu(z_scr[ns, cs]) * (sp + gbias_ref[:, cs])
        for c in chunks:
            mix_scr[c, 0:DA] = _rms(z_scr[c, 0:DA], ga).astype(BF)
        x1_ref[rs, :] = x_ref[rs, :] + _dot(mix_scr[rs, 0:DA], wout_ref[0:DA, :])

        if streaming:
            for s in streams:
                c = slice(s * L, (s + 1) * L)
                h, last = _scan_chunk(ra_scr[c, :], rx_scr[c, :], h0_ref[s])
                rx_scr[c, :] = h
                hlast_ref[s] = last
        else:
            for c0 in range(r0, r0 + GRP, HALO):
                c = slice(c0, c0 + HALO)
                h, carry = _scan_chunk(ra_scr[c, :], rx_scr[c, :], carry)
                rx_scr[c, :] = h
        for c in chunks:
            out_b = rx_scr[c, :] * _gelu(z_scr[c, 2 * DA:2 * DA + DB])
            mix_scr[c, DA:DA + DB] = _rms(out_b, gb).astype(BF)
        x1_ref[rs, :] += _dot(mix_scr[rs, DA:DA + DB], wout_ref[DA:DA + DB, :])

    tail = slice(HALO + L - (W - 1), HALO + L)
    if streaming:
        for s in range(S):
            convnew_ref[s] = y_scr[s, tail, :]
    else:
        convnew_ref[...] = y_scr[0, tail, :]
        y_scr[0, 0:HALO, :] = y_scr[0, L:L + HALO, :]
        carry_scr[...] = carry
        hlast_ref[...] = carry


def _mixer(streaming, x, states, p):
    rows, D = x.shape
    DA, DB = p["g_v"].shape[1], p["lru_lam"].shape[1]
    weights = [p["norm_mix"], p["w_in"], p["g_v"], p["gmlp_w_eff"], p["gmlp_bias_rows"], p["lru_conv_w"],
               p["lru_conv_b"], p["lru_wa"], p["lru_ba"], p["lru_wx"], p["lru_bx"], p["lru_lam"], p["g_a"],
               p["g_b"], p["w_out"]]
    w_specs = [_const_spec(w.shape) for w in weights]
    W = p["lru_conv_w"].shape[0]
    if streaming:
        pconv, h0 = states
        N = h0.shape[0]
        L = rows // N
        S = SAMPLE_MIX_STREAMS
        R = S * L
        grid = (N // S,)
        row_spec = lambda c: pl.BlockSpec((R, c), lambda i: (i, 0))
        in_specs = [row_spec(D), pl.BlockSpec((S, HALO, DB), lambda i: (i, 0, 0)),
                    pl.BlockSpec((S, 1, DB), lambda i: (i, 0, 0))] + w_specs
        out_specs = [row_spec(D), row_spec(DA), pl.BlockSpec((S, W - 1, DB), lambda i: (i, 0, 0)),
                     pl.BlockSpec((S, 1, DB), lambda i: (i, 0, 0))]
        out_shape = [jax.ShapeDtypeStruct((rows, D), F32), jax.ShapeDtypeStruct((rows, DA), F32),
                     jax.ShapeDtypeStruct((N, W - 1, DB), F32), jax.ShapeDtypeStruct((N, 1, DB), F32)]
        args = [x, pconv, h0] + weights
    else:
        B, T = states
        S, L, R = 1, MIX_ROWS, MIX_ROWS
        nt = T // R
        grid = (B, nt)
        row_spec = lambda c: pl.BlockSpec((R, c), lambda b, t: (b * nt + t, 0))
        in_specs = [row_spec(D)] + w_specs
        out_specs = [row_spec(D), pl.BlockSpec((None, W - 1, DB), lambda b, t: (b, 0, 0)),
                     pl.BlockSpec((None, 1, DB), lambda b, t: (b, 0, 0))]
        out_shape = [jax.ShapeDtypeStruct((rows, D), F32), jax.ShapeDtypeStruct((B, W - 1, DB), F32),
                     jax.ShapeDtypeStruct((B, 1, DB), F32)]
        args = [x] + weights
    G, NCH = p["gmlp_w_eff"].shape[0], p["gmlp_w_eff"].shape[1]
    scratch = [pltpu.VMEM((R, D), BF),
               pltpu.VMEM((R, 2 * DA + DB), F32),
               pltpu.VMEM((R, DA), BF),
               pltpu.VMEM((S, L + HALO, DB), F32),
               pltpu.VMEM((R, DB), F32),
               pltpu.VMEM((R, DB), BF),
               pltpu.VMEM((R, DB), F32),
               pltpu.VMEM((R, DB), F32),
               pltpu.VMEM((R, DA + DB), BF),
               pltpu.VMEM((G, NCH, NCH), BF),
               pltpu.VMEM((1, DB), F32)]
    return pl.pallas_call(
        functools.partial(_mixer_body, streaming, S, L),
        grid=grid, in_specs=in_specs, out_specs=out_specs, out_shape=out_shape, scratch_shapes=scratch,
        compiler_params=_params(len(grid)),
        name="mixer_sample" if streaming else "mixer_prompt",
    )(*args)


def _attend(q_scr, o_scr, s_scr, p_scr, row0, n_rows, keys, values, head_dim):
    n_heads = q_scr.shape[1] // head_dim
    scale = head_dim ** -0.5
    rows = pl.ds(row0, n_rows)
    n_chunk = min(n_rows, SOFTMAX_ROWS)
    heads = [slice(h * head_dim, (h + 1) * head_dim) for h in range(n_heads)]
    for h, cs in enumerate(heads):
        s_scr[h] = lax.dot_general(q_scr[rows, cs], keys(h), (((1,), (1,)), ((), ())),
                                   preferred_element_type=F32) * scale
    for h in range(n_heads):
        for r0 in range(0, n_rows, n_chunk):
            s = s_scr[h, r0:r0 + n_chunk, :]
            e = jnp.exp(s - jnp.max(s, axis=-1, keepdims=True))
            p_scr[h, r0:r0 + n_chunk, :] = (e / jnp.sum(e, axis=-1, keepdims=True)).astype(BF)
    for h, cs in enumerate(heads):
        o_scr[rows, cs] = _dot(p_scr[h], values(h)).astype(BF)


def _project_q(x_ref, nxa_ref, wq_ref, h_scr, q_scr):
    g = nxa_ref[...]
    for r0 in range(0, x_ref.shape[0], Q_GROUP):
        for c0 in range(r0, r0 + Q_GROUP, MIX_CHUNK):
            c = slice(c0, c0 + MIX_CHUNK)
            h_scr[c, :] = _rms(x_ref[c, :], g).astype(BF)
        rs = slice(r0, r0 + Q_GROUP)
        q_scr[rs, :] = _dot(h_scr[rs, :], wq_ref[...]).astype(BF)


def _attn_prompt_body(head_dim, x_ref, nxa_ref, wq_ref, k_ref, v_ref, wo_ref, o_ref,
                      h_scr, q_scr, a_scr, s_scr, p_scr):
    _project_q(x_ref, nxa_ref, wq_ref, h_scr, q_scr)
    head = lambda h: slice(h * head_dim, (h + 1) * head_dim)
    _attend(q_scr, a_scr, s_scr, p_scr, 0, x_ref.shape[0], lambda h: k_ref[:, head(h)],
            lambda h: v_ref[:, head(h)], head_dim)
    o_ref[...] = x_ref[...] + _dot(a_scr[...], wo_ref[...])


def _attn_sample_body(head_dim, n_streams, n_rows, x_ref, nxa_ref, wq_ref, k_hbm, v_hbm, wo_ref, o_ref,
                      h_scr, q_scr, a_scr, s_scr, p_scr, kv_buf, kv_sem):
    i = pl.program_id(0)
    n_steps = pl.num_programs(0)
    n_heads = q_scr.shape[1] // head_dim
    slot = lax.rem(i, 2)

    def copies(step, dst_slot):
        out = []
        for which, hbm in enumerate((k_hbm, v_hbm)):
            for s in range(n_streams):
                for h in range(n_heads):
                    out.append(pltpu.make_async_copy(hbm.at[step * n_streams + s, :, h, :],
                                                     kv_buf.at[dst_slot, which, s, h],
                                                     kv_sem.at[dst_slot, which, s, h]))
        return out

    @pl.when(i == 0)
    def _():
        for cp in copies(0, 0):
            cp.start()

    @pl.when(i + 1 < n_steps)
    def _():
        for cp in copies(i + 1, 1 - slot):
            cp.start()

    @pl.when(i == 0)
    def _():
        _project_q(x_ref, nxa_ref, wq_ref, h_scr, q_scr)

    for cp in copies(i, slot):
        cp.wait()

    for s in range(n_streams):
        row0 = pl.multiple_of((i * n_streams + s) * n_rows, n_rows)
        _attend(q_scr, a_scr, s_scr.at[s], p_scr.at[s], row0, n_rows,
                lambda h: kv_buf[slot, 0, s, h].astype(BF), lambda h: kv_buf[slot, 1, s, h].astype(BF), head_dim)

    @pl.when(i == n_steps - 1)
    def _():
        o_ref[...] = x_ref[...] + _dot(a_scr[...], wo_ref[...])


def _attn_prompt(x, k, v, p, n_heads, batch):
    rows, D = x.shape
    M = k.shape[0] // batch
    R = ATT_ROWS
    nt = rows // batch // R
    row_spec = pl.BlockSpec((R, D), lambda b, t: (b * nt + t, 0))
    mem_spec = pl.BlockSpec((M, D), lambda b, t: (b, 0))
    return pl.pallas_call(
        functools.partial(_attn_prompt_body, D // n_heads),
        grid=(batch, nt),
        in_specs=[row_spec, _const_spec((1, D)), _const_spec((D, D)), mem_spec, mem_spec, _const_spec((D, D))],
        out_specs=row_spec,
        out_shape=jax.ShapeDtypeStruct((rows, D), F32),
        scratch_shapes=[pltpu.VMEM((R, D), BF), pltpu.VMEM((R, D), BF), pltpu.VMEM((R, D), BF),
                        pltpu.VMEM((n_heads, R, M), F32), pltpu.VMEM((n_heads, R, M), BF)],
        compiler_params=_params(2),
        name="attn_prompt",
    )(x, p["norm_xa"], p["w_q"], k, v, p["w_o"])


def _attn_sample(x, k, v, p):
    rows, D = x.shape
    N, M, n_heads, head_dim = k.shape
    L = rows // N
    S = SAMPLE_ATT_STREAMS
    hbm = pl.BlockSpec(memory_space=pl.ANY)
    return pl.pallas_call(
        functools.partial(_attn_sample_body, head_dim, S, L),
        grid=(N // S,),
        in_specs=[_const_spec((rows, D)), _const_spec((1, D)), _const_spec((D, D)), hbm, hbm,
                  _const_spec((D, D))],
        out_specs=pl.BlockSpec((rows, D), lambda i: (0, 0)),
        out_shape=jax.ShapeDtypeStruct((rows, D), F32),
        scratch_shapes=[pltpu.VMEM((rows, D), BF), pltpu.VMEM((rows, D), BF), pltpu.VMEM((rows, D), BF),
                        pltpu.VMEM((S, n_heads, L, M), F32), pltpu.VMEM((S, n_heads, L, M), BF),
                        pltpu.VMEM((2, 2, S, n_heads, M, head_dim), F32),
                        pltpu.SemaphoreType.DMA((2, 2, S, n_heads))],
        compiler_params=_params(1),
        name="attn_sample",
    )(x, p["norm_xa"], p["w_q"], k, v, p["w_o"])


def _ffn_body(streaming, n_streams, n_rows, *refs):
    S, L = n_streams, n_rows
    R = S * L
    refs = list(refs)
    x_ref = refs.pop(0)
    if streaming:
        st_ref = refs.pop(0)
    (nffn_ref, wua_ref, wug_ref, cwa_ref, cwg_ref, cba_ref, cbg_ref, wd_ref, nfin_ref) = refs[:9]
    refs = refs[9:]
    y_ref, new_ref = refs.pop(0), refs.pop(0)
    if streaming:
        casts = ((wua_ref, refs.pop(0)), (wug_ref, refs.pop(0)), (wd_ref, refs.pop(0)))
    h_scr, acc_scr, ya_scr, yg_scr, act_scr, carry_scr = refs
    W = cwa_ref.shape[0]
    C = wua_ref.shape[1]

    if streaming:
        j = pl.program_id(1)
    else:
        t, j = pl.program_id(1), pl.program_id(2)
    nj = pl.num_programs(2 - int(streaming))

    @pl.when(j == 0)
    def _():
        g = nffn_ref[...]
        def norm(r0):
            rows = pl.ds(r0, ROWS)
            h_scr[rows, :] = _rms(x_ref[rows, :], g).astype(BF)
            acc_scr[rows, :] = jnp.zeros((ROWS, acc_scr.shape[1]), F32)
        _rows_loop(R, ROWS, norm)
        if not streaming:
            @pl.when(t == 0)
            def _():
                carry_scr[...] = jnp.zeros(carry_scr.shape, F32)

    if streaming:
        for src, dst in casts:
            dst[...] = src[...].astype(BF)
        (_, wua_ref), (_, wug_ref), (_, wd_ref) = casts

    tail = slice(HALO + L - (W - 1), HALO + L)
    n_chunk = min(L, FFN_CHUNK)
    for c in range(C // FFN_SUB):
        cs = slice(c * FFN_SUB, (c + 1) * FFN_SUB)
        for half, (y_scr, w_ref) in enumerate(((ya_scr, wua_ref), (yg_scr, wug_ref))):
            y_scr[:, HALO:HALO + L, cs] = _dot(h_scr[...], w_ref[:, cs]).reshape(S, L, FFN_SUB)
            if streaming:
                for s in range(S):
                    y_scr[s, HALO - (W - 1):HALO, cs] = st_ref[s, half, :, cs]
                    new_ref[s, half, :, cs] = y_scr[s, tail, cs]
            else:
                y_scr[0, 0:HALO, cs] = carry_scr[j, half, :, cs]
                carry_scr[j, half, :, cs] = y_scr[0, L:L + HALO, cs]
                new_ref[half, :, cs] = y_scr[0, tail, cs]

        taps_a = [jnp.broadcast_to(cwa_ref[k:k + 1, cs], (n_chunk, FFN_SUB)) for k in range(W)]
        taps_g = [jnp.broadcast_to(cwg_ref[k:k + 1, cs], (n_chunk, FFN_SUB)) for k in range(W)]
        bias_a = jnp.broadcast_to(cba_ref[:, cs], (n_chunk, FFN_SUB))
        bias_g = jnp.broadcast_to(cbg_ref[:, cs], (n_chunk, FFN_SUB))
        for s in range(S):
            for r0 in range(0, L, n_chunk):
                ext = slice(r0, r0 + n_chunk + HALO)
                lin = _causal_conv(ya_scr[s, ext, cs], taps_a, bias_a)
                gated = _causal_conv(yg_scr[s, ext, cs], taps_g, bias_g)
                out0 = s * L + r0
                act_scr[out0:out0 + n_chunk, cs] = (_gelu(gated) * lin).astype(BF)

    for k0 in range(0, C, max(C // 2, FFN_SUB)):
        ks = slice(k0, k0 + max(C // 2, FFN_SUB))
        acc_scr[...] += _dot(act_scr[:, ks], wd_ref[ks, :])

    @pl.when(j == nj - 1)
    def _():
        g = nfin_ref[...]
        def final(r0):
            rows = pl.ds(r0, ROWS)
            y_ref[rows, :] = _rms(x_ref[rows, :] + acc_scr[rows, :], g)
        _rows_loop(R, ROWS, final)


def _ffn(streaming, x, state, p, w_up_lin, w_up_gated, w_down):
    rows, D = x.shape
    DFF = w_down.shape[0]
    W = p["ffn_conv_w"].shape[0]
    C = FFN_SAMPLE_COLS if streaming else FFN_COLS
    nj = DFF // C
    (wua, off_a), (wug, off_g) = w_up_lin, w_up_gated
    if streaming:
        N = state.shape[0]
        L = rows // N
        R = FFN_ROWS
        S = R // L
        grid = (rows // R, nj)
        row_map = lambda i, j: (i, 0)
        colmap = lambda off: (lambda i, j: (0, j + off))
        wd_map = lambda i, j: (j, 0)
        new_spec = pl.BlockSpec((S, 2, W - 1, C), lambda i, j: (i, 0, 0, j))
        new_shape = jax.ShapeDtypeStruct((N, 2, W - 1, DFF), F32)
        in_specs = [pl.BlockSpec((R, D), row_map),
                    pl.BlockSpec((S, 2, W - 1, C), lambda i, j: (i, 0, 0, j))]
        args = [x, state]
    else:
        B, T = state
        S, L, R = 1, FFN_ROWS, FFN_ROWS
        nt = T // R
        grid = (B, nt, nj)
        row_map = lambda b, t, j: (b * nt + t, 0)
        colmap = lambda off: (lambda b, t, j: (0, j + off))
        wd_map = lambda b, t, j: (j, 0)
        new_spec = pl.BlockSpec((None, None, 2, W - 1, C), lambda b, t, j: (b, t, 0, 0, j))
        new_shape = jax.ShapeDtypeStruct((B, nt, 2, W - 1, DFF), F32)
        in_specs = [pl.BlockSpec((R, D), row_map)]
        args = [x]
    in_specs += [_const_spec((1, D)),
                 pl.BlockSpec((D, C), colmap(off_a // C)), pl.BlockSpec((D, C), colmap(off_g // C)),
                 pl.BlockSpec((W, C), colmap(0)), pl.BlockSpec((W, C), colmap(nj)),
                 pl.BlockSpec((1, C), colmap(0)), pl.BlockSpec((1, C), colmap(nj)),
                 pl.BlockSpec((C, D), wd_map), _const_spec((1, D))]
    args += [p["norm_ffn"], wua, wug, p["ffn_conv_w"], p["ffn_conv_w"], p["ffn_conv_b"],
             p["ffn_conv_b"], w_down, p["norm_final"]]
    out_specs = [pl.BlockSpec((R, D), row_map), new_spec]
    out_shape = [jax.ShapeDtypeStruct((rows, D), F32), new_shape]
    if streaming:
        out_specs += [pl.BlockSpec((D, C), colmap(0)), pl.BlockSpec((D, C), colmap(0)), pl.BlockSpec((C, D), wd_map)]
        out_shape += [jax.ShapeDtypeStruct((D, DFF), BF), jax.ShapeDtypeStruct((D, DFF), BF),
                      jax.ShapeDtypeStruct((DFF, D), BF)]
    scratch = [pltpu.VMEM((R, D), BF),
               pltpu.VMEM((R, D), F32),
               pltpu.VMEM((S, L + HALO, C), F32),
               pltpu.VMEM((S, L + HALO, C), F32),
               pltpu.VMEM((R, C), BF),
               pltpu.VMEM((nj, 2, HALO, C), F32)]
    return pl.pallas_call(
        functools.partial(_ffn_body, streaming, S, L),
        grid=grid, in_specs=in_specs, out_specs=out_specs, out_shape=out_shape,
        scratch_shapes=scratch,
        compiler_params=_params(len(grid)),
        name="ffn_sample" if streaming else "ffn_prompt",
    )(*args)


def kernel(x_prompt, x_sample, mem_prompt, cache_mem_k, cache_mem_v, state_lru_h, state_lru_conv, state_ffn_conv, norm_mix, w_in, g_v, gmlp_w, gmlp_b, lru_conv_w, lru_conv_b, lru_wa, lru_ba, lru_wx, lru_bx, lru_lam, g_a, g_b, w_out, norm_mem, w_kv, norm_xa, w_q, w_o, norm_ffn, w_up, ffn_conv_w, ffn_conv_b, w_down, norm_final):
    depth = w_in.shape[0]
    assert depth == 1, "single-layer trunk"
    B, T, D = x_prompt.shape
    N, L, _ = x_sample.shape
    M = mem_prompt.shape[1]
    n_heads, head_dim = cache_mem_k.shape[3], cache_mem_k.shape[4]
    G, NCH = gmlp_w.shape[1], gmlp_w.shape[2]
    DA, DB = g_v.shape[1], lru_lam.shape[1]
    GD = DA // G
    DFF = w_down.shape[1]
    W_LRU, W_FFN = lru_conv_w.shape[1], ffn_conv_w.shape[1]
    assert L <= CAUSAL_CHUNK and NCH % L == 0 and L == ROWS
    row = lambda a: a.reshape(1, -1)

    shared = {
        "norm_mix": row(norm_mix[0]), "g_v": row(g_v[0]),
        "lru_conv_w": lru_conv_w[0], "lru_conv_b": row(lru_conv_b[0]),
        "lru_wa": lru_wa[0], "lru_ba": row(lru_ba[0]),
        "lru_wx": lru_wx[0], "lru_bx": row(lru_bx[0]), "lru_lam": row(lru_lam[0]),
        "g_a": row(g_a[0]), "g_b": row(g_b[0]),
        "norm_xa": row(norm_xa[0]),
        "norm_ffn": row(norm_ffn[0]), "ffn_conv_w": ffn_conv_w[0],
        "ffn_conv_b": row(ffn_conv_b[0]), "norm_final": row(norm_final),
    }
    (mk, mv, mk_b, mv_b), resident = _memory_kv(mem_prompt.reshape(B * M, D), row(norm_mem[0]), w_kv[0],
                                                [w_in[0], w_out[0], w_q[0], w_o[0]], head_dim)
    shared.update(zip(("w_in", "w_out", "w_q", "w_o"), resident))
    prompt_p = dict(shared, gmlp_w_eff=gmlp_w[0],
                    gmlp_bias_rows=jnp.repeat(gmlp_b[0].T, GD, axis=1))
    reps = NCH // L
    sample_p = dict(shared, gmlp_w_eff=jnp.tile(gmlp_w[0][:, :L, :L], (1, reps, reps)),
                    gmlp_bias_rows=jnp.repeat(jnp.tile(gmlp_b[0][:, :L].T, (reps, 1)), GD, axis=1))

    xs = x_sample.reshape(N * L, D)
    pconv = jnp.pad(state_lru_conv[0], ((0, 0), (HALO - (W_LRU - 1), 0), (0, 0)))
    xs, v_s, conv_s, h_s = _mixer(True, xs, (pconv, state_lru_h[0].reshape(N, 1, DB)), sample_p)
    xs = _attn_sample(xs, cache_mem_k[0], cache_mem_v[0], sample_p)
    ffn_prev = jnp.swapaxes(state_ffn_conv[0].reshape(N, W_FFN - 1, 2, DFF), 1, 2)
    y_s, ffn_s, w_up_lin, w_up_gated, w_down_b = _ffn(True, xs, ffn_prev, sample_p,
                                                      (w_up[0], 0), (w_up[0], DFF), w_down[0])

    xp = x_prompt.reshape(B * T, D)
    xp, conv_p, h_p = _mixer(False, xp, (B, T), prompt_p)
    xp = _attn_prompt(xp, mk_b, mv_b, prompt_p, n_heads, B)
    y_p, ffn_p = _ffn(False, xp, (B, T), prompt_p, (w_up_lin, 0), (w_up_gated, 0), w_down_b)

    def ffn_state(a):
        return jnp.swapaxes(a, 1, 2).reshape(1, a.shape[0], W_FFN - 1, 2 * DFF)

    return (y_p.reshape(B, T, D), y_s.reshape(N, L, D),
            mk.reshape(1, B, M, n_heads, head_dim), mv.reshape(1, B, M, n_heads, head_dim),
            h_p.reshape(1, B, DB), conv_p.reshape(1, B, W_LRU - 1, DB), ffn_state(ffn_p[:, -1]),
            h_s.reshape(1, N, DB), conv_s.reshape(1, N, W_LRU - 1, DB), ffn_state(ffn_s),
            v_s.reshape(1, N, L, DA))
```

```python
import functools

import jax
import jax.numpy as jnp
from jax import lax
from jax.experimental import pallas as pl
from jax.experimental.pallas import tpu as pltpu

F32 = jnp.float32
BF = jnp.bfloat16

EPS = 1e-6
LRU_C = 8.0
CAUSAL_CHUNK = 64
HALO = 8
ROWS = 16
LOOP_UNROLL = 4
SOFTMAX_ROWS = 64
Q_GROUP = 128
MIX_CHUNK = 32
MIX_GROUP = 256
V7X_VMEM_BYTES = 64 * 1024 * 1024
VMEM_LIMIT = V7X_VMEM_BYTES - 8 * 1024 * 1024

MIX_ROWS = 256
ATT_ROWS = 512
FFN_ROWS = 512
FFN_COLS = 1024
FFN_SAMPLE_COLS = 256
FFN_SUB = 256
FFN_CHUNK = 64
KV_COLS = 256
SAMPLE_MIX_STREAMS = 16
SAMPLE_ATT_STREAMS = 2


def _rms(x, g):
    return x * lax.rsqrt(jnp.mean(x * x, axis=-1, keepdims=True) + EPS) * g


def _gelu(x):
    return x * (0.5 * (1.0 + jnp.tanh(0.7978845608028654 * (x + 0.044715 * (x * x * x)))))


def _softplus(x):
    return jnp.maximum(x, 0.0) + jnp.log1p(jnp.exp(-jnp.abs(x)))


def _dot(a, b):
    return jnp.dot(a, b, preferred_element_type=F32)


def _rows_loop(n_rows, chunk, fn, unroll=LOOP_UNROLL):
    def body(i, carry):
        fn(pl.multiple_of(i * chunk, chunk))
        return carry
    trips = n_rows // chunk
    lax.fori_loop(0, trips, body, 0, unroll=min(unroll, trips))


def _causal_conv(ext, taps, bias):
    width = len(taps)
    acc = None
    for k in range(width):
        shift = width - 1 - k
        src = ext if shift == 0 else pltpu.roll(ext, shift, axis=0)
        term = src[HALO:, :] * taps[k]
        acc = term if acc is None else acc + term
    return acc + bias


def _const_spec(shape):
    nd = len(shape)
    return pl.BlockSpec(shape, lambda *_: (0,) * nd, pipeline_mode=pl.Buffered(1))


def _params(n_grid):
    return pltpu.CompilerParams(dimension_semantics=("arbitrary",) * n_grid, vmem_limit_bytes=VMEM_LIMIT)


def _kv_body(n_casts, head_dim, mem_ref, g_ref, wk_ref, wv_ref, *refs):
    cast_in, refs = refs[:n_casts], refs[n_casts:]
    k_hbm, v_hbm, kb_ref, vb_ref = refs[:4]
    cast_out = refs[4:4 + n_casts]
    h_scr, stage, sem = refs[4 + n_casts:]
    j, n_steps = pl.program_id(0), pl.num_programs(0)
    slot = lax.rem(j, 2)
    cols = wk_ref.shape[1]
    per_head = head_dim // cols

    def stores(step, slot):
        head = lax.div(step, per_head)
        c0 = pl.multiple_of(lax.rem(step, per_head) * cols, cols)
        return [pltpu.make_async_copy(stage.at[slot, which], hbm.at[:, head, pl.ds(c0, cols)], sem.at[slot, which])
                for which, hbm in enumerate((k_hbm, v_hbm))]

    @pl.when(j == 0)
    def _():
        g = g_ref[...]
        def norm(r0):
            rows = pl.ds(r0, ROWS)
            h_scr[rows, :] = _rms(mem_ref[rows, :], g).astype(BF)
        _rows_loop(mem_ref.shape[0], ROWS, norm)

    @pl.when(j >= 2)
    def _():
        for cp in stores(j - 2, slot):
            cp.wait()

    k = _dot(h_scr[...], wk_ref[...].astype(BF))
    stage[slot, 0] = k
    kb_ref[...] = k.astype(BF)
    v = _dot(h_scr[...], wv_ref[...].astype(BF))
    stage[slot, 1] = v
    vb_ref[...] = v.astype(BF)
    for cp in stores(j, slot):
        cp.start()
    for src, dst in zip(cast_in, cast_out):
        dst[...] = src[...].astype(BF)

    @pl.when(j == n_steps - 1)
    def _():
        for cp in stores(j - 1, 1 - slot) + stores(j, slot):
            cp.wait()


def _memory_kv(mem, g, w_kv, resident, head_dim):
    m, d = mem.shape
    n_steps = d // KV_COLS
    assert n_steps >= 2 and head_dim % KV_COLS == 0
    col = pl.BlockSpec((m, KV_COLS), lambda j: (0, j))
    hbm = pl.BlockSpec(memory_space=pl.ANY)
    cast_specs = [pl.BlockSpec((w.shape[0] // n_steps, w.shape[1]), lambda j: (j, 0)) for w in resident]
    kv_shape = jax.ShapeDtypeStruct((m, d // head_dim, head_dim), F32)
    outs = pl.pallas_call(
        functools.partial(_kv_body, len(resident), head_dim),
        grid=(n_steps,),
        in_specs=[_const_spec((m, d)), _const_spec((1, d)),
                  pl.BlockSpec((d, KV_COLS), lambda j: (0, j)),
                  pl.BlockSpec((d, KV_COLS), lambda j: (0, j + n_steps))] + cast_specs,
        out_specs=[hbm, hbm, col, col] + cast_specs,
        out_shape=[kv_shape, kv_shape, jax.ShapeDtypeStruct((m, d), BF), jax.ShapeDtypeStruct((m, d), BF)]
                  + [jax.ShapeDtypeStruct(w.shape, BF) for w in resident],
        scratch_shapes=[pltpu.VMEM((m, d), BF), pltpu.VMEM((2, 2, m, KV_COLS), F32),
                        pltpu.SemaphoreType.DMA((2, 2))],
        compiler_params=_params(1),
        name="memory_kv",
    )(mem, g, w_kv, w_kv, *resident)
    return outs[:4], outs[4:]


def _scan_chunk(a, b, carry):
    n = a.shape[0]
    pos = lax.broadcasted_iota(jnp.int32, a.shape, 0)
    d = 1
    while d < n:
        keep = pos >= d
        a_prev = jnp.where(keep, pltpu.roll(a, d, axis=0), 1.0)
        b_prev = jnp.where(keep, pltpu.roll(b, d, axis=0), 0.0)
        b = a * b_prev + b
        a = a * a_prev
        d *= 2
    h = b + a * carry
    return h, h[n - 1:n, :]


def _mixer_body(streaming, n_streams, n_rows, *refs):
    S, L = n_streams, n_rows
    R = S * L
    refs = list(refs)
    x_ref = refs.pop(0)
    if streaming:
        pconv_ref, h0_ref = refs.pop(0), refs.pop(0)
    (nmix_ref, win_ref, gv_ref, gw_ref, gbias_ref, cw_ref, cb_ref, wa_ref, ba_ref, wx_ref, bx_ref,
     lam_ref, ga_ref, gb_ref, wout_ref) = refs[:15]
    refs = refs[15:]
    x1_ref = refs.pop(0)
    if streaming:
        v_ref = refs.pop(0)
    convnew_ref, hlast_ref = refs.pop(0), refs.pop(0)
    h_scr, z_scr, vb_scr, y_scr, xc_scr, xcb_scr, ra_scr, rx_scr, mix_scr, gwm_scr, carry_scr = refs

    DA = gv_ref.shape[1]
    DB = lam_ref.shape[1]
    G, NCH = gw_ref.shape[0], gw_ref.shape[1]
    GD = DA // G
    H, HD = wa_ref.shape[0], wa_ref.shape[1]
    W = cw_ref.shape[0]

    if streaming:
        first = pl.program_id(0) == 0
    else:
        t = pl.program_id(1)
        first = jnp.logical_and(pl.program_id(0) == 0, t == 0)

    @pl.when(first)
    def _():
        ri = lax.broadcasted_iota(jnp.int32, (NCH, NCH), 0)
        ci = lax.broadcasted_iota(jnp.int32, (NCH, NCH), 1)
        if streaming:
            sh = L.bit_length() - 1
            keep = lax.shift_right_logical(ri, sh) == lax.shift_right_logical(ci, sh)
        else:
            sh = CAUSAL_CHUNK.bit_length() - 1
            keep = lax.shift_right_logical(ri, sh) >= lax.shift_right_logical(ci, sh)
        for g in range(G):
            gwm_scr[g] = jnp.where(keep, gw_ref[g], 0.0).astype(BF)

    if streaming:
        y_scr[:, 0:HALO, :] = pconv_ref[...]
    else:
        @pl.when(t == 0)
        def _():
            y_scr[0, 0:HALO, :] = jnp.zeros((HALO, DB), F32)
            carry_scr[...] = jnp.zeros((1, DB), F32)

    nmix, gv, ga, gb = nmix_ref[...], gv_ref[...], ga_ref[...], gb_ref[...]
    cw, cb = [cw_ref[k:k + 1, :] for k in range(W)], cb_ref[...]
    ba, bx = ba_ref[...], bx_ref[...]
    sp_lam = _softplus(-lam_ref[...])
    carry = None if streaming else carry_scr[...]

    GRP = min(R, MIX_GROUP)
    per_group = GRP // L if streaming else 1
    for grp in range(R // GRP):
        r0 = grp * GRP
        rs = slice(r0, r0 + GRP)
        chunks = [slice(c0, c0 + MIX_CHUNK) for c0 in range(r0, r0 + GRP, MIX_CHUNK)]
        streams = range(grp * per_group, (grp + 1) * per_group)

        for c in chunks:
            h_scr[c, :] = _rms(x_ref[c, :], nmix).astype(BF)
        xr = _dot(h_scr[rs, :], win_ref[:, 2 * DA:2 * DA + DB])
        if streaming:
            y_scr[streams.start:streams.stop, HALO:HALO + L, :] = xr.reshape(per_group, L, DB)
        else:
            y_scr[0, HALO + r0:HALO + r0 + GRP, :] = xr
        z_scr[rs, DA:2 * DA] = _dot(h_scr[rs, :], win_ref[:, DA:2 * DA])

        if streaming:
            conv_in = [(y_scr[s], slice(s * L, (s + 1) * L)) for s in streams]
        else:
            conv_in = [(y_scr[0, c.start:c.stop + HALO, :], c) for c in chunks]
        for ext, c in conv_in:
            xc = _causal_conv(ext, cw, cb)
            xc_scr[c, :] = xc
            xcb_scr[c, :] = xc.astype(BF)
        z_scr[rs, 2 * DA:2 * DA + DB] = _dot(h_scr[rs, :], win_ref[:, 2 * DA + DB:2 * DA + 2 * DB])
        for hh in range(H):
            cs = slice(hh * HD, (hh + 1) * HD)
            ra_scr[rs, cs] = _dot(xcb_scr[rs, cs], wa_ref[hh].astype(BF))
            rx_scr[rs, cs] = _dot(xcb_scr[rs, cs], wx_ref[hh].astype(BF))

        for c in chunks:
            v = _rms(_gelu(z_scr[c, DA:2 * DA]), gv)
            if streaming:
                v_ref[c, :] = v
            vb_scr[c, :] = v.astype(BF)
        z_scr[rs, 0:DA] = _dot(h_scr[rs, :], win_ref[:, 0:DA])

        for c in chunks:
            r = jax.nn.sigmoid(ra_scr[c, :] + ba)
            i = jax.nn.sigmoid(rx_scr[c, :] + bx)
            a = jnp.exp(-LRU_C * r * sp_lam)
            ra_scr[c, :] = a
            rx_scr[c, :] = jnp.sqrt(1.0 - a * a) * (i * xc_scr[c, :])

        for n0 in range(r0, r0 + GRP, NCH):
            ns = slice(n0, n0 + NCH)
            for g in range(G):
                cs = slice(g * GD, (g + 1) * GD)
                sp = _dot(gwm_scr[g], vb_scr[ns, cs])
                z_scr[ns, cs] = _gelu(z_scr[ns, cs]) * (sp + gbias_ref[:, cs])
        for c in chunks:
            mix_scr[c, 0:DA] = _rms(z_scr[c, 0:DA], ga).astype(BF)
        x1_ref[rs, :] = x_ref[rs, :] + _dot(mix_scr[rs, 0:DA], wout_ref[0:DA, :])

        if streaming:
            for s in streams:
                c = slice(s * L, (s + 1) * L)
                h, last = _scan_chunk(ra_scr[c, :], rx_scr[c, :], h0_ref[s])
                rx_scr[c, :] = h
                hlast_ref[s] = last
        else:
            for c0 in range(r0, r0 + GRP, HALO):
                c = slice(c0, c0 + HALO)
                h, carry = _scan_chunk(ra_scr[c, :], rx_scr[c, :], carry)
                rx_scr[c, :] = h
        for c in chunks:
            out_b = rx_scr[c, :] * _gelu(z_scr[c, 2 * DA:2 * DA + DB])
            mix_scr[c, DA:DA + DB] = _rms(out_b, gb).astype(BF)
        x1_ref[rs, :] += _dot(mix_scr[rs, DA:DA + DB], wout_ref[DA:DA + DB, :])

    tail = slice(HALO + L - (W - 1), HALO + L)
    if streaming:
        for s in range(S):
            convnew_ref[s] = y_scr[s, tail, :]
    else:
        convnew_ref[...] = y_scr[0, tail, :]
        y_scr[0, 0:HALO, :] = y_scr[0, L:L + HALO, :]
        carry_scr[...] = carry
        hlast_ref[...] = carry


def _mixer(streaming, x, states, p):
    rows, D = x.shape
    DA, DB = p["g_v"].shape[1], p["lru_lam"].shape[1]
    weights = [p["norm_mix"], p["w_in"], p["g_v"], p["gmlp_w_eff"], p["gmlp_bias_rows"], p["lru_conv_w"],
               p["lru_conv_b"], p["lru_wa"], p["lru_ba"], p["lru_wx"], p["lru_bx"], p["lru_lam"], p["g_a"],
               p["g_b"], p["w_out"]]
    w_specs = [_const_spec(w.shape) for w in weights]
    W = p["lru_conv_w"].shape[0]
    if streaming:
        pconv, h0 = states
        N = h0.shape[0]
        L = rows // N
        S = SAMPLE_MIX_STREAMS
        R = S * L
        grid = (N // S,)
        row_spec = lambda c: pl.BlockSpec((R, c), lambda i: (i, 0))
        in_specs = [row_spec(D), pl.BlockSpec((S, HALO, DB), lambda i: (i, 0, 0)),
                    pl.BlockSpec((S, 1, DB), lambda i: (i, 0, 0))] + w_specs
        out_specs = [row_spec(D), row_spec(DA), pl.BlockSpec((S, W - 1, DB), lambda i: (i, 0, 0)),
                     pl.BlockSpec((S, 1, DB), lambda i: (i, 0, 0))]
        out_shape = [jax.ShapeDtypeStruct((rows, D), F32), jax.ShapeDtypeStruct((rows, DA), F32),
                     jax.ShapeDtypeStruct((N, W - 1, DB), F32), jax.ShapeDtypeStruct((N, 1, DB), F32)]
        args = [x, pconv, h0] + weights
    else:
        B, T = states
        S, L, R = 1, MIX_ROWS, MIX_ROWS
        nt = T // R
        grid = (B, nt)
        row_spec = lambda c: pl.BlockSpec((R, c), lambda b, t: (b * nt + t, 0))
        in_specs = [row_spec(D)] + w_specs
        out_specs = [row_spec(D), pl.BlockSpec((None, W - 1, DB), lambda b, t: (b, 0, 0)),
                     pl.BlockSpec((None, 1, DB), lambda b, t: (b, 0, 0))]
        out_shape = [jax.ShapeDtypeStruct((rows, D), F32), jax.ShapeDtypeStruct((B, W - 1, DB), F32),
                     jax.ShapeDtypeStruct((B, 1, DB), F32)]
        args = [x] + weights
    G, NCH = p["gmlp_w_eff"].shape[0], p["gmlp_w_eff"].shape[1]
    scratch = [pltpu.VMEM((R, D), BF),
               pltpu.VMEM((R, 2 * DA + DB), F32),
               pltpu.VMEM((R, DA), BF),
               pltpu.VMEM((S, L + HALO, DB), F32),
               pltpu.VMEM((R, DB), F32),
               pltpu.VMEM((R, DB), BF),
               pltpu.VMEM((R, DB), F32),
               pltpu.VMEM((R, DB), F32),
               pltpu.VMEM((R, DA + DB), BF),
               pltpu.VMEM((G, NCH, NCH), BF),
               pltpu.VMEM((1, DB), F32)]
    return pl.pallas_call(
        functools.partial(_mixer_body, streaming, S, L),
        grid=grid, in_specs=in_specs, out_specs=out_specs, out_shape=out_shape, scratch_shapes=scratch,
        compiler_params=_params(len(grid)),
        name="mixer_sample" if streaming else "mixer_prompt",
    )(*args)


def _attend(q_scr, o_scr, s_scr, p_scr, row0, n_rows, keys, values, head_dim):
    n_heads = q_scr.shape[1] // head_dim
    scale = head_dim ** -0.5
    rows = pl.ds(row0, n_rows)
    n_chunk = min(n_rows, SOFTMAX_ROWS)
    heads = [slice(h * head_dim, (h + 1) * head_dim) for h in range(n_heads)]
    for h, cs in enumerate(heads):
        s_scr[h] = lax.dot_general(q_scr[rows, cs], keys(h), (((1,), (1,)), ((), ())),
                                   preferred_element_type=F32) * scale
    for h in range(n_heads):
        for r0 in range(0, n_rows, n_chunk):
            s = s_scr[h, r0:r0 + n_chunk, :]
            e = jnp.exp(s - jnp.max(s, axis=-1, keepdims=True))
            p_scr[h, r0:r0 + n_chunk, :] = (e / jnp.sum(e, axis=-1, keepdims=True)).astype(BF)
    for h, cs in enumerate(heads):
        o_scr[rows, cs] = _dot(p_scr[h], values(h)).astype(BF)


def _project_q(x_ref, nxa_ref, wq_ref, h_scr, q_scr):
    g = nxa_ref[...]
    for r0 in range(0, x_ref.shape[0], Q_GROUP):
        for c0 in range(r0, r0 + Q_GROUP, MIX_CHUNK):
            c = slice(c0, c0 + MIX_CHUNK)
            h_scr[c, :] = _rms(x_ref[c, :], g).astype(BF)
        rs = slice(r0, r0 + Q_GROUP)
        q_scr[rs, :] = _dot(h_scr[rs, :], wq_ref[...]).astype(BF)


def _attn_prompt_body(head_dim, x_ref, nxa_ref, wq_ref, k_ref, v_ref, wo_ref, o_ref,
                      h_scr, q_scr, a_scr, s_scr, p_scr):
    _project_q(x_ref, nxa_ref, wq_ref, h_scr, q_scr)
    head = lambda h: slice(h * head_dim, (h + 1) * head_dim)
    _attend(q_scr, a_scr, s_scr, p_scr, 0, x_ref.shape[0], lambda h: k_ref[:, head(h)],
            lambda h: v_ref[:, head(h)], head_dim)
    o_ref[...] = x_ref[...] + _dot(a_scr[...], wo_ref[...])


def _attn_sample_body(head_dim, n_streams, n_rows, x_ref, nxa_ref, wq_ref, k_hbm, v_hbm, wo_ref, o_ref,
                      h_scr, q_scr, a_scr, s_scr, p_scr, kv_buf, kv_sem):
    i = pl.program_id(0)
    n_steps = pl.num_programs(0)
    n_heads = q_scr.shape[1] // head_dim
    slot = lax.rem(i, 2)

    def copies(step, dst_slot):
        out = []
        for which, hbm in enumerate((k_hbm, v_hbm)):
            for s in range(n_streams):
                for h in range(n_heads):
                    out.append(pltpu.make_async_copy(hbm.at[step * n_streams + s, :, h, :],
                                                     kv_buf.at[dst_slot, which, s, h],
                                                     kv_sem.at[dst_slot, which, s, h]))
        return out

    @pl.when(i == 0)
    def _():
        for cp in copies(0, 0):
            cp.start()

    @pl.when(i + 1 < n_steps)
    def _():
        for cp in copies(i + 1, 1 - slot):
            cp.start()

    @pl.when(i == 0)
    def _():
        _project_q(x_ref, nxa_ref, wq_ref, h_scr, q_scr)

    for cp in copies(i, slot):
        cp.wait()

    for s in range(n_streams):
        row0 = pl.multiple_of((i * n_streams + s) * n_rows, n_rows)
        _attend(q_scr, a_scr, s_scr.at[s], p_scr.at[s], row0, n_rows,
                lambda h: kv_buf[slot, 0, s, h].astype(BF), lambda h: kv_buf[slot, 1, s, h].astype(BF), head_dim)

    @pl.when(i == n_steps - 1)
    def _():
        o_ref[...] = x_ref[...] + _dot(a_scr[...], wo_ref[...])


def _attn_prompt(x, k, v, p, n_heads, batch):
    rows, D = x.shape
    M = k.shape[0] // batch
    R = ATT_ROWS
    nt = rows // batch // R
    row_spec = pl.BlockSpec((R, D), lambda b, t: (b * nt + t, 0))
    mem_spec = pl.BlockSpec((M, D), lambda b, t: (b, 0))
    return pl.pallas_call(
        functools.partial(_attn_prompt_body, D // n_heads),
        grid=(batch, nt),
        in_specs=[row_spec, _const_spec((1, D)), _const_spec((D, D)), mem_spec, mem_spec, _const_spec((D, D))],
        out_specs=row_spec,
        out_shape=jax.ShapeDtypeStruct((rows, D), F32),
        scratch_shapes=[pltpu.VMEM((R, D), BF), pltpu.VMEM((R, D), BF), pltpu.VMEM((R, D), BF),
                        pltpu.VMEM((n_heads, R, M), F32), pltpu.VMEM((n_heads, R, M), BF)],
        compiler_params=_params(2),
        name="attn_prompt",
    )(x, p["norm_xa"], p["w_q"], k, v, p["w_o"])


def _attn_sample(x, k, v, p):
    rows, D = x.shape
    N, M, n_heads, head_dim = k.shape
    L = rows // N
    S = SAMPLE_ATT_STREAMS
    hbm = pl.BlockSpec(memory_space=pl.ANY)
    return pl.pallas_call(
        functools.partial(_attn_sample_body, head_dim, S, L),
        grid=(N // S,),
        in_specs=[_const_spec((rows, D)), _const_spec((1, D)), _const_spec((D, D)), hbm, hbm,
                  _const_spec((D, D))],
        out_specs=pl.BlockSpec((rows, D), lambda i: (0, 0)),
        out_shape=jax.ShapeDtypeStruct((rows, D), F32),
        scratch_shapes=[pltpu.VMEM((rows, D), BF), pltpu.VMEM((rows, D), BF), pltpu.VMEM((rows, D), BF),
                        pltpu.VMEM((S, n_heads, L, M), F32), pltpu.VMEM((S, n_heads, L, M), BF),
                        pltpu.VMEM((2, 2, S, n_heads, M, head_dim), F32),
                        pltpu.SemaphoreType.DMA((2, 2, S, n_heads))],
        compiler_params=_params(1),
        name="attn_sample",
    )(x, p["norm_xa"], p["w_q"], k, v, p["w_o"])


def _ffn_body(streaming, n_streams, n_rows, *refs):
    S, L = n_streams, n_rows
    R = S * L
    refs = list(refs)
    x_ref = refs.pop(0)
    if streaming:
        st_ref = refs.pop(0)
    (nffn_ref, wua_ref, wug_ref, cwa_ref, cwg_ref, cba_ref, cbg_ref, wd_ref, nfin_ref) = refs[:9]
    refs = refs[9:]
    y_ref, new_ref = refs.pop(0), refs.pop(0)
    if streaming:
        casts = ((wua_ref, refs.pop(0)), (wug_ref, refs.pop(0)), (wd_ref, refs.pop(0)))
    h_scr, acc_scr, ya_scr, yg_scr, act_scr, carry_scr = refs
    W = cwa_ref.shape[0]
    C = wua_ref.shape[1]

    if streaming:
        j = pl.program_id(1)
    else:
        t, j = pl.program_id(1), pl.program_id(2)
    nj = pl.num_programs(2 - int(streaming))

    @pl.when(j == 0)
    def _():
        g = nffn_ref[...]
        def norm(r0):
            rows = pl.ds(r0, ROWS)
            h_scr[rows, :] = _rms(x_ref[rows, :], g).astype(BF)
            acc_scr[rows, :] = jnp.zeros((ROWS, acc_scr.shape[1]), F32)
        _rows_loop(R, ROWS, norm, unroll=2 * LOOP_UNROLL)
        if not streaming:
            @pl.when(t == 0)
            def _():
                carry_scr[...] = jnp.zeros(carry_scr.shape, F32)

    if streaming:
        for src, dst in casts:
            dst[...] = src[...].astype(BF)
        (_, wua_ref), (_, wug_ref), (_, wd_ref) = casts

    tail = slice(HALO + L - (W - 1), HALO + L)
    n_chunk = min(L, FFN_CHUNK)
    for c in range(C // FFN_SUB):
        cs = slice(c * FFN_SUB, (c + 1) * FFN_SUB)
        for half, (y_scr, w_ref) in enumerate(((ya_scr, wua_ref), (yg_scr, wug_ref))):
            y_scr[:, HALO:HALO + L, cs] = _dot(h_scr[...], w_ref[:, cs]).reshape(S, L, FFN_SUB)
            if streaming:
                for s in range(S):
                    y_scr[s, HALO - (W - 1):HALO, cs] = st_ref[s, half, :, cs]
                    new_ref[s, half, :, cs] = y_scr[s, tail, cs]
            else:
                y_scr[0, 0:HALO, cs] = carry_scr[j, half, :, cs]
                carry_scr[j, half, :, cs] = y_scr[0, L:L + HALO, cs]
                new_ref[half, :, cs] = y_scr[0, tail, cs]

        taps_a = [jnp.broadcast_to(cwa_ref[k:k + 1, cs], (n_chunk, FFN_SUB)) for k in range(W)]
        taps_g = [jnp.broadcast_to(cwg_ref[k:k + 1, cs], (n_chunk, FFN_SUB)) for k in range(W)]
        bias_a = jnp.broadcast_to(cba_ref[:, cs], (n_chunk, FFN_SUB))
        bias_g = jnp.broadcast_to(cbg_ref[:, cs], (n_chunk, FFN_SUB))
        for s in range(S):
            for r0 in range(0, L, n_chunk):
                ext = slice(r0, r0 + n_chunk + HALO)
                lin = _causal_conv(ya_scr[s, ext, cs], taps_a, bias_a)
                gated = _causal_conv(yg_scr[s, ext, cs], taps_g, bias_g)
                out0 = s * L + r0
                act_scr[out0:out0 + n_chunk, cs] = (_gelu(gated) * lin).astype(BF)

    for ks in (slice(0, C - FFN_SUB), slice(C - FFN_SUB, C)) if C > FFN_SUB else (slice(0, C),):
        acc_scr[...] += _dot(act_scr[:, ks], wd_ref[ks, :])

    @pl.when(j == nj - 1)
    def _():
        g = nfin_ref[...]
        def final(r0):
            rows = pl.ds(r0, ROWS)
            y_ref[rows, :] = _rms(x_ref[rows, :] + acc_scr[rows, :], g)
        _rows_loop(R, ROWS, final, unroll=2 * LOOP_UNROLL)


def _ffn(streaming, x, state, p, w_up_lin, w_up_gated, w_down):
    rows, D = x.shape
    DFF = w_down.shape[0]
    W = p["ffn_conv_w"].shape[0]
    C = FFN_SAMPLE_COLS if streaming else FFN_COLS
    nj = DFF // C
    (wua, off_a), (wug, off_g) = w_up_lin, w_up_gated
    if streaming:
        N = state.shape[0]
        L = rows // N
        R = FFN_ROWS
        S = R // L
        grid = (rows // R, nj)
        row_map = lambda i, j: (i, 0)
        colmap = lambda off: (lambda i, j: (0, j + off))
        wd_map = lambda i, j: (j, 0)
        new_spec = pl.BlockSpec((S, 2, W - 1, C), lambda i, j: (i, 0, 0, j))
        new_shape = jax.ShapeDtypeStruct((N, 2, W - 1, DFF), F32)
        in_specs = [pl.BlockSpec((R, D), row_map),
                    pl.BlockSpec((S, 2, W - 1, C), lambda i, j: (i, 0, 0, j))]
        args = [x, state]
    else:
        B, T = state
        S, L, R = 1, FFN_ROWS, FFN_ROWS
        nt = T // R
        grid = (B, nt, nj)
        row_map = lambda b, t, j: (b * nt + t, 0)
        colmap = lambda off: (lambda b, t, j: (0, j + off))
        wd_map = lambda b, t, j: (j, 0)
        new_spec = pl.BlockSpec((None, None, 2, W - 1, C), lambda b, t, j: (b, t, 0, 0, j))
        new_shape = jax.ShapeDtypeStruct((B, nt, 2, W - 1, DFF), F32)
        in_specs = [pl.BlockSpec((R, D), row_map)]
        args = [x]
    in_specs += [_const_spec((1, D)),
                 pl.BlockSpec((D, C), colmap(off_a // C)), pl.BlockSpec((D, C), colmap(off_g // C)),
                 pl.BlockSpec((W, C), colmap(0)), pl.BlockSpec((W, C), colmap(nj)),
                 pl.BlockSpec((1, C), colmap(0)), pl.BlockSpec((1, C), colmap(nj)),
                 pl.BlockSpec((C, D), wd_map), _const_spec((1, D))]
    args += [p["norm_ffn"], wua, wug, p["ffn_conv_w"], p["ffn_conv_w"], p["ffn_conv_b"],
             p["ffn_conv_b"], w_down, p["norm_final"]]
    out_specs = [pl.BlockSpec((R, D), row_map), new_spec]
    out_shape = [jax.ShapeDtypeStruct((rows, D), F32), new_shape]
    if streaming:
        out_specs += [pl.BlockSpec((D, C), colmap(0)), pl.BlockSpec((D, C), colmap(0)), pl.BlockSpec((C, D), wd_map)]
        out_shape += [jax.ShapeDtypeStruct((D, DFF), BF), jax.ShapeDtypeStruct((D, DFF), BF),
                      jax.ShapeDtypeStruct((DFF, D), BF)]
    scratch = [pltpu.VMEM((R, D), BF),
               pltpu.VMEM((R, D), F32),
               pltpu.VMEM((S, L + HALO, C), F32),
               pltpu.VMEM((S, L + HALO, C), F32),
               pltpu.VMEM((R, C), BF),
               pltpu.VMEM((nj, 2, HALO, C), F32)]
    return pl.pallas_call(
        functools.partial(_ffn_body, streaming, S, L),
        grid=grid, in_specs=in_specs, out_specs=out_specs, out_shape=out_shape,
        scratch_shapes=scratch,
        compiler_params=_params(len(grid)),
        name="ffn_sample" if streaming else "ffn_prompt",
    )(*args)


def kernel(x_prompt, x_sample, mem_prompt, cache_mem_k, cache_mem_v, state_lru_h, state_lru_conv, state_ffn_conv, norm_mix, w_in, g_v, gmlp_w, gmlp_b, lru_conv_w, lru_conv_b, lru_wa, lru_ba, lru_wx, lru_bx, lru_lam, g_a, g_b, w_out, norm_mem, w_kv, norm_xa, w_q, w_o, norm_ffn, w_up, ffn_conv_w, ffn_conv_b, w_down, norm_final):
    depth = w_in.shape[0]
    assert depth == 1, "single-layer trunk"
    B, T, D = x_prompt.shape
    N, L, _ = x_sample.shape
    M = mem_prompt.shape[1]
    n_heads, head_dim = cache_mem_k.shape[3], cache_mem_k.shape[4]
    G, NCH = gmlp_w.shape[1], gmlp_w.shape[2]
    DA, DB = g_v.shape[1], lru_lam.shape[1]
    GD = DA // G
    DFF = w_down.shape[1]
    W_LRU, W_FFN = lru_conv_w.shape[1], ffn_conv_w.shape[1]
    assert L <= CAUSAL_CHUNK and NCH % L == 0 and L == ROWS
    row = lambda a: a.reshape(1, -1)

    shared = {
        "norm_mix": row(norm_mix[0]), "g_v": row(g_v[0]),
        "lru_conv_w": lru_conv_w[0], "lru_conv_b": row(lru_conv_b[0]),
        "lru_wa": lru_wa[0], "lru_ba": row(lru_ba[0]),
        "lru_wx": lru_wx[0], "lru_bx": row(lru_bx[0]), "lru_lam": row(lru_lam[0]),
        "g_a": row(g_a[0]), "g_b": row(g_b[0]),
        "norm_xa": row(norm_xa[0]),
        "norm_ffn": row(norm_ffn[0]), "ffn_conv_w": ffn_conv_w[0],
        "ffn_conv_b": row(ffn_conv_b[0]), "norm_final": row(norm_final),
    }
    (mk, mv, mk_b, mv_b), resident = _memory_kv(mem_prompt.reshape(B * M, D), row(norm_mem[0]), w_kv[0],
                                                [w_in[0], w_out[0], w_q[0], w_o[0]], head_dim)
    shared.update(zip(("w_in", "w_out", "w_q", "w_o"), resident))
    prompt_p = dict(shared, gmlp_w_eff=gmlp_w[0],
                    gmlp_bias_rows=jnp.repeat(gmlp_b[0].T, GD, axis=1))
    reps = NCH // L
    sample_p = dict(shared, gmlp_w_eff=jnp.tile(gmlp_w[0][:, :L, :L], (1, reps, reps)),
                    gmlp_bias_rows=jnp.repeat(jnp.tile(gmlp_b[0][:, :L].T, (reps, 1)), GD, axis=1))

    xs = x_sample.reshape(N * L, D)
    pconv = jnp.pad(state_lru_conv[0], ((0, 0), (HALO - (W_LRU - 1), 0), (0, 0)))
    xs, v_s, conv_s, h_s = _mixer(True, xs, (pconv, state_lru_h[0].reshape(N, 1, DB)), sample_p)
    xs = _attn_sample(xs, cache_mem_k[0], cache_mem_v[0], sample_p)
    ffn_prev = jnp.swapaxes(state_ffn_conv[0].reshape(N, W_FFN - 1, 2, DFF), 1, 2)
    y_s, ffn_s, w_up_lin, w_up_gated, w_down_b = _ffn(True, xs, ffn_prev, sample_p,
                                                      (w_up[0], 0), (w_up[0], DFF), w_down[0])

    xp = x_prompt.reshape(B * T, D)
    xp, conv_p, h_p = _mixer(False, xp, (B, T), prompt_p)
    xp = _attn_prompt(xp, mk_b, mv_b, prompt_p, n_heads, B)
    y_p, ffn_p = _ffn(False, xp, (B, T), prompt_p, (w_up_lin, 0), (w_up_gated, 0), w_down_b)

    def ffn_state(a):
        return jnp.swapaxes(a, 1, 2).reshape(1, a.shape[0], W_FFN - 1, 2 * DFF)

    return (y_p.reshape(B, T, D), y_s.reshape(N, L, D),
            mk.reshape(1, B, M, n_heads, head_dim), mv.reshape(1, B, M, n_heads, head_dim),
            h_p.reshape(1, B, DB), conv_p.reshape(1, B, W_LRU - 1, DB), ffn_state(ffn_p[:, -1]),
            h_s.reshape(1, N, DB), conv_s.reshape(1, N, W_LRU - 1, DB), ffn_state(ffn_s),
            v_s.reshape(1, N, L, DA))
```

```python
import functools

import jax
import jax.numpy as jnp
from jax import lax
from jax.experimental import pallas as pl
from jax.experimental.pallas import tpu as pltpu

F32 = jnp.float32
BF = jnp.bfloat16
U32 = jnp.uint32

EPS = 1e-6
LRU_C = 8.0
CAUSAL_CHUNK = 64
HALO = 8
ROWS = 16
LOOP_UNROLL = 4
SOFTMAX_ROWS = 64
Q_GROUP = 128
MIX_CHUNK = 32
MIX_GROUP = 256
V7X_VMEM_BYTES = 64 * 1024 * 1024
VMEM_LIMIT = V7X_VMEM_BYTES - 8 * 1024 * 1024

MIX_ROWS = 256
ATT_ROWS = 512
FFN_ROWS = 512
FFN_COLS = 1024
FFN_SAMPLE_COLS = 256
FFN_SUB = 256
FFN_CHUNK = 64
KV_COLS = 256
SAMPLE_MIX_STREAMS = 16
SAMPLE_ATT_STREAMS = 2


class _Packed:
    def __init__(self, ref):
        self.ref = ref

    @property
    def shape(self):
        return self.ref.shape[:-2] + (2 * self.ref.shape[-2], self.ref.shape[-1])

    @property
    def at(self):
        packed = self

        class _At:
            def __getitem__(self, idx):
                return _Packed(packed.ref.at[idx])
        return _At()

    def _index(self, idx):
        nd = len(self.ref.shape)
        idx = idx if isinstance(idx, tuple) else (idx,)
        if Ellipsis in idx:
            k = idx.index(Ellipsis)
            idx = idx[:k] + (slice(None),) * (nd - len(idx) + 1) + idx[k + 1:]
        idx = list(idx) + [slice(None)] * (nd - len(idx))
        rows = idx[nd - 2]
        if isinstance(rows, slice):
            half = lambda v: None if v is None else v // 2
            assert rows.step is None and (rows.start or 0) % 16 == 0 and (rows.stop is None or rows.stop % 16 == 0)
            idx[nd - 2] = slice(half(rows.start), half(rows.stop))
        else:
            assert rows.size % 16 == 0
            start = rows.start // 2 if isinstance(rows.start, int) else pl.multiple_of(rows.start // 2, 8)
            idx[nd - 2] = pl.ds(start, rows.size // 2)
        return tuple(idx)

    def __getitem__(self, idx):
        return pltpu.bitcast(self.ref[self._index(idx)], BF)

    def __setitem__(self, idx, value):
        self.ref[self._index(idx)] = pltpu.bitcast(value.astype(BF), U32)


def _packed_shape(rows, cols, lead=()):
    return pltpu.VMEM(tuple(lead) + (rows // 2, cols), U32)


def _rms(x, g):
    return x * lax.rsqrt(jnp.mean(x * x, axis=-1, keepdims=True) + EPS) * g


def _gelu(x):
    return x * (0.5 * (1.0 + jnp.tanh(0.7978845608028654 * (x + 0.044715 * (x * x * x)))))


def _softplus(x):
    return jnp.maximum(x, 0.0) + jnp.log1p(jnp.exp(-jnp.abs(x)))


def _dot(a, b):
    return jnp.dot(a, b, preferred_element_type=F32)


def _rows_loop(n_rows, chunk, fn, unroll=LOOP_UNROLL):
    def body(i, carry):
        fn(pl.multiple_of(i * chunk, chunk))
        return carry
    trips = n_rows // chunk
    lax.fori_loop(0, trips, body, 0, unroll=min(unroll, trips))


def _causal_conv(ext, taps, bias):
    width = len(taps)
    acc = None
    for k in range(width):
        shift = width - 1 - k
        src = ext if shift == 0 else pltpu.roll(ext, shift, axis=0)
        term = src[HALO:, :] * taps[k]
        acc = term if acc is None else acc + term
    return acc + bias


def _const_spec(shape):
    nd = len(shape)
    return pl.BlockSpec(shape, lambda *_: (0,) * nd, pipeline_mode=pl.Buffered(1))


def _params(n_grid):
    return pltpu.CompilerParams(dimension_semantics=("arbitrary",) * n_grid, vmem_limit_bytes=VMEM_LIMIT)


def _kv_body(n_casts, head_dim, mem_ref, g_ref, wk_ref, wv_ref, *refs):
    cast_in, refs = refs[:n_casts], refs[n_casts:]
    k_hbm, v_hbm, kb_ref, vb_ref = refs[:4]
    cast_out = refs[4:4 + n_casts]
    h_scr, stage, sem = refs[4 + n_casts:]
    h_scr = _Packed(h_scr)
    cast_out = [_Packed(r) for r in cast_out]
    j, n_steps = pl.program_id(0), pl.num_programs(0)
    slot = lax.rem(j, 2)
    cols = wk_ref.shape[1]
    per_head = head_dim // cols

    def stores(step, slot):
        head = lax.div(step, per_head)
        c0 = pl.multiple_of(lax.rem(step, per_head) * cols, cols)
        return [pltpu.make_async_copy(stage.at[slot, which], hbm.at[:, head, pl.ds(c0, cols)], sem.at[slot, which])
                for which, hbm in enumerate((k_hbm, v_hbm))]

    @pl.when(j == 0)
    def _():
        g = g_ref[...]
        def norm(r0):
            rows = pl.ds(r0, ROWS)
            h_scr[rows, :] = _rms(mem_ref[rows, :], g)
        _rows_loop(mem_ref.shape[0], ROWS, norm)

    @pl.when(j >= 2)
    def _():
        for cp in stores(j - 2, slot):
            cp.wait()

    k = _dot(h_scr[...], wk_ref[...].astype(BF))
    stage[slot, 0] = k
    kb_ref[...] = k.astype(BF)
    v = _dot(h_scr[...], wv_ref[...].astype(BF))
    stage[slot, 1] = v
    vb_ref[...] = v.astype(BF)
    for cp in stores(j, slot):
        cp.start()
    for src, dst in zip(cast_in, cast_out):
        dst[...] = src[...]

    @pl.when(j == n_steps - 1)
    def _():
        for cp in stores(j - 1, 1 - slot) + stores(j, slot):
            cp.wait()


def _memory_kv(mem, g, w_kv, resident, head_dim):
    m, d = mem.shape
    n_steps = d // KV_COLS
    assert n_steps >= 2 and head_dim % KV_COLS == 0
    col = pl.BlockSpec((m, KV_COLS), lambda j: (0, j))
    hbm = pl.BlockSpec(memory_space=pl.ANY)
    cast_specs = [pl.BlockSpec((w.shape[0] // n_steps, w.shape[1]), lambda j: (j, 0)) for w in resident]
    packed_specs = [pl.BlockSpec((w.shape[0] // n_steps // 2, w.shape[1]), lambda j: (j, 0)) for w in resident]
    kv_shape = jax.ShapeDtypeStruct((m, d // head_dim, head_dim), F32)
    outs = pl.pallas_call(
        functools.partial(_kv_body, len(resident), head_dim),
        grid=(n_steps,),
        in_specs=[_const_spec((m, d)), _const_spec((1, d)),
                  pl.BlockSpec((d, KV_COLS), lambda j: (0, j)),
                  pl.BlockSpec((d, KV_COLS), lambda j: (0, j + n_steps))] + cast_specs,
        out_specs=[hbm, hbm, col, col] + packed_specs,
        out_shape=[kv_shape, kv_shape, jax.ShapeDtypeStruct((m, d), BF), jax.ShapeDtypeStruct((m, d), BF)]
                  + [jax.ShapeDtypeStruct((w.shape[0] // 2, w.shape[1]), U32) for w in resident],
        scratch_shapes=[_packed_shape(m, d), pltpu.VMEM((2, 2, m, KV_COLS), F32),
                        pltpu.SemaphoreType.DMA((2, 2))],
        compiler_params=_params(1),
        name="memory_kv",
    )(mem, g, w_kv, w_kv, *resident)
    return outs[:4], outs[4:]


def _scan_chunk(a, b, carry):
    n = a.shape[0]
    pos = lax.broadcasted_iota(jnp.int32, a.shape, 0)
    d = 1
    while d < n:
        keep = pos >= d
        a_prev = jnp.where(keep, pltpu.roll(a, d, axis=0), 1.0)
        b_prev = jnp.where(keep, pltpu.roll(b, d, axis=0), 0.0)
        b = a * b_prev + b
        a = a * a_prev
        d *= 2
    h = b + a * carry
    return h, h[n - 1:n, :]


def _mixer_body(streaming, n_streams, n_rows, *refs):
    S, L = n_streams, n_rows
    R = S * L
    refs = list(refs)
    x_ref = refs.pop(0)
    if streaming:
        pconv_ref, h0_ref = refs.pop(0), refs.pop(0)
    (nmix_ref, win_ref, gv_ref, gw_ref, gbias_ref, cw_ref, cb_ref, wa_ref, ba_ref, wx_ref, bx_ref,
     lam_ref, ga_ref, gb_ref, wout_ref) = refs[:15]
    refs = refs[15:]
    x1_ref = refs.pop(0)
    if streaming:
        v_ref = refs.pop(0)
    convnew_ref, hlast_ref = refs.pop(0), refs.pop(0)
    h_scr, z_scr, vb_scr, y_scr, xc_scr, xcb_scr, ra_scr, rx_scr, mix_scr, gwm_scr, carry_scr = refs
    h_scr, vb_scr, xcb_scr, mix_scr, gwm_scr = (_Packed(r) for r in (h_scr, vb_scr, xcb_scr, mix_scr, gwm_scr))
    win_ref, wout_ref = _Packed(win_ref), _Packed(wout_ref)

    DA = gv_ref.shape[1]
    DB = lam_ref.shape[1]
    G, NCH = gw_ref.shape[0], gw_ref.shape[1]
    GD = DA // G
    H, HD = wa_ref.shape[0], wa_ref.shape[1]
    W = cw_ref.shape[0]

    if streaming:
        first = pl.program_id(0) == 0
    else:
        t = pl.program_id(1)
        first = jnp.logical_and(pl.program_id(0) == 0, t == 0)

    @pl.when(first)
    def _():
        ri = lax.broadcasted_iota(jnp.int32, (NCH, NCH), 0)
        ci = lax.broadcasted_iota(jnp.int32, (NCH, NCH), 1)
        if streaming:
            sh = L.bit_length() - 1
            keep = lax.shift_right_logical(ri, sh) == lax.shift_right_logical(ci, sh)
        else:
            sh = CAUSAL_CHUNK.bit_length() - 1
            keep = lax.shift_right_logical(ri, sh) >= lax.shift_right_logical(ci, sh)
        for g in range(G):
            gwm_scr[g] = jnp.where(keep, gw_ref[g], 0.0)

    if streaming:
        y_scr[:, 0:HALO, :] = pconv_ref[...]
    else:
        @pl.when(t == 0)
        def _():
            y_scr[0, 0:HALO, :] = jnp.zeros((HALO, DB), F32)
            carry_scr[...] = jnp.zeros((1, DB), F32)

    nmix, gv, ga, gb = nmix_ref[...], gv_ref[...], ga_ref[...], gb_ref[...]
    cw, cb = [cw_ref[k:k + 1, :] for k in range(W)], cb_ref[...]
    ba, bx = ba_ref[...], bx_ref[...]
    sp_lam = _softplus(-lam_ref[...])
    carry = None if streaming else carry_scr[...]

    GRP = min(R, MIX_GROUP)
    per_group = GRP // L if streaming else 1
    for grp in range(R // GRP):
        r0 = grp * GRP
        rs = slice(r0, r0 + GRP)
        chunks = [slice(c0, c0 + MIX_CHUNK) for c0 in range(r0, r0 + GRP, MIX_CHUNK)]
        streams = range(grp * per_group, (grp + 1) * per_group)

        for c in chunks:
            h_scr[c, :] = _rms(x_ref[c, :], nmix)
        xr = _dot(h_scr[rs, :], win_ref[:, 2 * DA:2 * DA + DB])
        if streaming:
            y_scr[streams.start:streams.stop, HALO:HALO + L, :] = xr.reshape(per_group, L, DB)
        else:
            y_scr[0, HALO + r0:HALO + r0 + GRP, :] = xr
        z_scr[rs, DA:2 * DA] = _dot(h_scr[rs, :], win_ref[:, DA:2 * DA])

        if streaming:
            conv_in = [(y_scr[s], slice(s * L, (s + 1) * L)) for s in streams]
        else:
            conv_in = [(y_scr[0, c.start:c.stop + HALO, :], c) for c in chunks]
        for ext, c in conv_in:
            xc = _causal_conv(ext, cw, cb)
            xc_scr[c, :] = xc
            xcb_scr[c, :] = xc
        z_scr[rs, 2 * DA:2 * DA + DB] = _dot(h_scr[rs, :], win_ref[:, 2 * DA + DB:2 * DA + 2 * DB])
        for hh in range(H):
            cs = slice(hh * HD, (hh + 1) * HD)
            ra_scr[rs, cs] = _dot(xcb_scr[rs, cs], wa_ref[hh].astype(BF))
            rx_scr[rs, cs] = _dot(xcb_scr[rs, cs], wx_ref[hh].astype(BF))

        for c in chunks:
            v = _rms(_gelu(z_scr[c, DA:2 * DA]), gv)
            if streaming:
                v_ref[c, :] = v
            vb_scr[c, :] = v
        z_scr[rs, 0:DA] = _dot(h_scr[rs, :], win_ref[:, 0:DA])

        for c in chunks:
            r = jax.nn.sigmoid(ra_scr[c, :] + ba)
            i = jax.nn.sigmoid(rx_scr[c, :] + bx)
            a = jnp.exp(-LRU_C * r * sp_lam)
            ra_scr[c, :] = a
            rx_scr[c, :] = jnp.sqrt(1.0 - a * a) * (i * xc_scr[c, :])

        for n0 in range(r0, r0 + GRP, NCH):
            ns = slice(n0, n0 + NCH)
            for g in range(G):
                cs = slice(g * GD, (g + 1) * GD)
                sp = _dot(gwm_scr[g], vb_scr[ns, cs])
                z_scr[ns, cs] = _gelu(z_scr[ns, cs]) * (sp + gbias_ref[:, cs])
        for c in chunks:
            mix_scr[c, 0:DA] = _rms(z_scr[c, 0:DA], ga)
        x1_ref[rs, :] = x_ref[rs, :] + _dot(mix_scr[rs, 0:DA], wout_ref[0:DA, :])

        if streaming:
            for s in streams:
                c = slice(s * L, (s + 1) * L)
                h, last = _scan_chunk(ra_scr[c, :], rx_scr[c, :], h0_ref[s])
                rx_scr[c, :] = h
                hlast_ref[s] = last
        else:
            for c0 in range(r0, r0 + GRP, HALO):
                c = slice(c0, c0 + HALO)
                h, carry = _scan_chunk(ra_scr[c, :], rx_scr[c, :], carry)
                rx_scr[c, :] = h
        for c in chunks:
            out_b = rx_scr[c, :] * _gelu(z_scr[c, 2 * DA:2 * DA + DB])
            mix_scr[c, DA:DA + DB] = _rms(out_b, gb)
        x1_ref[rs, :] += _dot(mix_scr[rs, DA:DA + DB], wout_ref[DA:DA + DB, :])

    tail = slice(HALO + L - (W - 1), HALO + L)
    if streaming:
        for s in range(S):
            convnew_ref[s] = y_scr[s, tail, :]
    else:
        convnew_ref[...] = y_scr[0, tail, :]
        y_scr[0, 0:HALO, :] = y_scr[0, L:L + HALO, :]
        carry_scr[...] = carry
        hlast_ref[...] = carry


def _mixer(streaming, x, states, p):
    rows, D = x.shape
    DA, DB = p["g_v"].shape[1], p["lru_lam"].shape[1]
    weights = [p["norm_mix"], p["w_in"], p["g_v"], p["gmlp_w_eff"], p["gmlp_bias_rows"], p["lru_conv_w"],
               p["lru_conv_b"], p["lru_wa"], p["lru_ba"], p["lru_wx"], p["lru_bx"], p["lru_lam"], p["g_a"],
               p["g_b"], p["w_out"]]
    w_specs = [_const_spec(w.shape) for w in weights]
    W = p["lru_conv_w"].shape[0]
    if streaming:
        pconv, h0 = states
        N = h0.shape[0]
        L = rows // N
        S = SAMPLE_MIX_STREAMS
        R = S * L
        grid = (N // S,)
        row_spec = lambda c: pl.BlockSpec((R, c), lambda i: (i, 0))
        in_specs = [row_spec(D), pl.BlockSpec((S, HALO, DB), lambda i: (i, 0, 0)),
                    pl.BlockSpec((S, 1, DB), lambda i: (i, 0, 0))] + w_specs
        out_specs = [row_spec(D), row_spec(DA), pl.BlockSpec((S, W - 1, DB), lambda i: (i, 0, 0)),
                     pl.BlockSpec((S, 1, DB), lambda i: (i, 0, 0))]
        out_shape = [jax.ShapeDtypeStruct((rows, D), F32), jax.ShapeDtypeStruct((rows, DA), F32),
                     jax.ShapeDtypeStruct((N, W - 1, DB), F32), jax.ShapeDtypeStruct((N, 1, DB), F32)]
        args = [x, pconv, h0] + weights
    else:
        B, T = states
        S, L, R = 1, MIX_ROWS, MIX_ROWS
        nt = T // R
        grid = (B, nt)
        row_spec = lambda c: pl.BlockSpec((R, c), lambda b, t: (b * nt + t, 0))
        in_specs = [row_spec(D)] + w_specs
        out_specs = [row_spec(D), pl.BlockSpec((None, W - 1, DB), lambda b, t: (b, 0, 0)),
                     pl.BlockSpec((None, 1, DB), lambda b, t: (b, 0, 0))]
        out_shape = [jax.ShapeDtypeStruct((rows, D), F32), jax.ShapeDtypeStruct((B, W - 1, DB), F32),
                     jax.ShapeDtypeStruct((B, 1, DB), F32)]
        args = [x] + weights
    G, NCH = p["gmlp_w_eff"].shape[0], p["gmlp_w_eff"].shape[1]
    scratch = [_packed_shape(R, D),
               pltpu.VMEM((R, 2 * DA + DB), F32),
               _packed_shape(R, DA),
               pltpu.VMEM((S, L + HALO, DB), F32),
               pltpu.VMEM((R, DB), F32),
               _packed_shape(R, DB),
               pltpu.VMEM((R, DB), F32),
               pltpu.VMEM((R, DB), F32),
               _packed_shape(R, DA + DB),
               _packed_shape(NCH, NCH, (G,)),
               pltpu.VMEM((1, DB), F32)]
    return pl.pallas_call(
        functools.partial(_mixer_body, streaming, S, L),
        grid=grid, in_specs=in_specs, out_specs=out_specs, out_shape=out_shape, scratch_shapes=scratch,
        compiler_params=_params(len(grid)),
        name="mixer_sample" if streaming else "mixer_prompt",
    )(*args)


def _attend(q_scr, o_scr, s_scr, p_scr, row0, n_rows, keys, values, head_dim):
    n_heads = q_scr.shape[1] // head_dim
    scale = head_dim ** -0.5
    rows = pl.ds(row0, n_rows)
    n_chunk = min(n_rows, SOFTMAX_ROWS)
    heads = [slice(h * head_dim, (h + 1) * head_dim) for h in range(n_heads)]
    for h, cs in enumerate(heads):
        s_scr[h] = lax.dot_general(q_scr[rows, cs], keys(h), (((1,), (1,)), ((), ())),
                                   preferred_element_type=F32) * scale
    for h in range(n_heads):
        for r0 in range(0, n_rows, n_chunk):
            s = s_scr[h, r0:r0 + n_chunk, :]
            e = jnp.exp(s - jnp.max(s, axis=-1, keepdims=True))
            p_scr[h, r0:r0 + n_chunk, :] = e / jnp.sum(e, axis=-1, keepdims=True)
    for h, cs in enumerate(heads):
        o_scr[rows, cs] = _dot(p_scr[h], values(h))


def _project_q(x_ref, nxa_ref, wq_ref, h_scr, q_scr):
    g = nxa_ref[...]
    for r0 in range(0, x_ref.shape[0], Q_GROUP):
        for c0 in range(r0, r0 + Q_GROUP, MIX_CHUNK):
            c = slice(c0, c0 + MIX_CHUNK)
            h_scr[c, :] = _rms(x_ref[c, :], g)
        rs = slice(r0, r0 + Q_GROUP)
        q_scr[rs, :] = _dot(h_scr[rs, :], wq_ref[...])


def _attn_prompt_body(head_dim, x_ref, nxa_ref, wq_ref, k_ref, v_ref, wo_ref, o_ref,
                      h_scr, q_scr, a_scr, s_scr, p_scr):
    h_scr, q_scr, a_scr, p_scr, wq_ref, wo_ref = (_Packed(r) for r in (h_scr, q_scr, a_scr, p_scr, wq_ref, wo_ref))
    _project_q(x_ref, nxa_ref, wq_ref, h_scr, q_scr)
    head = lambda h: slice(h * head_dim, (h + 1) * head_dim)
    _attend(q_scr, a_scr, s_scr, p_scr, 0, x_ref.shape[0], lambda h: k_ref[:, head(h)],
            lambda h: v_ref[:, head(h)], head_dim)
    o_ref[...] = x_ref[...] + _dot(a_scr[...], wo_ref[...])


def _attn_sample_body(head_dim, n_streams, n_rows, x_ref, nxa_ref, wq_ref, k_hbm, v_hbm, wo_ref, o_ref,
                      h_scr, q_scr, a_scr, s_scr, p_scr, kv_buf, kv_sem):
    h_scr, q_scr, a_scr, p_scr, wq_ref, wo_ref = (_Packed(r) for r in (h_scr, q_scr, a_scr, p_scr, wq_ref, wo_ref))
    i = pl.program_id(0)
    n_steps = pl.num_programs(0)
    n_heads = q_scr.shape[1] // head_dim
    slot = lax.rem(i, 2)

    def copies(step, dst_slot):
        out = []
        for which, hbm in enumerate((k_hbm, v_hbm)):
            for s in range(n_streams):
                for h in range(n_heads):
                    out.append(pltpu.make_async_copy(hbm.at[step * n_streams + s, :, h, :],
                                                     kv_buf.at[dst_slot, which, s, h],
                                                     kv_sem.at[dst_slot, which, s, h]))
        return out

    @pl.when(i == 0)
    def _():
        for cp in copies(0, 0):
            cp.start()

    @pl.when(i + 1 < n_steps)
    def _():
        for cp in copies(i + 1, 1 - slot):
            cp.start()

    @pl.when(i == 0)
    def _():
        _project_q(x_ref, nxa_ref, wq_ref, h_scr, q_scr)

    for cp in copies(i, slot):
        cp.wait()

    for s in range(n_streams):
        row0 = pl.multiple_of((i * n_streams + s) * n_rows, n_rows)
        _attend(q_scr, a_scr, s_scr.at[s], p_scr.at[s], row0, n_rows,
                lambda h: kv_buf[slot, 0, s, h].astype(BF), lambda h: kv_buf[slot, 1, s, h].astype(BF), head_dim)

    @pl.when(i == n_steps - 1)
    def _():
        o_ref[...] = x_ref[...] + _dot(a_scr[...], wo_ref[...])


def _attn_prompt(x, k, v, p, n_heads, batch):
    rows, D = x.shape
    M = k.shape[0] // batch
    R = ATT_ROWS
    nt = rows // batch // R
    row_spec = pl.BlockSpec((R, D), lambda b, t: (b * nt + t, 0))
    mem_spec = pl.BlockSpec((M, D), lambda b, t: (b, 0))
    return pl.pallas_call(
        functools.partial(_attn_prompt_body, D // n_heads),
        grid=(batch, nt),
        in_specs=[row_spec, _const_spec((1, D)), _const_spec(p["w_q"].shape), mem_spec, mem_spec,
                  _const_spec(p["w_o"].shape)],
        out_specs=row_spec,
        out_shape=jax.ShapeDtypeStruct((rows, D), F32),
        scratch_shapes=[_packed_shape(R, D), _packed_shape(R, D), _packed_shape(R, D),
                        pltpu.VMEM((n_heads, R, M), F32), _packed_shape(R, M, (n_heads,))],
        compiler_params=_params(2),
        name="attn_prompt",
    )(x, p["norm_xa"], p["w_q"], k, v, p["w_o"])


def _attn_sample(x, k, v, p):
    rows, D = x.shape
    N, M, n_heads, head_dim = k.shape
    L = rows // N
    S = SAMPLE_ATT_STREAMS
    hbm = pl.BlockSpec(memory_space=pl.ANY)
    return pl.pallas_call(
        functools.partial(_attn_sample_body, head_dim, S, L),
        grid=(N // S,),
        in_specs=[_const_spec((rows, D)), _const_spec((1, D)), _const_spec(p["w_q"].shape), hbm, hbm,
                  _const_spec(p["w_o"].shape)],
        out_specs=pl.BlockSpec((rows, D), lambda i: (0, 0)),
        out_shape=jax.ShapeDtypeStruct((rows, D), F32),
        scratch_shapes=[_packed_shape(rows, D), _packed_shape(rows, D), _packed_shape(rows, D),
                        pltpu.VMEM((S, n_heads, L, M), F32), _packed_shape(L, M, (S, n_heads)),
                        pltpu.VMEM((2, 2, S, n_heads, M, head_dim), F32),
                        pltpu.SemaphoreType.DMA((2, 2, S, n_heads))],
        compiler_params=_params(1),
        name="attn_sample",
    )(x, p["norm_xa"], p["w_q"], k, v, p["w_o"])


def _ffn_body(streaming, n_streams, n_rows, *refs):
    S, L = n_streams, n_rows
    R = S * L
    refs = list(refs)
    x_ref = refs.pop(0)
    if streaming:
        st_ref = refs.pop(0)
    (nffn_ref, wua_ref, wug_ref, cwa_ref, cwg_ref, cba_ref, cbg_ref, wd_ref, nfin_ref) = refs[:9]
    refs = refs[9:]
    y_ref, new_ref = refs.pop(0), refs.pop(0)
    if streaming:
        casts = tuple((src, _Packed(refs.pop(0))) for src in (wua_ref, wug_ref, wd_ref))
    else:
        wua_ref, wug_ref, wd_ref = _Packed(wua_ref), _Packed(wug_ref), _Packed(wd_ref)
    h_scr, acc_scr, ya_scr, yg_scr, act_scr, carry_scr = refs
    h_scr, act_scr = _Packed(h_scr), _Packed(act_scr)
    W = cwa_ref.shape[0]
    C = wua_ref.shape[1]

    if streaming:
        j = pl.program_id(1)
    else:
        t, j = pl.program_id(1), pl.program_id(2)
    nj = pl.num_programs(2 - int(streaming))

    @pl.when(j == 0)
    def _():
        g = nffn_ref[...]
        def norm(r0):
            rows = pl.ds(r0, ROWS)
            h_scr[rows, :] = _rms(x_ref[rows, :], g)
            acc_scr[rows, :] = jnp.zeros((ROWS, acc_scr.shape[1]), F32)
        _rows_loop(R, ROWS, norm, unroll=2 * LOOP_UNROLL)
        if not streaming:
            @pl.when(t == 0)
            def _():
                carry_scr[...] = jnp.zeros(carry_scr.shape, F32)

    if streaming:
        for src, dst in casts:
            dst[...] = src[...]
        (_, wua_ref), (_, wug_ref), (_, wd_ref) = casts

    tail = slice(HALO + L - (W - 1), HALO + L)
    n_chunk = min(L, FFN_CHUNK)
    for c in range(C // FFN_SUB):
        cs = slice(c * FFN_SUB, (c + 1) * FFN_SUB)
        for half, (y_scr, w_ref) in enumerate(((ya_scr, wua_ref), (yg_scr, wug_ref))):
            y_scr[:, HALO:HALO + L, cs] = _dot(h_scr[...], w_ref[:, cs]).reshape(S, L, FFN_SUB)
            if streaming:
                for s in range(S):
                    y_scr[s, HALO - (W - 1):HALO, cs] = st_ref[s, half, :, cs]
                    new_ref[s, half, :, cs] = y_scr[s, tail, cs]
            else:
                y_scr[0, 0:HALO, cs] = carry_scr[j, half, :, cs]
                carry_scr[j, half, :, cs] = y_scr[0, L:L + HALO, cs]
                new_ref[half, :, cs] = y_scr[0, tail, cs]

        taps_a = [jnp.broadcast_to(cwa_ref[k:k + 1, cs], (n_chunk, FFN_SUB)) for k in range(W)]
        taps_g = [jnp.broadcast_to(cwg_ref[k:k + 1, cs], (n_chunk, FFN_SUB)) for k in range(W)]
        bias_a = jnp.broadcast_to(cba_ref[:, cs], (n_chunk, FFN_SUB))
        bias_g = jnp.broadcast_to(cbg_ref[:, cs], (n_chunk, FFN_SUB))
        for s in range(S):
            for r0 in range(0, L, n_chunk):
                ext = slice(r0, r0 + n_chunk + HALO)
                lin = _causal_conv(ya_scr[s, ext, cs], taps_a, bias_a)
                gated = _causal_conv(yg_scr[s, ext, cs], taps_g, bias_g)
                out0 = s * L + r0
                act_scr[out0:out0 + n_chunk, cs] = _gelu(gated) * lin

    for ks in (slice(0, C - FFN_SUB), slice(C - FFN_SUB, C)) if C > FFN_SUB else (slice(0, C),):
        acc_scr[...] += _dot(act_scr[:, ks], wd_ref[ks, :])

    @pl.when(j == nj - 1)
    def _():
        g = nfin_ref[...]
        def final(r0):
            rows = pl.ds(r0, ROWS)
            y_ref[rows, :] = _rms(x_ref[rows, :] + acc_scr[rows, :], g)
        _rows_loop(R, ROWS, final, unroll=2 * LOOP_UNROLL)


def _ffn(streaming, x, state, p, w_up_lin, w_up_gated, w_down):
    rows, D = x.shape
    DFF = w_down.shape[0] if streaming else 2 * w_down.shape[0]
    W = p["ffn_conv_w"].shape[0]
    C = FFN_SAMPLE_COLS if streaming else FFN_COLS
    nj = DFF // C
    (wua, off_a), (wug, off_g) = w_up_lin, w_up_gated
    if streaming:
        N = state.shape[0]
        L = rows // N
        R = FFN_ROWS
        S = R // L
        grid = (rows // R, nj)
        row_map = lambda i, j: (i, 0)
        colmap = lambda off: (lambda i, j: (0, j + off))
        wd_map = lambda i, j: (j, 0)
        new_spec = pl.BlockSpec((S, 2, W - 1, C), lambda i, j: (i, 0, 0, j))
        new_shape = jax.ShapeDtypeStruct((N, 2, W - 1, DFF), F32)
        in_specs = [pl.BlockSpec((R, D), row_map),
                    pl.BlockSpec((S, 2, W - 1, C), lambda i, j: (i, 0, 0, j))]
        args = [x, state]
    else:
        B, T = state
        S, L, R = 1, FFN_ROWS, FFN_ROWS
        nt = T // R
        grid = (B, nt, nj)
        row_map = lambda b, t, j: (b * nt + t, 0)
        colmap = lambda off: (lambda b, t, j: (0, j + off))
        wd_map = lambda b, t, j: (j, 0)
        new_spec = pl.BlockSpec((None, None, 2, W - 1, C), lambda b, t, j: (b, t, 0, 0, j))
        new_shape = jax.ShapeDtypeStruct((B, nt, 2, W - 1, DFF), F32)
        in_specs = [pl.BlockSpec((R, D), row_map)]
        args = [x]
    in_specs += [_const_spec((1, D)),
                 pl.BlockSpec((wua.shape[0], C), colmap(off_a // C)), pl.BlockSpec((wug.shape[0], C), colmap(off_g // C)),
                 pl.BlockSpec((W, C), colmap(0)), pl.BlockSpec((W, C), colmap(nj)),
                 pl.BlockSpec((1, C), colmap(0)), pl.BlockSpec((1, C), colmap(nj)),
                 pl.BlockSpec((C if streaming else C // 2, D), wd_map), _const_spec((1, D))]
    args += [p["norm_ffn"], wua, wug, p["ffn_conv_w"], p["ffn_conv_w"], p["ffn_conv_b"],
             p["ffn_conv_b"], w_down, p["norm_final"]]
    out_specs = [pl.BlockSpec((R, D), row_map), new_spec]
    out_shape = [jax.ShapeDtypeStruct((rows, D), F32), new_shape]
    if streaming:
        out_specs += [pl.BlockSpec((D // 2, C), colmap(0)), pl.BlockSpec((D // 2, C), colmap(0)),
                      pl.BlockSpec((C // 2, D), wd_map)]
        out_shape += [jax.ShapeDtypeStruct((D // 2, DFF), U32), jax.ShapeDtypeStruct((D // 2, DFF), U32),
                      jax.ShapeDtypeStruct((DFF // 2, D), U32)]
    scratch = [_packed_shape(R, D),
               pltpu.VMEM((R, D), F32),
               pltpu.VMEM((S, L + HALO, C), F32),
               pltpu.VMEM((S, L + HALO, C), F32),
               _packed_shape(R, C),
               pltpu.VMEM((nj, 2, HALO, C), F32)]
    return pl.pallas_call(
        functools.partial(_ffn_body, streaming, S, L),
        grid=grid, in_specs=in_specs, out_specs=out_specs, out_shape=out_shape,
        scratch_shapes=scratch,
        compiler_params=_params(len(grid)),
        name="ffn_sample" if streaming else "ffn_prompt",
    )(*args)


def kernel(x_prompt, x_sample, mem_prompt, cache_mem_k, cache_mem_v, state_lru_h, state_lru_conv, state_ffn_conv, norm_mix, w_in, g_v, gmlp_w, gmlp_b, lru_conv_w, lru_conv_b, lru_wa, lru_ba, lru_wx, lru_bx, lru_lam, g_a, g_b, w_out, norm_mem, w_kv, norm_xa, w_q, w_o, norm_ffn, w_up, ffn_conv_w, ffn_conv_b, w_down, norm_final):
    depth = w_in.shape[0]
    assert depth == 1, "single-layer trunk"
    B, T, D = x_prompt.shape
    N, L, _ = x_sample.shape
    M = mem_prompt.shape[1]
    n_heads, head_dim = cache_mem_k.shape[3], cache_mem_k.shape[4]
    G, NCH = gmlp_w.shape[1], gmlp_w.shape[2]
    DA, DB = g_v.shape[1], lru_lam.shape[1]
    GD = DA // G
    DFF = w_down.shape[1]
    W_LRU, W_FFN = lru_conv_w.shape[1], ffn_conv_w.shape[1]
    assert L <= CAUSAL_CHUNK and NCH % L == 0 and L == ROWS
    row = lambda a: a.reshape(1, -1)

    shared = {
        "norm_mix": row(norm_mix[0]), "g_v": row(g_v[0]),
        "lru_conv_w": lru_conv_w[0], "lru_conv_b": row(lru_conv_b[0]),
        "lru_wa": lru_wa[0], "lru_ba": row(lru_ba[0]),
        "lru_wx": lru_wx[0], "lru_bx": row(lru_bx[0]), "lru_lam": row(lru_lam[0]),
        "g_a": row(g_a[0]), "g_b": row(g_b[0]),
        "norm_xa": row(norm_xa[0]),
        "norm_ffn": row(norm_ffn[0]), "ffn_conv_w": ffn_conv_w[0],
        "ffn_conv_b": row(ffn_conv_b[0]), "norm_final": row(norm_final),
    }
    (mk, mv, mk_b, mv_b), resident = _memory_kv(mem_prompt.reshape(B * M, D), row(norm_mem[0]), w_kv[0],
                                                [w_in[0], w_out[0], w_q[0], w_o[0]], head_dim)
    shared.update(zip(("w_in", "w_out", "w_q", "w_o"), resident))
    prompt_p = dict(shared, gmlp_w_eff=gmlp_w[0],
                    gmlp_bias_rows=jnp.repeat(gmlp_b[0].T, GD, axis=1))
    reps = NCH // L
    sample_p = dict(shared, gmlp_w_eff=jnp.tile(gmlp_w[0][:, :L, :L], (1, reps, reps)),
                    gmlp_bias_rows=jnp.repeat(jnp.tile(gmlp_b[0][:, :L].T, (reps, 1)), GD, axis=1))

    xs = x_sample.reshape(N * L, D)
    pconv = jnp.pad(state_lru_conv[0], ((0, 0), (HALO - (W_LRU - 1), 0), (0, 0)))
    xs, v_s, conv_s, h_s = _mixer(True, xs, (pconv, state_lru_h[0].reshape(N, 1, DB)), sample_p)
    xs = _attn_sample(xs, cache_mem_k[0], cache_mem_v[0], sample_p)
    ffn_prev = jnp.swapaxes(state_ffn_conv[0].reshape(N, W_FFN - 1, 2, DFF), 1, 2)
    y_s, ffn_s, w_up_lin, w_up_gated, w_down_b = _ffn(True, xs, ffn_prev, sample_p,
                                                      (w_up[0], 0), (w_up[0], DFF), w_down[0])

    xp = x_prompt.reshape(B * T, D)
    xp, conv_p, h_p = _mixer(False, xp, (B, T), prompt_p)
    xp = _attn_prompt(xp, mk_b, mv_b, prompt_p, n_heads, B)
    y_p, ffn_p = _ffn(False, xp, (B, T), prompt_p, (w_up_lin, 0), (w_up_gated, 0), w_down_b)

    def ffn_state(a):
        return jnp.swapaxes(a, 1, 2).reshape(1, a.shape[0], W_FFN - 1, 2 * DFF)

    return (y_p.reshape(B, T, D), y_s.reshape(N, L, D),
            mk.reshape(1, B, M, n_heads, head_dim), mv.reshape(1, B, M, n_heads, head_dim),
            h_p.reshape(1, B, DB), conv_p.reshape(1, B, W_LRU - 1, DB), ffn_state(ffn_p[:, -1]),
            h_s.reshape(1, N, DB), conv_s.reshape(1, N, W_LRU - 1, DB), ffn_state(ffn_s),
            v_s.reshape(1, N, L, DA))
```

```python
import functools

import jax
import jax.numpy as jnp
from jax import lax
from jax.experimental import pallas as pl
from jax.experimental.pallas import tpu as pltpu

F32 = jnp.float32
BF = jnp.bfloat16
U32 = jnp.uint32

EPS = 1e-6
LRU_C = 8.0
CAUSAL_CHUNK = 64
HALO = 8
ROWS = 16
LOOP_UNROLL = 4
SOFTMAX_ROWS = 64
Q_GROUP = 128
MIX_CHUNK = 32
MIX_GROUP = 256
V7X_VMEM_BYTES = 64 * 1024 * 1024
VMEM_LIMIT = V7X_VMEM_BYTES - 8 * 1024 * 1024

MIX_ROWS = 256
ATT_ROWS = 512
FFN_ROWS = 512
FFN_COLS = 1024
FFN_SAMPLE_COLS = 256
FFN_SUB = 256
FFN_CHUNK = 64
KV_COLS = 256
SAMPLE_MIX_STREAMS = 16
SAMPLE_ATT_STREAMS = 2


class _Packed:
    def __init__(self, ref):
        self.ref = ref

    @property
    def shape(self):
        return self.ref.shape[:-2] + (2 * self.ref.shape[-2], self.ref.shape[-1])

    @property
    def at(self):
        packed = self

        class _At:
            def __getitem__(self, idx):
                return _Packed(packed.ref.at[idx])
        return _At()

    def _index(self, idx):
        nd = len(self.ref.shape)
        idx = idx if isinstance(idx, tuple) else (idx,)
        if Ellipsis in idx:
            k = idx.index(Ellipsis)
            idx = idx[:k] + (slice(None),) * (nd - len(idx) + 1) + idx[k + 1:]
        idx = list(idx) + [slice(None)] * (nd - len(idx))
        rows = idx[nd - 2]
        if isinstance(rows, slice):
            half = lambda v: None if v is None else v // 2
            assert rows.step is None and (rows.start or 0) % ROWS == 0 and (rows.stop is None or rows.stop % ROWS == 0)
            idx[nd - 2] = slice(half(rows.start), half(rows.stop))
        else:
            assert rows.size % ROWS == 0
            start = rows.start // 2 if isinstance(rows.start, int) else pl.multiple_of(rows.start // 2, ROWS // 2)
            idx[nd - 2] = pl.ds(start, rows.size // 2)
        return tuple(idx)

    def __getitem__(self, idx):
        return pltpu.bitcast(self.ref[self._index(idx)], BF)

    def __setitem__(self, idx, value):
        self.ref[self._index(idx)] = pltpu.bitcast(value.astype(BF), U32)


def _packed_shape(rows, cols, lead=()):
    return pltpu.VMEM(tuple(lead) + (rows // 2, cols), U32)


def _rms(x, g):
    return x * lax.rsqrt(jnp.mean(x * x, axis=-1, keepdims=True) + EPS) * g


def _gelu(x):
    return x * (0.5 * (1.0 + jnp.tanh(0.7978845608028654 * (x + 0.044715 * (x * x * x)))))


def _softplus(x):
    return jnp.maximum(x, 0.0) + jnp.log1p(jnp.exp(-jnp.abs(x)))


def _dot(a, b):
    return jnp.dot(a, b, preferred_element_type=F32)


def _rows_loop(n_rows, chunk, fn, unroll=LOOP_UNROLL):
    def body(i, carry):
        fn(pl.multiple_of(i * chunk, chunk))
        return carry
    trips = n_rows // chunk
    lax.fori_loop(0, trips, body, 0, unroll=min(unroll, trips))


def _causal_conv(ext, taps, bias):
    width = len(taps)
    acc = None
    for k in range(width):
        shift = width - 1 - k
        src = ext if shift == 0 else pltpu.roll(ext, shift, axis=0)
        term = src[HALO:, :] * taps[k]
        acc = term if acc is None else acc + term
    return acc + bias


def _const_spec(shape):
    nd = len(shape)
    return pl.BlockSpec(shape, lambda *_: (0,) * nd, pipeline_mode=pl.Buffered(1))


def _params(n_grid):
    return pltpu.CompilerParams(dimension_semantics=("arbitrary",) * n_grid, vmem_limit_bytes=VMEM_LIMIT)


def _kv_body(n_casts, head_dim, mem_ref, g_ref, wk_ref, wv_ref, *refs):
    cast_in, refs = refs[:n_casts], refs[n_casts:]
    k_hbm, v_hbm, kb_ref, vb_ref = refs[:4]
    cast_out = refs[4:4 + n_casts]
    h_scr, stage, sem = refs[4 + n_casts:]
    h_scr = _Packed(h_scr)
    cast_out = [_Packed(r) for r in cast_out]
    j, n_steps = pl.program_id(0), pl.num_programs(0)
    slot = lax.rem(j, 2)
    cols = wk_ref.shape[1]
    per_head = head_dim // cols

    def stores(step, slot):
        head = lax.div(step, per_head)
        c0 = pl.multiple_of(lax.rem(step, per_head) * cols, cols)
        return [pltpu.make_async_copy(stage.at[slot, which], hbm.at[:, head, pl.ds(c0, cols)], sem.at[slot, which])
                for which, hbm in enumerate((k_hbm, v_hbm))]

    @pl.when(j == 0)
    def _():
        g = g_ref[...]
        def norm(r0):
            rows = pl.ds(r0, ROWS)
            h_scr[rows, :] = _rms(mem_ref[rows, :], g)
        _rows_loop(mem_ref.shape[0], ROWS, norm)

    @pl.when(j >= 2)
    def _():
        for cp in stores(j - 2, slot):
            cp.wait()

    k = _dot(h_scr[...], wk_ref[...].astype(BF))
    stage[slot, 0] = k
    kb_ref[...] = k.astype(BF)
    v = _dot(h_scr[...], wv_ref[...].astype(BF))
    stage[slot, 1] = v
    vb_ref[...] = v.astype(BF)
    for cp in stores(j, slot):
        cp.start()
    for src, dst in zip(cast_in, cast_out):
        dst[...] = src[...]

    @pl.when(j == n_steps - 1)
    def _():
        for cp in stores(j - 1, 1 - slot) + stores(j, slot):
            cp.wait()


def _memory_kv(mem, g, w_kv, resident, head_dim):
    m, d = mem.shape
    n_steps = d // KV_COLS
    assert n_steps >= 2 and head_dim % KV_COLS == 0
    col = pl.BlockSpec((m, KV_COLS), lambda j: (0, j))
    hbm = pl.BlockSpec(memory_space=pl.ANY)
    cast_specs = [pl.BlockSpec((w.shape[0] // n_steps, w.shape[1]), lambda j: (j, 0)) for w in resident]
    packed_specs = [pl.BlockSpec((w.shape[0] // n_steps // 2, w.shape[1]), lambda j: (j, 0)) for w in resident]
    kv_shape = jax.ShapeDtypeStruct((m, d // head_dim, head_dim), F32)
    outs = pl.pallas_call(
        functools.partial(_kv_body, len(resident), head_dim),
        grid=(n_steps,),
        in_specs=[_const_spec((m, d)), _const_spec((1, d)),
                  pl.BlockSpec((d, KV_COLS), lambda j: (0, j)),
                  pl.BlockSpec((d, KV_COLS), lambda j: (0, j + n_steps))] + cast_specs,
        out_specs=[hbm, hbm, col, col] + packed_specs,
        out_shape=[kv_shape, kv_shape, jax.ShapeDtypeStruct((m, d), BF), jax.ShapeDtypeStruct((m, d), BF)]
                  + [jax.ShapeDtypeStruct((w.shape[0] // 2, w.shape[1]), U32) for w in resident],
        scratch_shapes=[_packed_shape(m, d), pltpu.VMEM((2, 2, m, KV_COLS), F32),
                        pltpu.SemaphoreType.DMA((2, 2))],
        compiler_params=_params(1),
        name="memory_kv",
    )(mem, g, w_kv, w_kv, *resident)
    return outs[:4], outs[4:]


def _scan_chunk(a, b, carry):
    n = a.shape[0]
    pos = lax.broadcasted_iota(jnp.int32, a.shape, 0)
    d = 1
    while d < n:
        keep = pos >= d
        a_prev = jnp.where(keep, pltpu.roll(a, d, axis=0), 1.0)
        b_prev = jnp.where(keep, pltpu.roll(b, d, axis=0), 0.0)
        b = a * b_prev + b
        a = a * a_prev
        d *= 2
    h = b + a * carry
    return h, h[n - 1:n, :]


def _mixer_body(streaming, n_streams, n_rows, *refs):
    S, L = n_streams, n_rows
    R = S * L
    refs = list(refs)
    x_ref = refs.pop(0)
    if streaming:
        pconv_ref, h0_ref = refs.pop(0), refs.pop(0)
    (nmix_ref, win_ref, gv_ref, gw_ref, gbias_ref, cw_ref, cb_ref, wa_ref, ba_ref, wx_ref, bx_ref,
     lam_ref, ga_ref, gb_ref, wout_ref) = refs[:15]
    refs = refs[15:]
    x1_ref = refs.pop(0)
    if streaming:
        v_ref = refs.pop(0)
    convnew_ref, hlast_ref = refs.pop(0), refs.pop(0)
    h_scr, z_scr, vb_scr, y_scr, xc_scr, xcb_scr, ra_scr, rx_scr, mix_scr, gwm_scr, carry_scr = refs
    h_scr, vb_scr, xcb_scr, mix_scr, gwm_scr = (_Packed(r) for r in (h_scr, vb_scr, xcb_scr, mix_scr, gwm_scr))
    win_ref, wout_ref = _Packed(win_ref), _Packed(wout_ref)

    DA = gv_ref.shape[1]
    DB = lam_ref.shape[1]
    G, NCH = gw_ref.shape[0], gw_ref.shape[1]
    GD = DA // G
    H, HD = wa_ref.shape[0], wa_ref.shape[1]
    W = cw_ref.shape[0]

    if streaming:
        first = pl.program_id(0) == 0
    else:
        t = pl.program_id(1)
        first = jnp.logical_and(pl.program_id(0) == 0, t == 0)

    @pl.when(first)
    def _():
        ri = lax.broadcasted_iota(jnp.int32, (NCH, NCH), 0)
        ci = lax.broadcasted_iota(jnp.int32, (NCH, NCH), 1)
        if streaming:
            sh = L.bit_length() - 1
            keep = lax.shift_right_logical(ri, sh) == lax.shift_right_logical(ci, sh)
        else:
            sh = CAUSAL_CHUNK.bit_length() - 1
            keep = lax.shift_right_logical(ri, sh) >= lax.shift_right_logical(ci, sh)
        for g in range(G):
            gwm_scr[g] = jnp.where(keep, gw_ref[g], 0.0)

    if streaming:
        y_scr[:, 0:HALO, :] = pconv_ref[...]
    else:
        @pl.when(t == 0)
        def _():
            y_scr[0, 0:HALO, :] = jnp.zeros((HALO, DB), F32)
            carry_scr[...] = jnp.zeros((1, DB), F32)

    nmix, gv, ga, gb = nmix_ref[...], gv_ref[...], ga_ref[...], gb_ref[...]
    cw, cb = [cw_ref[k:k + 1, :] for k in range(W)], cb_ref[...]
    ba, bx = ba_ref[...], bx_ref[...]
    sp_lam = _softplus(-lam_ref[...])
    carry = None if streaming else carry_scr[...]

    GRP = min(R, MIX_GROUP)
    per_group = GRP // L if streaming else 1
    for grp in range(R // GRP):
        r0 = grp * GRP
        rs = slice(r0, r0 + GRP)
        chunks = [slice(c0, c0 + MIX_CHUNK) for c0 in range(r0, r0 + GRP, MIX_CHUNK)]
        streams = range(grp * per_group, (grp + 1) * per_group)

        for c in chunks:
            h_scr[c, :] = _rms(x_ref[c, :], nmix)
        xr = _dot(h_scr[rs, :], win_ref[:, 2 * DA:2 * DA + DB])
        if streaming:
            y_scr[streams.start:streams.stop, HALO:HALO + L, :] = xr.reshape(per_group, L, DB)
        else:
            y_scr[0, HALO + r0:HALO + r0 + GRP, :] = xr
        z_scr[rs, DA:2 * DA] = _dot(h_scr[rs, :], win_ref[:, DA:2 * DA])

        if streaming:
            conv_in = [(y_scr[s], slice(s * L, (s + 1) * L)) for s in streams]
        else:
            conv_in = [(y_scr[0, c.start:c.stop + HALO, :], c) for c in chunks]
        for ext, c in conv_in:
            xc = _causal_conv(ext, cw, cb)
            xc_scr[c, :] = xc
            xcb_scr[c, :] = xc
        z_scr[rs, 2 * DA:2 * DA + DB] = _dot(h_scr[rs, :], win_ref[:, 2 * DA + DB:2 * DA + 2 * DB])
        for hh in range(H):
            cs = slice(hh * HD, (hh + 1) * HD)
            ra_scr[rs, cs] = _dot(xcb_scr[rs, cs], wa_ref[hh].astype(BF))
            rx_scr[rs, cs] = _dot(xcb_scr[rs, cs], wx_ref[hh].astype(BF))

        for c in chunks:
            v = _rms(_gelu(z_scr[c, DA:2 * DA]), gv)
            if streaming:
                v_ref[c, :] = v
            vb_scr[c, :] = v
        z_scr[rs, 0:DA] = _dot(h_scr[rs, :], win_ref[:, 0:DA])

        for c in chunks:
            r = jax.nn.sigmoid(ra_scr[c, :] + ba)
            i = jax.nn.sigmoid(rx_scr[c, :] + bx)
            a = jnp.exp(-LRU_C * r * sp_lam)
            ra_scr[c, :] = a
            rx_scr[c, :] = jnp.sqrt(1.0 - a * a) * (i * xc_scr[c, :])

        for n0 in range(r0, r0 + GRP, NCH):
            ns = slice(n0, n0 + NCH)
            for g in range(G):
                cs = slice(g * GD, (g + 1) * GD)
                sp = _dot(gwm_scr[g], vb_scr[ns, cs])
                z_scr[ns, cs] = _gelu(z_scr[ns, cs]) * (sp + gbias_ref[:, cs])
        for c in chunks:
            mix_scr[c, 0:DA] = _rms(z_scr[c, 0:DA], ga)
        x1_ref[rs, :] = x_ref[rs, :] + _dot(mix_scr[rs, 0:DA], wout_ref[0:DA, :])

        if streaming:
            for s in streams:
                c = slice(s * L, (s + 1) * L)
                h, last = _scan_chunk(ra_scr[c, :], rx_scr[c, :], h0_ref[s])
                rx_scr[c, :] = h
                hlast_ref[s] = last
        else:
            for c0 in range(r0, r0 + GRP, HALO):
                c = slice(c0, c0 + HALO)
                h, carry = _scan_chunk(ra_scr[c, :], rx_scr[c, :], carry)
                rx_scr[c, :] = h
        for c in chunks:
            out_b = rx_scr[c, :] * _gelu(z_scr[c, 2 * DA:2 * DA + DB])
            mix_scr[c, DA:DA + DB] = _rms(out_b, gb)
        x1_ref[rs, :] += _dot(mix_scr[rs, DA:DA + DB], wout_ref[DA:DA + DB, :])

    tail = slice(HALO + L - (W - 1), HALO + L)
    if streaming:
        for s in range(S):
            convnew_ref[s] = y_scr[s, tail, :]
    else:
        convnew_ref[...] = y_scr[0, tail, :]
        y_scr[0, 0:HALO, :] = y_scr[0, L:L + HALO, :]
        carry_scr[...] = carry
        hlast_ref[...] = carry


def _mixer(streaming, x, states, p):
    rows, D = x.shape
    DA, DB = p["g_v"].shape[1], p["lru_lam"].shape[1]
    weights = [p["norm_mix"], p["w_in"], p["g_v"], p["gmlp_w_eff"], p["gmlp_bias_rows"], p["lru_conv_w"],
               p["lru_conv_b"], p["lru_wa"], p["lru_ba"], p["lru_wx"], p["lru_bx"], p["lru_lam"], p["g_a"],
               p["g_b"], p["w_out"]]
    w_specs = [_const_spec(w.shape) for w in weights]
    W = p["lru_conv_w"].shape[0]
    if streaming:
        pconv, h0 = states
        N = h0.shape[0]
        L = rows // N
        S = SAMPLE_MIX_STREAMS
        R = S * L
        grid = (N // S,)
        row_spec = lambda c: pl.BlockSpec((R, c), lambda i: (i, 0))
        in_specs = [row_spec(D), pl.BlockSpec((S, HALO, DB), lambda i: (i, 0, 0)),
                    pl.BlockSpec((S, 1, DB), lambda i: (i, 0, 0))] + w_specs
        out_specs = [row_spec(D), row_spec(DA), pl.BlockSpec((S, W - 1, DB), lambda i: (i, 0, 0)),
                     pl.BlockSpec((S, 1, DB), lambda i: (i, 0, 0))]
        out_shape = [jax.ShapeDtypeStruct((rows, D), F32), jax.ShapeDtypeStruct((rows, DA), F32),
                     jax.ShapeDtypeStruct((N, W - 1, DB), F32), jax.ShapeDtypeStruct((N, 1, DB), F32)]
        args = [x, pconv, h0] + weights
    else:
        B, T = states
        S, L, R = 1, MIX_ROWS, MIX_ROWS
        nt = T // R
        grid = (B, nt)
        row_spec = lambda c: pl.BlockSpec((R, c), lambda b, t: (b * nt + t, 0))
        in_specs = [row_spec(D)] + w_specs
        out_specs = [row_spec(D), pl.BlockSpec((None, W - 1, DB), lambda b, t: (b, 0, 0)),
                     pl.BlockSpec((None, 1, DB), lambda b, t: (b, 0, 0))]
        out_shape = [jax.ShapeDtypeStruct((rows, D), F32), jax.ShapeDtypeStruct((B, W - 1, DB), F32),
                     jax.ShapeDtypeStruct((B, 1, DB), F32)]
        args = [x] + weights
    G, NCH = p["gmlp_w_eff"].shape[0], p["gmlp_w_eff"].shape[1]
    scratch = [_packed_shape(R, D),
               pltpu.VMEM((R, 2 * DA + DB), F32),
               _packed_shape(R, DA),
               pltpu.VMEM((S, L + HALO, DB), F32),
               pltpu.VMEM((R, DB), F32),
               _packed_shape(R, DB),
               pltpu.VMEM((R, DB), F32),
               pltpu.VMEM((R, DB), F32),
               _packed_shape(R, DA + DB),
               _packed_shape(NCH, NCH, (G,)),
               pltpu.VMEM((1, DB), F32)]
    return pl.pallas_call(
        functools.partial(_mixer_body, streaming, S, L),
        grid=grid, in_specs=in_specs, out_specs=out_specs, out_shape=out_shape, scratch_shapes=scratch,
        compiler_params=_params(len(grid)),
        name="mixer_sample" if streaming else "mixer_prompt",
    )(*args)


def _attend(q_scr, o_scr, s_scr, p_scr, row0, n_rows, keys, values, head_dim):
    n_heads = q_scr.shape[1] // head_dim
    scale = head_dim ** -0.5
    rows = pl.ds(row0, n_rows)
    n_chunk = min(n_rows, SOFTMAX_ROWS)
    heads = [slice(h * head_dim, (h + 1) * head_dim) for h in range(n_heads)]
    for h, cs in enumerate(heads):
        s_scr[h] = lax.dot_general(q_scr[rows, cs], keys(h), (((1,), (1,)), ((), ())),
                                   preferred_element_type=F32) * scale
    for h in range(n_heads):
        for r0 in range(0, n_rows, n_chunk):
            s = s_scr[h, r0:r0 + n_chunk, :]
            e = jnp.exp(s - jnp.max(s, axis=-1, keepdims=True))
            p_scr[h, r0:r0 + n_chunk, :] = e / jnp.sum(e, axis=-1, keepdims=True)
    for h, cs in enumerate(heads):
        o_scr[rows, cs] = _dot(p_scr[h], values(h))


def _project_q(x_ref, nxa_ref, wq_ref, h_scr, q_scr):
    g = nxa_ref[...]
    for r0 in range(0, x_ref.shape[0], Q_GROUP):
        for c0 in range(r0, r0 + Q_GROUP, MIX_CHUNK):
            c = slice(c0, c0 + MIX_CHUNK)
            h_scr[c, :] = _rms(x_ref[c, :], g)
        rs = slice(r0, r0 + Q_GROUP)
        q_scr[rs, :] = _dot(h_scr[rs, :], wq_ref[...])


def _attn_prompt_body(head_dim, x_ref, nxa_ref, wq_ref, k_ref, v_ref, wo_ref, o_ref,
                      h_scr, q_scr, a_scr, s_scr, p_scr):
    h_scr, q_scr, a_scr, p_scr, wq_ref, wo_ref = (_Packed(r) for r in (h_scr, q_scr, a_scr, p_scr, wq_ref, wo_ref))
    _project_q(x_ref, nxa_ref, wq_ref, h_scr, q_scr)
    head = lambda h: slice(h * head_dim, (h + 1) * head_dim)
    _attend(q_scr, a_scr, s_scr, p_scr, 0, x_ref.shape[0], lambda h: k_ref[:, head(h)],
            lambda h: v_ref[:, head(h)], head_dim)
    o_ref[...] = x_ref[...] + _dot(a_scr[...], wo_ref[...])


def _attn_sample_body(head_dim, n_streams, n_rows, x_ref, nxa_ref, wq_ref, k_hbm, v_hbm, wo_ref, o_ref,
                      h_scr, q_scr, a_scr, s_scr, p_scr, kv_buf, kv_sem):
    h_scr, q_scr, a_scr, p_scr, wq_ref, wo_ref = (_Packed(r) for r in (h_scr, q_scr, a_scr, p_scr, wq_ref, wo_ref))
    i = pl.program_id(0)
    n_steps = pl.num_programs(0)
    n_heads = q_scr.shape[1] // head_dim
    slot = lax.rem(i, 2)

    def copies(step, dst_slot):
        out = []
        for which, hbm in enumerate((k_hbm, v_hbm)):
            for s in range(n_streams):
                for h in range(n_heads):
                    out.append(pltpu.make_async_copy(hbm.at[step * n_streams + s, :, h, :],
                                                     kv_buf.at[dst_slot, which, s, h],
                                                     kv_sem.at[dst_slot, which, s, h]))
        return out

    @pl.when(i == 0)
    def _():
        for cp in copies(0, 0):
            cp.start()

    @pl.when(i + 1 < n_steps)
    def _():
        for cp in copies(i + 1, 1 - slot):
            cp.start()

    @pl.when(i == 0)
    def _():
        _project_q(x_ref, nxa_ref, wq_ref, h_scr, q_scr)

    for cp in copies(i, slot):
        cp.wait()

    for s in range(n_streams):
        row0 = pl.multiple_of((i * n_streams + s) * n_rows, n_rows)
        _attend(q_scr, a_scr, s_scr.at[s], p_scr.at[s], row0, n_rows,
                lambda h: kv_buf[slot, 0, s, h].astype(BF), lambda h: kv_buf[slot, 1, s, h].astype(BF), head_dim)

    @pl.when(i == n_steps - 1)
    def _():
        o_ref[...] = x_ref[...] + _dot(a_scr[...], wo_ref[...])


def _attn_prompt(x, k, v, p, n_heads, batch):
    rows, D = x.shape
    M = k.shape[0] // batch
    R = ATT_ROWS
    nt = rows // batch // R
    row_spec = pl.BlockSpec((R, D), lambda b, t: (b * nt + t, 0))
    mem_spec = pl.BlockSpec((M, D), lambda b, t: (b, 0))
    return pl.pallas_call(
        functools.partial(_attn_prompt_body, D // n_heads),
        grid=(batch, nt),
        in_specs=[row_spec, _const_spec((1, D)), _const_spec(p["w_q"].shape), mem_spec, mem_spec,
                  _const_spec(p["w_o"].shape)],
        out_specs=row_spec,
        out_shape=jax.ShapeDtypeStruct((rows, D), F32),
        scratch_shapes=[_packed_shape(R, D), _packed_shape(R, D), _packed_shape(R, D),
                        pltpu.VMEM((n_heads, R, M), F32), _packed_shape(R, M, (n_heads,))],
        compiler_params=_params(2),
        name="attn_prompt",
    )(x, p["norm_xa"], p["w_q"], k, v, p["w_o"])


def _attn_sample(x, k, v, p):
    rows, D = x.shape
    N, M, n_heads, head_dim = k.shape
    L = rows // N
    S = SAMPLE_ATT_STREAMS
    hbm = pl.BlockSpec(memory_space=pl.ANY)
    return pl.pallas_call(
        functools.partial(_attn_sample_body, head_dim, S, L),
        grid=(N // S,),
        in_specs=[_const_spec((rows, D)), _const_spec((1, D)), _const_spec(p["w_q"].shape), hbm, hbm,
                  _const_spec(p["w_o"].shape)],
        out_specs=pl.BlockSpec((rows, D), lambda i: (0, 0)),
        out_shape=jax.ShapeDtypeStruct((rows, D), F32),
        scratch_shapes=[_packed_shape(rows, D), _packed_shape(rows, D), _packed_shape(rows, D),
                        pltpu.VMEM((S, n_heads, L, M), F32), _packed_shape(L, M, (S, n_heads)),
                        pltpu.VMEM((2, 2, S, n_heads, M, head_dim), F32),
                        pltpu.SemaphoreType.DMA((2, 2, S, n_heads))],
        compiler_params=_params(1),
        name="attn_sample",
    )(x, p["norm_xa"], p["w_q"], k, v, p["w_o"])


def _ffn_body(streaming, n_streams, n_rows, *refs):
    S, L = n_streams, n_rows
    R = S * L
    refs = list(refs)
    x_ref = refs.pop(0)
    if streaming:
        st_ref = refs.pop(0)
    (nffn_ref, wua_ref, wug_ref, cw_ref, cb_ref, wd_ref, nfin_ref) = refs[:7]
    refs = refs[7:]
    y_ref, new_ref = refs.pop(0), refs.pop(0)
    if streaming:
        casts = tuple((src, _Packed(refs.pop(0))) for src in (wua_ref, wug_ref, wd_ref))
    else:
        wua_ref, wug_ref, wd_ref = _Packed(wua_ref), _Packed(wug_ref), _Packed(wd_ref)
    h_scr, acc_scr, ya_scr, yg_scr, act_scr, carry_scr = refs
    h_scr, act_scr = _Packed(h_scr), _Packed(act_scr)
    W = cw_ref.shape[0]
    DFF = cw_ref.shape[1] // 2
    C = wua_ref.shape[1]

    if streaming:
        j = pl.program_id(1)
    else:
        t, j = pl.program_id(1), pl.program_id(2)
    nj = pl.num_programs(2 - int(streaming))

    @pl.when(j == 0)
    def _():
        g = nffn_ref[...]
        def norm(r0):
            rows = pl.ds(r0, ROWS)
            h_scr[rows, :] = _rms(x_ref[rows, :], g)
            acc_scr[rows, :] = jnp.zeros((ROWS, acc_scr.shape[1]), F32)
        _rows_loop(R, ROWS, norm, unroll=2 * LOOP_UNROLL)
        if not streaming:
            @pl.when(t == 0)
            def _():
                carry_scr[...] = jnp.zeros(carry_scr.shape, F32)

    if streaming:
        for src, dst in casts:
            dst[...] = src[...]
        (_, wua_ref), (_, wug_ref), (_, wd_ref) = casts

    tail = slice(HALO + L - (W - 1), HALO + L)
    n_chunk = min(L, FFN_CHUNK)
    for c in range(C // FFN_SUB):
        cs = slice(c * FFN_SUB, (c + 1) * FFN_SUB)
        for half, (y_scr, w_ref) in enumerate(((ya_scr, wua_ref), (yg_scr, wug_ref))):
            y_scr[:, HALO:HALO + L, cs] = _dot(h_scr[...], w_ref[:, cs]).reshape(S, L, FFN_SUB)
            if streaming:
                for s in range(S):
                    y_scr[s, HALO - (W - 1):HALO, cs] = st_ref[s, half, :, cs]
                    new_ref[s, half, :, cs] = y_scr[s, tail, cs]
            else:
                y_scr[0, 0:HALO, cs] = carry_scr[j, half, :, cs]
                carry_scr[j, half, :, cs] = y_scr[0, L:L + HALO, cs]
                out_cols = pl.ds(pl.multiple_of(j * C + c * FFN_SUB, FFN_SUB), FFN_SUB)
                new_ref[half, :, out_cols] = y_scr[0, tail, cs]

        lin_cols = pl.ds(pl.multiple_of(j * C + c * FFN_SUB, FFN_SUB), FFN_SUB)
        gated_cols = pl.ds(pl.multiple_of(DFF + j * C + c * FFN_SUB, FFN_SUB), FFN_SUB)
        taps_a = [jnp.broadcast_to(cw_ref[k:k + 1, lin_cols], (n_chunk, FFN_SUB)) for k in range(W)]
        taps_g = [jnp.broadcast_to(cw_ref[k:k + 1, gated_cols], (n_chunk, FFN_SUB)) for k in range(W)]
        bias_a = jnp.broadcast_to(cb_ref[:, lin_cols], (n_chunk, FFN_SUB))
        bias_g = jnp.broadcast_to(cb_ref[:, gated_cols], (n_chunk, FFN_SUB))
        for s in range(S):
            for r0 in range(0, L, n_chunk):
                ext = slice(r0, r0 + n_chunk + HALO)
                lin = _causal_conv(ya_scr[s, ext, cs], taps_a, bias_a)
                gated = _causal_conv(yg_scr[s, ext, cs], taps_g, bias_g)
                out0 = s * L + r0
                act_scr[out0:out0 + n_chunk, cs] = _gelu(gated) * lin

    for ks in (slice(0, C - FFN_SUB), slice(C - FFN_SUB, C)) if C > FFN_SUB else (slice(0, C),):
        acc_scr[...] += _dot(act_scr[:, ks], wd_ref[ks, :])

    @pl.when(j == nj - 1)
    def _():
        g = nfin_ref[...]
        def final(r0):
            rows = pl.ds(r0, ROWS)
            y_ref[rows, :] = _rms(x_ref[rows, :] + acc_scr[rows, :], g)
        _rows_loop(R, ROWS, final, unroll=2 * LOOP_UNROLL)


def _ffn(streaming, x, state, p, w_up_lin, w_up_gated, w_down):
    rows, D = x.shape
    DFF = w_down.shape[0] if streaming else 2 * w_down.shape[0]
    W = p["ffn_conv_w"].shape[0]
    C = FFN_SAMPLE_COLS if streaming else FFN_COLS
    nj = DFF // C
    (wua, off_a), (wug, off_g) = w_up_lin, w_up_gated
    if streaming:
        N = state.shape[0]
        L = rows // N
        R = FFN_ROWS
        S = R // L
        grid = (rows // R, nj)
        row_map = lambda i, j: (i, 0)
        colmap = lambda off: (lambda i, j: (0, j + off))
        wd_map = lambda i, j: (j, 0)
        new_spec = pl.BlockSpec((S, 2, W - 1, C), lambda i, j: (i, 0, 0, j))
        new_shape = jax.ShapeDtypeStruct((N, 2, W - 1, DFF), F32)
        in_specs = [pl.BlockSpec((R, D), row_map),
                    pl.BlockSpec((S, 2, W - 1, C), lambda i, j: (i, 0, 0, j))]
        args = [x, state]
    else:
        B, T = state
        S, L, R = 1, FFN_ROWS, FFN_ROWS
        nt = T // R
        grid = (B, nt, nj)
        row_map = lambda b, t, j: (b * nt + t, 0)
        colmap = lambda off: (lambda b, t, j: (0, j + off))
        wd_map = lambda b, t, j: (j, 0)
        new_spec = pl.BlockSpec((None, None, 2, W - 1, DFF), lambda b, t, j: (b, t, 0, 0, 0))
        new_shape = jax.ShapeDtypeStruct((B, nt, 2, W - 1, DFF), F32)
        in_specs = [pl.BlockSpec((R, D), row_map)]
        args = [x]
    in_specs += [_const_spec((1, D)),
                 pl.BlockSpec((wua.shape[0], C), colmap(off_a // C)), pl.BlockSpec((wug.shape[0], C), colmap(off_g // C)),
                 _const_spec((W, 2 * DFF)), _const_spec((1, 2 * DFF)),
                 pl.BlockSpec((C if streaming else C // 2, D), wd_map), _const_spec((1, D))]
    args += [p["norm_ffn"], wua, wug, p["ffn_conv_w"], p["ffn_conv_b"], w_down, p["norm_final"]]
    out_specs = [pl.BlockSpec((R, D), row_map), new_spec]
    out_shape = [jax.ShapeDtypeStruct((rows, D), F32), new_shape]
    if streaming:
        out_specs += [pl.BlockSpec((D // 2, C), colmap(0)), pl.BlockSpec((D // 2, C), colmap(0)),
                      pl.BlockSpec((C // 2, D), wd_map)]
        out_shape += [jax.ShapeDtypeStruct((D // 2, DFF), U32), jax.ShapeDtypeStruct((D // 2, DFF), U32),
                      jax.ShapeDtypeStruct((DFF // 2, D), U32)]
    scratch = [_packed_shape(R, D),
               pltpu.VMEM((R, D), F32),
               pltpu.VMEM((S, L + HALO, C), F32),
               pltpu.VMEM((S, L + HALO, C), F32),
               _packed_shape(R, C),
               pltpu.VMEM((nj, 2, HALO, C), F32)]
    return pl.pallas_call(
        functools.partial(_ffn_body, streaming, S, L),
        grid=grid, in_specs=in_specs, out_specs=out_specs, out_shape=out_shape,
        scratch_shapes=scratch,
        compiler_params=_params(len(grid)),
        name="ffn_sample" if streaming else "ffn_prompt",
    )(*args)


def kernel(x_prompt, x_sample, mem_prompt, cache_mem_k, cache_mem_v, state_lru_h, state_lru_conv, state_ffn_conv, norm_mix, w_in, g_v, gmlp_w, gmlp_b, lru_conv_w, lru_conv_b, lru_wa, lru_ba, lru_wx, lru_bx, lru_lam, g_a, g_b, w_out, norm_mem, w_kv, norm_xa, w_q, w_o, norm_ffn, w_up, ffn_conv_w, ffn_conv_b, w_down, norm_final):
    depth = w_in.shape[0]
    assert depth == 1, "single-layer trunk"
    B, T, D = x_prompt.shape
    N, L, _ = x_sample.shape
    M = mem_prompt.shape[1]
    n_heads, head_dim = cache_mem_k.shape[3], cache_mem_k.shape[4]
    G, NCH = gmlp_w.shape[1], gmlp_w.shape[2]
    DA, DB = g_v.shape[1], lru_lam.shape[1]
    GD = DA // G
    DFF = w_down.shape[1]
    W_LRU, W_FFN = lru_conv_w.shape[1], ffn_conv_w.shape[1]
    assert L <= CAUSAL_CHUNK and NCH % L == 0 and L == ROWS
    row = lambda a: a.reshape(1, -1)

    shared = {
        "norm_mix": row(norm_mix[0]), "g_v": row(g_v[0]),
        "lru_conv_w": lru_conv_w[0], "lru_conv_b": row(lru_conv_b[0]),
        "lru_wa": lru_wa[0], "lru_ba": row(lru_ba[0]),
        "lru_wx": lru_wx[0], "lru_bx": row(lru_bx[0]), "lru_lam": row(lru_lam[0]),
        "g_a": row(g_a[0]), "g_b": row(g_b[0]),
        "norm_xa": row(norm_xa[0]),
        "norm_ffn": row(norm_ffn[0]), "ffn_conv_w": ffn_conv_w[0],
        "ffn_conv_b": row(ffn_conv_b[0]), "norm_final": row(norm_final),
    }
    (mk, mv, mk_b, mv_b), resident = _memory_kv(mem_prompt.reshape(B * M, D), row(norm_mem[0]), w_kv[0],
                                                [w_in[0], w_out[0], w_q[0], w_o[0]], head_dim)
    shared.update(zip(("w_in", "w_out", "w_q", "w_o"), resident))
    prompt_p = dict(shared, gmlp_w_eff=gmlp_w[0],
                    gmlp_bias_rows=jnp.repeat(gmlp_b[0].T, GD, axis=1))
    reps = NCH // L
    sample_p = dict(shared, gmlp_w_eff=jnp.tile(gmlp_w[0][:, :L, :L], (1, reps, reps)),
                    gmlp_bias_rows=jnp.repeat(jnp.tile(gmlp_b[0][:, :L].T, (reps, 1)), GD, axis=1))

    xs = x_sample.reshape(N * L, D)
    pconv = jnp.pad(state_lru_conv[0], ((0, 0), (HALO - (W_LRU - 1), 0), (0, 0)))
    xs, v_s, conv_s, h_s = _mixer(True, xs, (pconv, state_lru_h[0].reshape(N, 1, DB)), sample_p)
    xs = _attn_sample(xs, cache_mem_k[0], cache_mem_v[0], sample_p)
    ffn_prev = jnp.swapaxes(state_ffn_conv[0].reshape(N, W_FFN - 1, 2, DFF), 1, 2)
    y_s, ffn_s, w_up_lin, w_up_gated, w_down_b = _ffn(True, xs, ffn_prev, sample_p,
                                                      (w_up[0], 0), (w_up[0], DFF), w_down[0])

    xp = x_prompt.reshape(B * T, D)
    xp, conv_p, h_p = _mixer(False, xp, (B, T), prompt_p)
    xp = _attn_prompt(xp, mk_b, mv_b, prompt_p, n_heads, B)
    y_p, ffn_p = _ffn(False, xp, (B, T), prompt_p, (w_up_lin, 0), (w_up_gated, 0), w_down_b)

    def ffn_state(a):
        return jnp.swapaxes(a, 1, 2).reshape(1, a.shape[0], W_FFN - 1, 2 * DFF)

    return (y_p.reshape(B, T, D), y_s.reshape(N, L, D),
            mk.reshape(1, B, M, n_heads, head_dim), mv.reshape(1, B, M, n_heads, head_dim),
            h_p.reshape(1, B, DB), conv_p.reshape(1, B, W_LRU - 1, DB), ffn_state(ffn_p[:, -1]),
            h_s.reshape(1, N, DB), conv_s.reshape(1, N, W_LRU - 1, DB), ffn_state(ffn_s),
            v_s.reshape(1, N, L, DA))
```

```python
import functools

import jax
import jax.numpy as jnp
from jax import lax
from jax.experimental import pallas as pl
from jax.experimental.pallas import tpu as pltpu

F32 = jnp.float32
BF = jnp.bfloat16
U32 = jnp.uint32

EPS = 1e-6
LRU_C = 8.0
CAUSAL_CHUNK = 64
HALO = 8
ROWS = 16
LOOP_UNROLL = 4
SOFTMAX_ROWS = 64
Q_GROUP = 128
MIX_CHUNK = 32
MIX_GROUP = 256
V7X_VMEM_BYTES = 64 * 1024 * 1024
VMEM_LIMIT = V7X_VMEM_BYTES - 8 * 1024 * 1024

MIX_ROWS = 256
ATT_ROWS = 512
FFN_ROWS = 512
FFN_COLS = 1024
FFN_SAMPLE_COLS = 256
FFN_SUB = 256
FFN_CHUNK = 64
KV_COLS = 256
SAMPLE_MIX_STREAMS = 16
SAMPLE_ATT_STREAMS = 2


class _Packed:
    def __init__(self, ref):
        self.ref = ref

    @property
    def shape(self):
        return self.ref.shape[:-2] + (2 * self.ref.shape[-2], self.ref.shape[-1])

    @property
    def at(self):
        packed = self

        class _At:
            def __getitem__(self, idx):
                return _Packed(packed.ref.at[idx])
        return _At()

    def _index(self, idx):
        nd = len(self.ref.shape)
        idx = idx if isinstance(idx, tuple) else (idx,)
        if Ellipsis in idx:
            k = idx.index(Ellipsis)
            idx = idx[:k] + (slice(None),) * (nd - len(idx) + 1) + idx[k + 1:]
        idx = list(idx) + [slice(None)] * (nd - len(idx))
        rows = idx[nd - 2]
        if isinstance(rows, slice):
            half = lambda v: None if v is None else v // 2
            assert rows.step is None and (rows.start or 0) % ROWS == 0 and (rows.stop is None or rows.stop % ROWS == 0)
            idx[nd - 2] = slice(half(rows.start), half(rows.stop))
        else:
            assert rows.size % ROWS == 0
            start = rows.start // 2 if isinstance(rows.start, int) else pl.multiple_of(rows.start // 2, ROWS // 2)
            idx[nd - 2] = pl.ds(start, rows.size // 2)
        return tuple(idx)

    def __getitem__(self, idx):
        return pltpu.bitcast(self.ref[self._index(idx)], BF)

    def __setitem__(self, idx, value):
        self.ref[self._index(idx)] = pltpu.bitcast(value.astype(BF), U32)


def _packed_shape(rows, cols, lead=()):
    return pltpu.VMEM(tuple(lead) + (rows // 2, cols), U32)


def _rms(x, g):
    return x * lax.rsqrt(jnp.mean(x * x, axis=-1, keepdims=True) + EPS) * g


def _gelu(x):
    return x * (0.5 * (1.0 + jnp.tanh(0.7978845608028654 * (x + 0.044715 * (x * x * x)))))


def _sigmoid(x):
    return 0.5 + 0.5 * jnp.tanh(0.5 * x)


def _softplus(x):
    return jnp.maximum(x, 0.0) + jnp.log1p(jnp.exp(-jnp.abs(x)))


def _dot(a, b):
    return jnp.dot(a, b, preferred_element_type=F32)


def _rows_loop(n_rows, chunk, fn, unroll=LOOP_UNROLL):
    def body(i, carry):
        fn(pl.multiple_of(i * chunk, chunk))
        return carry
    trips = n_rows // chunk
    lax.fori_loop(0, trips, body, 0, unroll=min(unroll, trips))


def _causal_conv(ext, taps, bias):
    width = len(taps)
    acc = None
    for k in range(width):
        shift = width - 1 - k
        src = ext if shift == 0 else pltpu.roll(ext, shift, axis=0)
        term = src[HALO:, :] * taps[k]
        acc = term if acc is None else acc + term
    return acc + bias


def _const_spec(shape):
    nd = len(shape)
    return pl.BlockSpec(shape, lambda *_: (0,) * nd, pipeline_mode=pl.Buffered(1))


def _params(n_grid):
    return pltpu.CompilerParams(dimension_semantics=("arbitrary",) * n_grid, vmem_limit_bytes=VMEM_LIMIT)


def _kv_body(n_casts, head_dim, mem_ref, g_ref, wk_ref, wv_ref, *refs):
    cast_in, refs = refs[:n_casts], refs[n_casts:]
    k_hbm, v_hbm, kb_ref, vb_ref = refs[:4]
    cast_out = refs[4:4 + n_casts]
    h_scr, stage, sem = refs[4 + n_casts:]
    h_scr = _Packed(h_scr)
    cast_out = [_Packed(r) for r in cast_out]
    j, n_steps = pl.program_id(0), pl.num_programs(0)
    slot = lax.rem(j, 2)
    cols = wk_ref.shape[1]
    per_head = head_dim // cols

    def stores(step, slot):
        head = lax.div(step, per_head)
        c0 = pl.multiple_of(lax.rem(step, per_head) * cols, cols)
        return [pltpu.make_async_copy(stage.at[slot, which], hbm.at[:, head, pl.ds(c0, cols)], sem.at[slot, which])
                for which, hbm in enumerate((k_hbm, v_hbm))]

    @pl.when(j == 0)
    def _():
        g = g_ref[...]
        def norm(r0):
            rows = pl.ds(r0, ROWS)
            h_scr[rows, :] = _rms(mem_ref[rows, :], g)
        _rows_loop(mem_ref.shape[0], ROWS, norm)

    @pl.when(j >= 2)
    def _():
        for cp in stores(j - 2, slot):
            cp.wait()

    k = _dot(h_scr[...], wk_ref[...].astype(BF))
    stage[slot, 0] = k
    kb_ref[...] = k.astype(BF)
    v = _dot(h_scr[...], wv_ref[...].astype(BF))
    stage[slot, 1] = v
    vb_ref[...] = v.astype(BF)
    for cp in stores(j, slot):
        cp.start()
    for src, dst in zip(cast_in, cast_out):
        dst[...] = src[...]

    @pl.when(j == n_steps - 1)
    def _():
        for cp in stores(j - 1, 1 - slot) + stores(j, slot):
            cp.wait()


def _memory_kv(mem, g, w_kv, resident, head_dim):
    m, d = mem.shape
    n_steps = d // KV_COLS
    assert n_steps >= 2 and head_dim % KV_COLS == 0
    col = pl.BlockSpec((m, KV_COLS), lambda j: (0, j))
    hbm = pl.BlockSpec(memory_space=pl.ANY)
    cast_specs = [pl.BlockSpec((w.shape[0] // n_steps, w.shape[1]), lambda j: (j, 0)) for w in resident]
    packed_specs = [pl.BlockSpec((w.shape[0] // n_steps // 2, w.shape[1]), lambda j: (j, 0)) for w in resident]
    kv_shape = jax.ShapeDtypeStruct((m, d // head_dim, head_dim), F32)
    outs = pl.pallas_call(
        functools.partial(_kv_body, len(resident), head_dim),
        grid=(n_steps,),
        in_specs=[_const_spec((m, d)), _const_spec((1, d)),
                  pl.BlockSpec((d, KV_COLS), lambda j: (0, j)),
                  pl.BlockSpec((d, KV_COLS), lambda j: (0, j + n_steps))] + cast_specs,
        out_specs=[hbm, hbm, col, col] + packed_specs,
        out_shape=[kv_shape, kv_shape, jax.ShapeDtypeStruct((m, d), BF), jax.ShapeDtypeStruct((m, d), BF)]
                  + [jax.ShapeDtypeStruct((w.shape[0] // 2, w.shape[1]), U32) for w in resident],
        scratch_shapes=[_packed_shape(m, d), pltpu.VMEM((2, 2, m, KV_COLS), F32),
                        pltpu.SemaphoreType.DMA((2, 2))],
        compiler_params=_params(1),
        name="memory_kv",
    )(mem, g, w_kv, w_kv, *resident)
    return outs[:4], outs[4:]


def _scan_chunk(a, b, carry):
    n = a.shape[0]
    pos = lax.broadcasted_iota(jnp.int32, a.shape, 0)
    d = 1
    while d < n:
        keep = pos >= d
        a_prev = jnp.where(keep, pltpu.roll(a, d, axis=0), 1.0)
        b_prev = jnp.where(keep, pltpu.roll(b, d, axis=0), 0.0)
        b = a * b_prev + b
        a = a * a_prev
        d *= 2
    h = b + a * carry
    return h, h[n - 1:n, :]


def _mixer_body(streaming, n_streams, n_rows, *refs):
    S, L = n_streams, n_rows
    R = S * L
    refs = list(refs)
    x_ref = refs.pop(0)
    if streaming:
        pconv_ref, h0_ref = refs.pop(0), refs.pop(0)
    (nmix_ref, win_ref, gv_ref, gw_ref, gbias_ref, cw_ref, cb_ref, wa_ref, ba_ref, wx_ref, bx_ref,
     lam_ref, ga_ref, gb_ref, wout_ref) = refs[:15]
    refs = refs[15:]
    x1_ref = refs.pop(0)
    if streaming:
        v_ref = refs.pop(0)
    convnew_ref, hlast_ref = refs.pop(0), refs.pop(0)
    h_scr, z_scr, vb_scr, y_scr, xc_scr, xcb_scr, ra_scr, rx_scr, mix_scr, gwm_scr, carry_scr = refs
    h_scr, vb_scr, xcb_scr, mix_scr, gwm_scr = (_Packed(r) for r in (h_scr, vb_scr, xcb_scr, mix_scr, gwm_scr))
    win_ref, wout_ref = _Packed(win_ref), _Packed(wout_ref)

    DA = gv_ref.shape[1]
    DB = lam_ref.shape[1]
    G, NCH = gw_ref.shape[0], gw_ref.shape[1]
    GD = DA // G
    H, HD = wa_ref.shape[0], wa_ref.shape[1]
    W = cw_ref.shape[0]

    if streaming:
        first = pl.program_id(0) == 0
    else:
        t = pl.program_id(1)
        first = jnp.logical_and(pl.program_id(0) == 0, t == 0)

    @pl.when(first)
    def _():
        ri = lax.broadcasted_iota(jnp.int32, (NCH, NCH), 0)
        ci = lax.broadcasted_iota(jnp.int32, (NCH, NCH), 1)
        if streaming:
            sh = L.bit_length() - 1
            keep = lax.shift_right_logical(ri, sh) == lax.shift_right_logical(ci, sh)
        else:
            sh = CAUSAL_CHUNK.bit_length() - 1
            keep = lax.shift_right_logical(ri, sh) >= lax.shift_right_logical(ci, sh)
        for g in range(G):
            gwm_scr[g] = jnp.where(keep, gw_ref[g], 0.0)

    if streaming:
        y_scr[:, 0:HALO, :] = pconv_ref[...]
    else:
        @pl.when(t == 0)
        def _():
            y_scr[0, 0:HALO, :] = jnp.zeros((HALO, DB), F32)
            carry_scr[...] = jnp.zeros((1, DB), F32)

    nmix, gv, ga, gb = nmix_ref[...], gv_ref[...], ga_ref[...], gb_ref[...]
    cw, cb = [cw_ref[k:k + 1, :] for k in range(W)], cb_ref[...]
    ba, bx = ba_ref[...], bx_ref[...]
    sp_lam = _softplus(-lam_ref[...])
    carry = None if streaming else carry_scr[...]

    GRP = min(R, MIX_GROUP)
    per_group = GRP // L if streaming else 1
    for grp in range(R // GRP):
        r0 = grp * GRP
        rs = slice(r0, r0 + GRP)
        chunks = [slice(c0, c0 + MIX_CHUNK) for c0 in range(r0, r0 + GRP, MIX_CHUNK)]
        streams = range(grp * per_group, (grp + 1) * per_group)

        for c in chunks:
            h_scr[c, :] = _rms(x_ref[c, :], nmix)
        xr = _dot(h_scr[rs, :], win_ref[:, 2 * DA:2 * DA + DB])
        if streaming:
            y_scr[streams.start:streams.stop, HALO:HALO + L, :] = xr.reshape(per_group, L, DB)
        else:
            y_scr[0, HALO + r0:HALO + r0 + GRP, :] = xr
        z_scr[rs, DA:2 * DA] = _dot(h_scr[rs, :], win_ref[:, DA:2 * DA])

        if streaming:
            conv_in = [(y_scr[s], slice(s * L, (s + 1) * L)) for s in streams]
        else:
            conv_in = [(y_scr[0, c.start:c.stop + HALO, :], c) for c in chunks]
        for ext, c in conv_in:
            xc = _causal_conv(ext, cw, cb)
            xc_scr[c, :] = xc
            xcb_scr[c, :] = xc
        z_scr[rs, 2 * DA:2 * DA + DB] = _dot(h_scr[rs, :], win_ref[:, 2 * DA + DB:2 * DA + 2 * DB])
        for hh in range(H):
            cs = slice(hh * HD, (hh + 1) * HD)
            ra_scr[rs, cs] = _dot(xcb_scr[rs, cs], wa_ref[hh].astype(BF))
            rx_scr[rs, cs] = _dot(xcb_scr[rs, cs], wx_ref[hh].astype(BF))

        for c in chunks:
            v = _rms(_gelu(z_scr[c, DA:2 * DA]), gv)
            if streaming:
                v_ref[c, :] = v
            vb_scr[c, :] = v
        z_scr[rs, 0:DA] = _dot(h_scr[rs, :], win_ref[:, 0:DA])

        for c in chunks:
            r = _sigmoid(ra_scr[c, :] + ba)
            i = _sigmoid(rx_scr[c, :] + bx)
            a = jnp.exp(-LRU_C * r * sp_lam)
            ra_scr[c, :] = a
            rx_scr[c, :] = jnp.sqrt(1.0 - a * a) * (i * xc_scr[c, :])

        for n0 in range(r0, r0 + GRP, NCH):
            ns = slice(n0, n0 + NCH)
            for g in range(G):
                cs = slice(g * GD, (g + 1) * GD)
                sp = _dot(gwm_scr[g], vb_scr[ns, cs])
                z_scr[ns, cs] = _gelu(z_scr[ns, cs]) * (sp + gbias_ref[:, cs])
        for c in chunks:
            mix_scr[c, 0:DA] = _rms(z_scr[c, 0:DA], ga)
        x1_ref[rs, :] = x_ref[rs, :] + _dot(mix_scr[rs, 0:DA], wout_ref[0:DA, :])

        if streaming:
            for s in streams:
                c = slice(s * L, (s + 1) * L)
                h, last = _scan_chunk(ra_scr[c, :], rx_scr[c, :], h0_ref[s])
                rx_scr[c, :] = h
                hlast_ref[s] = last
        else:
            for c0 in range(r0, r0 + GRP, HALO):
                c = slice(c0, c0 + HALO)
                h, carry = _scan_chunk(ra_scr[c, :], rx_scr[c, :], carry)
                rx_scr[c, :] = h
        for c in chunks:
            out_b = rx_scr[c, :] * _gelu(z_scr[c, 2 * DA:2 * DA + DB])
            mix_scr[c, DA:DA + DB] = _rms(out_b, gb)
        x1_ref[rs, :] += _dot(mix_scr[rs, DA:DA + DB], wout_ref[DA:DA + DB, :])

    tail = slice(HALO + L - (W - 1), HALO + L)
    if streaming:
        for s in range(S):
            convnew_ref[s] = y_scr[s, tail, :]
    else:
        convnew_ref[...] = y_scr[0, tail, :]
        y_scr[0, 0:HALO, :] = y_scr[0, L:L + HALO, :]
        carry_scr[...] = carry
        hlast_ref[...] = carry


def _mixer(streaming, x, states, p):
    rows, D = x.shape
    DA, DB = p["g_v"].shape[1], p["lru_lam"].shape[1]
    weights = [p["norm_mix"], p["w_in"], p["g_v"], p["gmlp_w_eff"], p["gmlp_bias_rows"], p["lru_conv_w"],
               p["lru_conv_b"], p["lru_wa"], p["lru_ba"], p["lru_wx"], p["lru_bx"], p["lru_lam"], p["g_a"],
               p["g_b"], p["w_out"]]
    w_specs = [_const_spec(w.shape) for w in weights]
    W = p["lru_conv_w"].shape[0]
    if streaming:
        pconv, h0 = states
        N = h0.shape[0]
        L = rows // N
        S = SAMPLE_MIX_STREAMS
        R = S * L
        grid = (N // S,)
        row_spec = lambda c: pl.BlockSpec((R, c), lambda i: (i, 0))
        in_specs = [row_spec(D), pl.BlockSpec((S, HALO, DB), lambda i: (i, 0, 0)),
                    pl.BlockSpec((S, 1, DB), lambda i: (i, 0, 0))] + w_specs
        out_specs = [row_spec(D), row_spec(DA), pl.BlockSpec((S, W - 1, DB), lambda i: (i, 0, 0)),
                     pl.BlockSpec((S, 1, DB), lambda i: (i, 0, 0))]
        out_shape = [jax.ShapeDtypeStruct((rows, D), F32), jax.ShapeDtypeStruct((rows, DA), F32),
                     jax.ShapeDtypeStruct((N, W - 1, DB), F32), jax.ShapeDtypeStruct((N, 1, DB), F32)]
        args = [x, pconv, h0] + weights
    else:
        B, T = states
        S, L, R = 1, MIX_ROWS, MIX_ROWS
        nt = T // R
        grid = (B, nt)
        row_spec = lambda c: pl.BlockSpec((R, c), lambda b, t: (b * nt + t, 0))
        in_specs = [row_spec(D)] + w_specs
        out_specs = [row_spec(D), pl.BlockSpec((None, W - 1, DB), lambda b, t: (b, 0, 0)),
                     pl.BlockSpec((None, 1, DB), lambda b, t: (b, 0, 0))]
        out_shape = [jax.ShapeDtypeStruct((rows, D), F32), jax.ShapeDtypeStruct((B, W - 1, DB), F32),
                     jax.ShapeDtypeStruct((B, 1, DB), F32)]
        args = [x] + weights
    G, NCH = p["gmlp_w_eff"].shape[0], p["gmlp_w_eff"].shape[1]
    scratch = [_packed_shape(R, D),
               pltpu.VMEM((R, 2 * DA + DB), F32),
               _packed_shape(R, DA),
               pltpu.VMEM((S, L + HALO, DB), F32),
               pltpu.VMEM((R, DB), F32),
               _packed_shape(R, DB),
               pltpu.VMEM((R, DB), F32),
               pltpu.VMEM((R, DB), F32),
               _packed_shape(R, DA + DB),
               _packed_shape(NCH, NCH, (G,)),
               pltpu.VMEM((1, DB), F32)]
    return pl.pallas_call(
        functools.partial(_mixer_body, streaming, S, L),
        grid=grid, in_specs=in_specs, out_specs=out_specs, out_shape=out_shape, scratch_shapes=scratch,
        compiler_params=_params(len(grid)),
        name="mixer_sample" if streaming else "mixer_prompt",
    )(*args)


def _attend(q_scr, o_scr, s_scr, p_scr, row0, n_rows, keys, values, head_dim):
    n_heads = q_scr.shape[1] // head_dim
    scale = head_dim ** -0.5
    rows = pl.ds(row0, n_rows)
    n_chunk = min(n_rows, SOFTMAX_ROWS)
    heads = [slice(h * head_dim, (h + 1) * head_dim) for h in range(n_heads)]
    for h, cs in enumerate(heads):
        s_scr[h] = lax.dot_general(q_scr[rows, cs], keys(h), (((1,), (1,)), ((), ())),
                                   preferred_element_type=F32) * scale
    for h in range(n_heads):
        for r0 in range(0, n_rows, n_chunk):
            s = s_scr[h, r0:r0 + n_chunk, :]
            e = jnp.exp(s - jnp.max(s, axis=-1, keepdims=True))
            p_scr[h, r0:r0 + n_chunk, :] = e / jnp.sum(e, axis=-1, keepdims=True)
    for h, cs in enumerate(heads):
        o_scr[rows, cs] = _dot(p_scr[h], values(h))


def _project_q(x_ref, nxa_ref, wq_ref, h_scr, q_scr):
    g = nxa_ref[...]
    for r0 in range(0, x_ref.shape[0], Q_GROUP):
        for c0 in range(r0, r0 + Q_GROUP, MIX_CHUNK):
            c = slice(c0, c0 + MIX_CHUNK)
            h_scr[c, :] = _rms(x_ref[c, :], g)
        rs = slice(r0, r0 + Q_GROUP)
        q_scr[rs, :] = _dot(h_scr[rs, :], wq_ref[...])


def _attn_prompt_body(head_dim, x_ref, nxa_ref, wq_ref, k_ref, v_ref, wo_ref, o_ref,
                      h_scr, q_scr, a_scr, s_scr, p_scr):
    h_scr, q_scr, a_scr, p_scr, wq_ref, wo_ref = (_Packed(r) for r in (h_scr, q_scr, a_scr, p_scr, wq_ref, wo_ref))
    _project_q(x_ref, nxa_ref, wq_ref, h_scr, q_scr)
    head = lambda h: slice(h * head_dim, (h + 1) * head_dim)
    _attend(q_scr, a_scr, s_scr, p_scr, 0, x_ref.shape[0], lambda h: k_ref[:, head(h)],
            lambda h: v_ref[:, head(h)], head_dim)
    o_ref[...] = x_ref[...] + _dot(a_scr[...], wo_ref[...])


def _attn_sample_body(head_dim, n_streams, n_rows, x_ref, nxa_ref, wq_ref, k_hbm, v_hbm, wo_ref, o_ref,
                      h_scr, q_scr, a_scr, s_scr, p_scr, kv_buf, kv_sem):
    h_scr, q_scr, a_scr, p_scr, wq_ref, wo_ref = (_Packed(r) for r in (h_scr, q_scr, a_scr, p_scr, wq_ref, wo_ref))
    i = pl.program_id(0)
    n_steps = pl.num_programs(0)
    n_heads = q_scr.shape[1] // head_dim
    slot = lax.rem(i, 2)

    def copies(step, dst_slot):
        out = []
        for which, hbm in enumerate((k_hbm, v_hbm)):
            for s in range(n_streams):
                for h in range(n_heads):
                    out.append(pltpu.make_async_copy(hbm.at[step * n_streams + s, :, h, :],
                                                     kv_buf.at[dst_slot, which, s, h],
                                                     kv_sem.at[dst_slot, which, s, h]))
        return out

    @pl.when(i == 0)
    def _():
        for cp in copies(0, 0):
            cp.start()

    @pl.when(i + 1 < n_steps)
    def _():
        for cp in copies(i + 1, 1 - slot):
            cp.start()

    @pl.when(i == 0)
    def _():
        _project_q(x_ref, nxa_ref, wq_ref, h_scr, q_scr)

    for cp in copies(i, slot):
        cp.wait()

    for s in range(n_streams):
        row0 = pl.multiple_of((i * n_streams + s) * n_rows, n_rows)
        _attend(q_scr, a_scr, s_scr.at[s], p_scr.at[s], row0, n_rows,
                lambda h: kv_buf[slot, 0, s, h].astype(BF), lambda h: kv_buf[slot, 1, s, h].astype(BF), head_dim)

    @pl.when(i == n_steps - 1)
    def _():
        o_ref[...] = x_ref[...] + _dot(a_scr[...], wo_ref[...])


def _attn_prompt(x, k, v, p, n_heads, batch):
    rows, D = x.shape
    M = k.shape[0] // batch
    R = ATT_ROWS
    nt = rows // batch // R
    row_spec = pl.BlockSpec((R, D), lambda b, t: (b * nt + t, 0))
    mem_spec = pl.BlockSpec((M, D), lambda b, t: (b, 0))
    return pl.pallas_call(
        functools.partial(_attn_prompt_body, D // n_heads),
        grid=(batch, nt),
        in_specs=[row_spec, _const_spec((1, D)), _const_spec(p["w_q"].shape), mem_spec, mem_spec,
                  _const_spec(p["w_o"].shape)],
        out_specs=row_spec,
        out_shape=jax.ShapeDtypeStruct((rows, D), F32),
        scratch_shapes=[_packed_shape(R, D), _packed_shape(R, D), _packed_shape(R, D),
                        pltpu.VMEM((n_heads, R, M), F32), _packed_shape(R, M, (n_heads,))],
        compiler_params=_params(2),
        name="attn_prompt",
    )(x, p["norm_xa"], p["w_q"], k, v, p["w_o"])


def _attn_sample(x, k, v, p):
    rows, D = x.shape
    N, M, n_heads, head_dim = k.shape
    L = rows // N
    S = SAMPLE_ATT_STREAMS
    hbm = pl.BlockSpec(memory_space=pl.ANY)
    return pl.pallas_call(
        functools.partial(_attn_sample_body, head_dim, S, L),
        grid=(N // S,),
        in_specs=[_const_spec((rows, D)), _const_spec((1, D)), _const_spec(p["w_q"].shape), hbm, hbm,
                  _const_spec(p["w_o"].shape)],
        out_specs=pl.BlockSpec((rows, D), lambda i: (0, 0)),
        out_shape=jax.ShapeDtypeStruct((rows, D), F32),
        scratch_shapes=[_packed_shape(rows, D), _packed_shape(rows, D), _packed_shape(rows, D),
                        pltpu.VMEM((S, n_heads, L, M), F32), _packed_shape(L, M, (S, n_heads)),
                        pltpu.VMEM((2, 2, S, n_heads, M, head_dim), F32),
                        pltpu.SemaphoreType.DMA((2, 2, S, n_heads))],
        compiler_params=_params(1),
        name="attn_sample",
    )(x, p["norm_xa"], p["w_q"], k, v, p["w_o"])


def _ffn_body(streaming, n_streams, n_rows, *refs):
    S, L = n_streams, n_rows
    R = S * L
    refs = list(refs)
    x_ref = refs.pop(0)
    if streaming:
        st_ref = refs.pop(0)
    (nffn_ref, wua_ref, wug_ref, cw_ref, cb_ref, wd_ref, nfin_ref) = refs[:7]
    refs = refs[7:]
    y_ref, new_ref = refs.pop(0), refs.pop(0)
    if streaming:
        casts = tuple((src, _Packed(refs.pop(0))) for src in (wua_ref, wug_ref, wd_ref))
    else:
        wua_ref, wug_ref, wd_ref = _Packed(wua_ref), _Packed(wug_ref), _Packed(wd_ref)
    h_scr, acc_scr, ya_scr, yg_scr, act_scr, carry_scr = refs
    h_scr, act_scr = _Packed(h_scr), _Packed(act_scr)
    W = cw_ref.shape[0]
    DFF = cw_ref.shape[1] // 2
    C = wua_ref.shape[1]

    if streaming:
        j = pl.program_id(1)
    else:
        t, j = pl.program_id(1), pl.program_id(2)
    nj = pl.num_programs(2 - int(streaming))

    @pl.when(j == 0)
    def _():
        g = nffn_ref[...]
        def norm(r0):
            rows = pl.ds(r0, ROWS)
            h_scr[rows, :] = _rms(x_ref[rows, :], g)
            acc_scr[rows, :] = jnp.zeros((ROWS, acc_scr.shape[1]), F32)
        _rows_loop(R, ROWS, norm, unroll=2 * LOOP_UNROLL)
        if not streaming:
            @pl.when(t == 0)
            def _():
                carry_scr[...] = jnp.zeros(carry_scr.shape, F32)

    if streaming:
        for src, dst in casts:
            dst[...] = src[...]
        (_, wua_ref), (_, wug_ref), (_, wd_ref) = casts

    tail = slice(HALO + L - (W - 1), HALO + L)
    n_chunk = min(L, FFN_CHUNK)
    for c in range(C // FFN_SUB):
        cs = slice(c * FFN_SUB, (c + 1) * FFN_SUB)
        for half, (y_scr, w_ref) in enumerate(((ya_scr, wua_ref), (yg_scr, wug_ref))):
            y_scr[:, HALO:HALO + L, cs] = _dot(h_scr[...], w_ref[:, cs]).reshape(S, L, FFN_SUB)
            if streaming:
                for s in range(S):
                    y_scr[s, HALO - (W - 1):HALO, cs] = st_ref[s, half, :, cs]
                    new_ref[s, half, :, cs] = y_scr[s, tail, cs]
            else:
                y_scr[0, 0:HALO, cs] = carry_scr[j, half, :, cs]
                carry_scr[j, half, :, cs] = y_scr[0, L:L + HALO, cs]
                out_cols = pl.ds(pl.multiple_of(j * C + c * FFN_SUB, FFN_SUB), FFN_SUB)
                new_ref[half, :, out_cols] = y_scr[0, tail, cs]

        lin_cols = pl.ds(pl.multiple_of(j * C + c * FFN_SUB, FFN_SUB), FFN_SUB)
        gated_cols = pl.ds(pl.multiple_of(DFF + j * C + c * FFN_SUB, FFN_SUB), FFN_SUB)
        taps_a = [jnp.broadcast_to(cw_ref[k:k + 1, lin_cols], (n_chunk, FFN_SUB)) for k in range(W)]
        taps_g = [jnp.broadcast_to(cw_ref[k:k + 1, gated_cols], (n_chunk, FFN_SUB)) for k in range(W)]
        bias_a = jnp.broadcast_to(cb_ref[:, lin_cols], (n_chunk, FFN_SUB))
        bias_g = jnp.broadcast_to(cb_ref[:, gated_cols], (n_chunk, FFN_SUB))
        for s in range(S):
            for r0 in range(0, L, n_chunk):
                ext = slice(r0, r0 + n_chunk + HALO)
                lin = _causal_conv(ya_scr[s, ext, cs], taps_a, bias_a)
                gated = _causal_conv(yg_scr[s, ext, cs], taps_g, bias_g)
                out0 = s * L + r0
                act_scr[out0:out0 + n_chunk, cs] = _gelu(gated) * lin

    for ks in (slice(0, C - FFN_SUB), slice(C - FFN_SUB, C)) if C > FFN_SUB else (slice(0, C),):
        acc_scr[...] += _dot(act_scr[:, ks], wd_ref[ks, :])

    @pl.when(j == nj - 1)
    def _():
        g = nfin_ref[...]
        def final(r0):
            rows = pl.ds(r0, ROWS)
            y_ref[rows, :] = _rms(x_ref[rows, :] + acc_scr[rows, :], g)
        _rows_loop(R, ROWS, final, unroll=2 * LOOP_UNROLL)


def _ffn(streaming, x, state, p, w_up_lin, w_up_gated, w_down):
    rows, D = x.shape
    DFF = w_down.shape[0] if streaming else 2 * w_down.shape[0]
    W = p["ffn_conv_w"].shape[0]
    C = FFN_SAMPLE_COLS if streaming else FFN_COLS
    nj = DFF // C
    (wua, off_a), (wug, off_g) = w_up_lin, w_up_gated
    if streaming:
        N = state.shape[0]
        L = rows // N
        R = FFN_ROWS
        S = R // L
        grid = (rows // R, nj)
        row_map = lambda i, j: (i, 0)
        colmap = lambda off: (lambda i, j: (0, j + off))
        wd_map = lambda i, j: (j, 0)
        new_spec = pl.BlockSpec((S, 2, W - 1, C), lambda i, j: (i, 0, 0, j))
        new_shape = jax.ShapeDtypeStruct((N, 2, W - 1, DFF), F32)
        in_specs = [pl.BlockSpec((R, D), row_map),
                    pl.BlockSpec((S, 2, W - 1, C), lambda i, j: (i, 0, 0, j))]
        args = [x, state]
    else:
        B, T = state
        S, L, R = 1, FFN_ROWS, FFN_ROWS
        nt = T // R
        grid = (B, nt, nj)
        row_map = lambda b, t, j: (b * nt + t, 0)
        colmap = lambda off: (lambda b, t, j: (0, j + off))
        wd_map = lambda b, t, j: (j, 0)
        new_spec = pl.BlockSpec((None, None, 2, W - 1, DFF), lambda b, t, j: (b, t, 0, 0, 0))
        new_shape = jax.ShapeDtypeStruct((B, nt, 2, W - 1, DFF), F32)
        in_specs = [pl.BlockSpec((R, D), row_map)]
        args = [x]
    in_specs += [_const_spec((1, D)),
                 pl.BlockSpec((wua.shape[0], C), colmap(off_a // C)), pl.BlockSpec((wug.shape[0], C), colmap(off_g // C)),
                 _const_spec((W, 2 * DFF)), _const_spec((1, 2 * DFF)),
                 pl.BlockSpec((C if streaming else C // 2, D), wd_map), _const_spec((1, D))]
    args += [p["norm_ffn"], wua, wug, p["ffn_conv_w"], p["ffn_conv_b"], w_down, p["norm_final"]]
    out_specs = [pl.BlockSpec((R, D), row_map), new_spec]
    out_shape = [jax.ShapeDtypeStruct((rows, D), F32), new_shape]
    if streaming:
        out_specs += [pl.BlockSpec((D // 2, C), colmap(0)), pl.BlockSpec((D // 2, C), colmap(0)),
                      pl.BlockSpec((C // 2, D), wd_map)]
        out_shape += [jax.ShapeDtypeStruct((D // 2, DFF), U32), jax.ShapeDtypeStruct((D // 2, DFF), U32),
                      jax.ShapeDtypeStruct((DFF // 2, D), U32)]
    scratch = [_packed_shape(R, D),
               pltpu.VMEM((R, D), F32),
               pltpu.VMEM((S, L + HALO, C), F32),
               pltpu.VMEM((S, L + HALO, C), F32),
               _packed_shape(R, C),
               pltpu.VMEM((nj, 2, HALO, C), F32)]
    return pl.pallas_call(
        functools.partial(_ffn_body, streaming, S, L),
        grid=grid, in_specs=in_specs, out_specs=out_specs, out_shape=out_shape,
        scratch_shapes=scratch,
        compiler_params=_params(len(grid)),
        name="ffn_sample" if streaming else "ffn_prompt",
    )(*args)


def kernel(x_prompt, x_sample, mem_prompt, cache_mem_k, cache_mem_v, state_lru_h, state_lru_conv, state_ffn_conv, norm_mix, w_in, g_v, gmlp_w, gmlp_b, lru_conv_w, lru_conv_b, lru_wa, lru_ba, lru_wx, lru_bx, lru_lam, g_a, g_b, w_out, norm_mem, w_kv, norm_xa, w_q, w_o, norm_ffn, w_up, ffn_conv_w, ffn_conv_b, w_down, norm_final):
    depth = w_in.shape[0]
    assert depth == 1, "single-layer trunk"
    B, T, D = x_prompt.shape
    N, L, _ = x_sample.shape
    M = mem_prompt.shape[1]
    n_heads, head_dim = cache_mem_k.shape[3], cache_mem_k.shape[4]
    G, NCH = gmlp_w.shape[1], gmlp_w.shape[2]
    DA, DB = g_v.shape[1], lru_lam.shape[1]
    GD = DA // G
    DFF = w_down.shape[1]
    W_LRU, W_FFN = lru_conv_w.shape[1], ffn_conv_w.shape[1]
    assert L <= CAUSAL_CHUNK and NCH % L == 0 and L == ROWS
    row = lambda a: a.reshape(1, -1)

    shared = {
        "norm_mix": row(norm_mix[0]), "g_v": row(g_v[0]),
        "lru_conv_w": lru_conv_w[0], "lru_conv_b": row(lru_conv_b[0]),
        "lru_wa": lru_wa[0], "lru_ba": row(lru_ba[0]),
        "lru_wx": lru_wx[0], "lru_bx": row(lru_bx[0]), "lru_lam": row(lru_lam[0]),
        "g_a": row(g_a[0]), "g_b": row(g_b[0]),
        "norm_xa": row(norm_xa[0]),
        "norm_ffn": row(norm_ffn[0]), "ffn_conv_w": ffn_conv_w[0],
        "ffn_conv_b": row(ffn_conv_b[0]), "norm_final": row(norm_final),
    }
    (mk, mv, mk_b, mv_b), resident = _memory_kv(mem_prompt.reshape(B * M, D), row(norm_mem[0]), w_kv[0],
                                                [w_in[0], w_out[0], w_q[0], w_o[0]], head_dim)
    shared.update(zip(("w_in", "w_out", "w_q", "w_o"), resident))
    prompt_p = dict(shared, gmlp_w_eff=gmlp_w[0],
                    gmlp_bias_rows=jnp.repeat(gmlp_b[0].T, GD, axis=1))
    reps = NCH // L
    sample_p = dict(shared, gmlp_w_eff=jnp.tile(gmlp_w[0][:, :L, :L], (1, reps, reps)),
                    gmlp_bias_rows=jnp.repeat(jnp.tile(gmlp_b[0][:, :L].T, (reps, 1)), GD, axis=1))

    xs = x_sample.reshape(N * L, D)
    pconv = jnp.pad(state_lru_conv[0], ((0, 0), (HALO - (W_LRU - 1), 0), (0, 0)))
    xs, v_s, conv_s, h_s = _mixer(True, xs, (pconv, state_lru_h[0].reshape(N, 1, DB)), sample_p)
    xs = _attn_sample(xs, cache_mem_k[0], cache_mem_v[0], sample_p)
    ffn_prev = jnp.swapaxes(state_ffn_conv[0].reshape(N, W_FFN - 1, 2, DFF), 1, 2)
    y_s, ffn_s, w_up_lin, w_up_gated, w_down_b = _ffn(True, xs, ffn_prev, sample_p,
                                                      (w_up[0], 0), (w_up[0], DFF), w_down[0])

    xp = x_prompt.reshape(B * T, D)
    xp, conv_p, h_p = _mixer(False, xp, (B, T), prompt_p)
    xp = _attn_prompt(xp, mk_b, mv_b, prompt_p, n_heads, B)
    y_p, ffn_p = _ffn(False, xp, (B, T), prompt_p, (w_up_lin, 0), (w_up_gated, 0), w_down_b)

    def ffn_state(a):
        return jnp.swapaxes(a, 1, 2).reshape(1, a.shape[0], W_FFN - 1, 2 * DFF)

    return (y_p.reshape(B, T, D), y_s.reshape(N, L, D),
            mk.reshape(1, B, M, n_heads, head_dim), mv.reshape(1, B, M, n_heads, head_dim),
            h_p.reshape(1, B, DB), conv_p.reshape(1, B, W_LRU - 1, DB), ffn_state(ffn_p[:, -1]),
            h_s.reshape(1, N, DB), conv_s.reshape(1, N, W_LRU - 1, DB), ffn_state(ffn_s),
            v_s.reshape(1, N, L, DA))
```

```python
import functools

import jax
import jax.numpy as jnp
from jax import lax
from jax.experimental import pallas as pl
from jax.experimental.pallas import tpu as pltpu

F32 = jnp.float32
BF = jnp.bfloat16
U32 = jnp.uint32

EPS = 1e-6
LRU_C = 8.0
CAUSAL_CHUNK = 64
HALO = 8
ROWS = 16
LOOP_UNROLL = 4
SOFTMAX_ROWS = 64
Q_GROUP = 128
MIX_CHUNK = 32
MIX_ORDER = ("norm", "xr", "conv", "maps", "gate", "v", "vpath", "u", "gates", "spatial", "norm_a", "out_a", "scan",
             "norm_b", "out_b")
V7X_VMEM_BYTES = 64 * 1024 * 1024
VMEM_LIMIT = V7X_VMEM_BYTES - 8 * 1024 * 1024

MIX_ROWS = 256
ATT_ROWS = 512
FFN_ROWS = 512
FFN_COLS = 1024
FFN_SAMPLE_COLS = 256
FFN_SUB = 256
FFN_CHUNK = 64
KV_COLS = 256
SAMPLE_MIX_STREAMS = 16
SAMPLE_ATT_STREAMS = 2


class _Packed:
    def __init__(self, ref):
        self.ref = ref

    @property
    def shape(self):
        return self.ref.shape[:-2] + (2 * self.ref.shape[-2], self.ref.shape[-1])

    @property
    def at(self):
        packed = self

        class _At:
            def __getitem__(self, idx):
                return _Packed(packed.ref.at[idx])
        return _At()

    def _index(self, idx):
        nd = len(self.ref.shape)
        idx = idx if isinstance(idx, tuple) else (idx,)
        if Ellipsis in idx:
            k = idx.index(Ellipsis)
            idx = idx[:k] + (slice(None),) * (nd - len(idx) + 1) + idx[k + 1:]
        idx = list(idx) + [slice(None)] * (nd - len(idx))
        rows = idx[nd - 2]
        if isinstance(rows, slice):
            half = lambda v: None if v is None else v // 2
            assert rows.step is None and (rows.start or 0) % ROWS == 0 and (rows.stop is None or rows.stop % ROWS == 0)
            idx[nd - 2] = slice(half(rows.start), half(rows.stop))
        else:
            assert rows.size % ROWS == 0
            start = rows.start // 2 if isinstance(rows.start, int) else pl.multiple_of(rows.start // 2, ROWS // 2)
            idx[nd - 2] = pl.ds(start, rows.size // 2)
        return tuple(idx)

    def __getitem__(self, idx):
        return pltpu.bitcast(self.ref[self._index(idx)], BF)

    def __setitem__(self, idx, value):
        self.ref[self._index(idx)] = pltpu.bitcast(value.astype(BF), U32)


def _packed_shape(rows, cols, lead=()):
    return pltpu.VMEM(tuple(lead) + (rows // 2, cols), U32)


def _rms(x, g):
    return x * lax.rsqrt(jnp.mean(x * x, axis=-1, keepdims=True) + EPS) * g


def _gelu(x):
    return x * (0.5 * (1.0 + jnp.tanh(0.7978845608028654 * (x + 0.044715 * (x * x * x)))))


def _sigmoid(x):
    return 0.5 + 0.5 * jnp.tanh(0.5 * x)


def _softplus(x):
    return jnp.maximum(x, 0.0) + jnp.log1p(jnp.exp(-jnp.abs(x)))


def _dot(a, b):
    return jnp.dot(a, b, preferred_element_type=F32)


def _rows_loop(n_rows, chunk, fn, unroll=LOOP_UNROLL):
    def body(i, carry):
        fn(pl.multiple_of(i * chunk, chunk))
        return carry
    trips = n_rows // chunk
    lax.fori_loop(0, trips, body, 0, unroll=min(unroll, trips))


def _causal_conv(ext, taps, bias):
    width = len(taps)
    acc = None
    for k in range(width):
        shift = width - 1 - k
        src = ext if shift == 0 else pltpu.roll(ext, shift, axis=0)
        term = src[HALO:, :] * taps[k]
        acc = term if acc is None else acc + term
    return acc + bias


def _const_spec(shape):
    nd = len(shape)
    return pl.BlockSpec(shape, lambda *_: (0,) * nd, pipeline_mode=pl.Buffered(1))


def _params(n_grid):
    return pltpu.CompilerParams(dimension_semantics=("arbitrary",) * n_grid, vmem_limit_bytes=VMEM_LIMIT)


def _kv_body(n_casts, head_dim, mem_ref, g_ref, wk_ref, wv_ref, *refs):
    cast_in, refs = refs[:n_casts], refs[n_casts:]
    k_hbm, v_hbm, kb_ref, vb_ref = refs[:4]
    cast_out = refs[4:4 + n_casts]
    h_scr, stage, sem = refs[4 + n_casts:]
    h_scr = _Packed(h_scr)
    cast_out = [_Packed(r) for r in cast_out]
    j, n_steps = pl.program_id(0), pl.num_programs(0)
    slot = lax.rem(j, 2)
    cols = wk_ref.shape[1]
    per_head = head_dim // cols

    def stores(step, slot):
        head = lax.div(step, per_head)
        c0 = pl.multiple_of(lax.rem(step, per_head) * cols, cols)
        return [pltpu.make_async_copy(stage.at[slot, which], hbm.at[:, head, pl.ds(c0, cols)], sem.at[slot, which])
                for which, hbm in enumerate((k_hbm, v_hbm))]

    @pl.when(j == 0)
    def _():
        g = g_ref[...]
        def norm(r0):
            rows = pl.ds(r0, ROWS)
            h_scr[rows, :] = _rms(mem_ref[rows, :], g)
        _rows_loop(mem_ref.shape[0], ROWS, norm)

    @pl.when(j >= 2)
    def _():
        for cp in stores(j - 2, slot):
            cp.wait()

    k = _dot(h_scr[...], wk_ref[...].astype(BF))
    stage[slot, 0] = k
    kb_ref[...] = k.astype(BF)
    v = _dot(h_scr[...], wv_ref[...].astype(BF))
    stage[slot, 1] = v
    vb_ref[...] = v.astype(BF)
    for cp in stores(j, slot):
        cp.start()
    for src, dst in zip(cast_in, cast_out):
        dst[...] = src[...]

    @pl.when(j == n_steps - 1)
    def _():
        for cp in stores(j - 1, 1 - slot) + stores(j, slot):
            cp.wait()


def _memory_kv(mem, g, w_kv, resident, head_dim):
    m, d = mem.shape
    n_steps = d // KV_COLS
    assert n_steps >= 2 and head_dim % KV_COLS == 0
    col = pl.BlockSpec((m, KV_COLS), lambda j: (0, j))
    hbm = pl.BlockSpec(memory_space=pl.ANY)
    cast_specs = [pl.BlockSpec((w.shape[0] // n_steps, w.shape[1]), lambda j: (j, 0)) for w in resident]
    packed_specs = [pl.BlockSpec((w.shape[0] // n_steps // 2, w.shape[1]), lambda j: (j, 0)) for w in resident]
    kv_shape = jax.ShapeDtypeStruct((m, d // head_dim, head_dim), F32)
    outs = pl.pallas_call(
        functools.partial(_kv_body, len(resident), head_dim),
        grid=(n_steps,),
        in_specs=[_const_spec((m, d)), _const_spec((1, d)),
                  pl.BlockSpec((d, KV_COLS), lambda j: (0, j)),
                  pl.BlockSpec((d, KV_COLS), lambda j: (0, j + n_steps))] + cast_specs,
        out_specs=[hbm, hbm, col, col] + packed_specs,
        out_shape=[kv_shape, kv_shape, jax.ShapeDtypeStruct((m, d), BF), jax.ShapeDtypeStruct((m, d), BF)]
                  + [jax.ShapeDtypeStruct((w.shape[0] // 2, w.shape[1]), U32) for w in resident],
        scratch_shapes=[_packed_shape(m, d), pltpu.VMEM((2, 2, m, KV_COLS), F32),
                        pltpu.SemaphoreType.DMA((2, 2))],
        compiler_params=_params(1),
        name="memory_kv",
    )(mem, g, w_kv, w_kv, *resident)
    return outs[:4], outs[4:]


def _scan_chunk(a, b, carry):
    n = a.shape[0]
    pos = lax.broadcasted_iota(jnp.int32, a.shape, 0)
    d = 1
    while d < n:
        keep = pos >= d
        a_prev = jnp.where(keep, pltpu.roll(a, d, axis=0), 1.0)
        b_prev = jnp.where(keep, pltpu.roll(b, d, axis=0), 0.0)
        b = a * b_prev + b
        a = a * a_prev
        d *= 2
    h = b + a * carry
    return h, h[n - 1:n, :]


def _mixer_body(streaming, n_streams, n_rows, *refs):
    S, L = n_streams, n_rows
    R = S * L
    refs = list(refs)
    x_ref = refs.pop(0)
    if streaming:
        pconv_ref, h0_ref = refs.pop(0), refs.pop(0)
    (nmix_ref, win_ref, gv_ref, gw_ref, gbias_ref, cw_ref, cb_ref, wa_ref, ba_ref, wx_ref, bx_ref,
     lam_ref, ga_ref, gb_ref, wout_ref) = refs[:15]
    refs = refs[15:]
    x1_ref = refs.pop(0)
    if streaming:
        v_ref = refs.pop(0)
    convnew_ref, hlast_ref = refs.pop(0), refs.pop(0)
    h_scr, z_scr, vb_scr, y_scr, xc_scr, xcb_scr, ra_scr, rx_scr, mix_scr, gwm_scr, carry_scr = refs
    h_scr, vb_scr, xcb_scr, mix_scr, gwm_scr = (_Packed(r) for r in (h_scr, vb_scr, xcb_scr, mix_scr, gwm_scr))
    win_ref, wout_ref = _Packed(win_ref), _Packed(wout_ref)

    DA = gv_ref.shape[1]
    DB = lam_ref.shape[1]
    G, NCH = gw_ref.shape[0], gw_ref.shape[1]
    GD = DA // G
    H, HD = wa_ref.shape[0], wa_ref.shape[1]
    W = cw_ref.shape[0]

    if streaming:
        first = pl.program_id(0) == 0
    else:
        t = pl.program_id(1)
        first = jnp.logical_and(pl.program_id(0) == 0, t == 0)

    @pl.when(first)
    def _():
        ri = lax.broadcasted_iota(jnp.int32, (NCH, NCH), 0)
        ci = lax.broadcasted_iota(jnp.int32, (NCH, NCH), 1)
        if streaming:
            sh = L.bit_length() - 1
            keep = lax.shift_right_logical(ri, sh) == lax.shift_right_logical(ci, sh)
        else:
            sh = CAUSAL_CHUNK.bit_length() - 1
            keep = lax.shift_right_logical(ri, sh) >= lax.shift_right_logical(ci, sh)
        for g in range(G):
            gwm_scr[g] = jnp.where(keep, gw_ref[g], 0.0)

    if streaming:
        y_scr[:, 0:HALO, :] = pconv_ref[...]
    else:
        @pl.when(t == 0)
        def _():
            y_scr[0, 0:HALO, :] = jnp.zeros((HALO, DB), F32)
            carry_scr[...] = jnp.zeros((1, DB), F32)

    nmix, gv, ga, gb = nmix_ref[...], gv_ref[...], ga_ref[...], gb_ref[...]
    cw, cb = [cw_ref[k:k + 1, :] for k in range(W)], cb_ref[...]
    ba, bx = ba_ref[...], bx_ref[...]
    sp_lam = _softplus(-lam_ref[...])
    chunks = [slice(c0, c0 + MIX_CHUNK) for c0 in range(0, R, MIX_CHUNK)]
    if streaming:
        conv_in = [(lambda s=s: y_scr[s], slice(s * L, (s + 1) * L)) for s in range(S)]
    else:
        conv_in = [(lambda c=c: y_scr[0, c.start:c.stop + HALO, :], c) for c in chunks]
    state = {"carry": None if streaming else carry_scr[...], "first_out": True}

    def p_norm():
        for c in chunks:
            h_scr[c, :] = _rms(x_ref[c, :], nmix)

    def p_xr():
        xr = _dot(h_scr[...], win_ref[:, 2 * DA:2 * DA + DB])
        y_scr[:, HALO:HALO + L, :] = xr.reshape(S, L, DB)

    def p_v():
        z_scr[:, DA:2 * DA] = _dot(h_scr[...], win_ref[:, DA:2 * DA])

    def p_gate():
        z_scr[:, 2 * DA:2 * DA + DB] = _dot(h_scr[...], win_ref[:, 2 * DA + DB:2 * DA + 2 * DB])

    def p_u():
        z_scr[:, 0:DA] = _dot(h_scr[...], win_ref[:, 0:DA])

    def p_conv():
        for ext, c in conv_in:
            xc = _causal_conv(ext(), cw, cb)
            xc_scr[c, :] = xc
            xcb_scr[c, :] = xc

    def p_maps():
        for hh in range(H):
            cs = slice(hh * HD, (hh + 1) * HD)
            ra_scr[:, cs] = _dot(xcb_scr[:, cs], wa_ref[hh].astype(BF))
            rx_scr[:, cs] = _dot(xcb_scr[:, cs], wx_ref[hh].astype(BF))

    def p_vpath():
        for c in chunks:
            v = _rms(_gelu(z_scr[c, DA:2 * DA]), gv)
            if streaming:
                v_ref[c, :] = v
            vb_scr[c, :] = v

    def p_gates():
        for c in chunks:
            r = _sigmoid(ra_scr[c, :] + ba)
            i = _sigmoid(rx_scr[c, :] + bx)
            a = jnp.exp(-LRU_C * r * sp_lam)
            ra_scr[c, :] = a
            rx_scr[c, :] = jnp.sqrt(1.0 - a * a) * (i * xc_scr[c, :])

    def p_spatial():
        for n0 in range(0, R, NCH):
            ns = slice(n0, n0 + NCH)
            for g in range(G):
                cs = slice(g * GD, (g + 1) * GD)
                sp = _dot(gwm_scr[g], vb_scr[ns, cs])
                z_scr[ns, cs] = _gelu(z_scr[ns, cs]) * (sp + gbias_ref[:, cs])

    def p_norm_a():
        for c in chunks:
            mix_scr[c, 0:DA] = _rms(z_scr[c, 0:DA], ga)

    def p_scan():
        if streaming:
            for s in range(S):
                c = slice(s * L, (s + 1) * L)
                h, last = _scan_chunk(ra_scr[c, :], rx_scr[c, :], h0_ref[s])
                rx_scr[c, :] = h
                hlast_ref[s] = last
        else:
            carry = state["carry"]
            for c0 in range(0, R, HALO):
                c = slice(c0, c0 + HALO)
                h, carry = _scan_chunk(ra_scr[c, :], rx_scr[c, :], carry)
                rx_scr[c, :] = h
            state["carry"] = carry

    def p_norm_b():
        for c in chunks:
            out_b = rx_scr[c, :] * _gelu(z_scr[c, 2 * DA:2 * DA + DB])
            mix_scr[c, DA:DA + DB] = _rms(out_b, gb)

    def p_out(cols):
        upd = _dot(mix_scr[:, cols], wout_ref[cols, :])
        if state["first_out"]:
            x1_ref[...] = x_ref[...] + upd
            state["first_out"] = False
        else:
            x1_ref[...] += upd

    pieces = {"norm": p_norm, "xr": p_xr, "v": p_v, "gate": p_gate, "u": p_u, "conv": p_conv, "maps": p_maps,
              "vpath": p_vpath, "gates": p_gates, "spatial": p_spatial, "norm_a": p_norm_a, "scan": p_scan,
              "norm_b": p_norm_b, "out_a": lambda: p_out(slice(0, DA)), "out_b": lambda: p_out(slice(DA, DA + DB))}
    assert sorted(MIX_ORDER) == sorted(pieces)
    for name in MIX_ORDER:
        pieces[name]()

    tail = slice(HALO + L - (W - 1), HALO + L)
    if streaming:
        for s in range(S):
            convnew_ref[s] = y_scr[s, tail, :]
    else:
        convnew_ref[...] = y_scr[0, tail, :]
        y_scr[0, 0:HALO, :] = y_scr[0, L:L + HALO, :]
        carry_scr[...] = state["carry"]
        hlast_ref[...] = state["carry"]


def _mixer(streaming, x, states, p):
    rows, D = x.shape
    DA, DB = p["g_v"].shape[1], p["lru_lam"].shape[1]
    weights = [p["norm_mix"], p["w_in"], p["g_v"], p["gmlp_w_eff"], p["gmlp_bias_rows"], p["lru_conv_w"],
               p["lru_conv_b"], p["lru_wa"], p["lru_ba"], p["lru_wx"], p["lru_bx"], p["lru_lam"], p["g_a"],
               p["g_b"], p["w_out"]]
    w_specs = [_const_spec(w.shape) for w in weights]
    W = p["lru_conv_w"].shape[0]
    if streaming:
        pconv, h0 = states
        N = h0.shape[0]
        L = rows // N
        S = SAMPLE_MIX_STREAMS
        R = S * L
        grid = (N // S,)
        row_spec = lambda c: pl.BlockSpec((R, c), lambda i: (i, 0))
        in_specs = [row_spec(D), pl.BlockSpec((S, HALO, DB), lambda i: (i, 0, 0)),
                    pl.BlockSpec((S, 1, DB), lambda i: (i, 0, 0))] + w_specs
        out_specs = [row_spec(D), row_spec(DA), pl.BlockSpec((S, W - 1, DB), lambda i: (i, 0, 0)),
                     pl.BlockSpec((S, 1, DB), lambda i: (i, 0, 0))]
        out_shape = [jax.ShapeDtypeStruct((rows, D), F32), jax.ShapeDtypeStruct((rows, DA), F32),
                     jax.ShapeDtypeStruct((N, W - 1, DB), F32), jax.ShapeDtypeStruct((N, 1, DB), F32)]
        args = [x, pconv, h0] + weights
    else:
        B, T = states
        S, L, R = 1, MIX_ROWS, MIX_ROWS
        nt = T // R
        grid = (B, nt)
        row_spec = lambda c: pl.BlockSpec((R, c), lambda b, t: (b * nt + t, 0))
        in_specs = [row_spec(D)] + w_specs
        out_specs = [row_spec(D), pl.BlockSpec((None, W - 1, DB), lambda b, t: (b, 0, 0)),
                     pl.BlockSpec((None, 1, DB), lambda b, t: (b, 0, 0))]
        out_shape = [jax.ShapeDtypeStruct((rows, D), F32), jax.ShapeDtypeStruct((B, W - 1, DB), F32),
                     jax.ShapeDtypeStruct((B, 1, DB), F32)]
        args = [x] + weights
    G, NCH = p["gmlp_w_eff"].shape[0], p["gmlp_w_eff"].shape[1]
    scratch = [_packed_shape(R, D),
               pltpu.VMEM((R, 2 * DA + DB), F32),
               _packed_shape(R, DA),
               pltpu.VMEM((S, L + HALO, DB), F32),
               pltpu.VMEM((R, DB), F32),
               _packed_shape(R, DB),
               pltpu.VMEM((R, DB), F32),
               pltpu.VMEM((R, DB), F32),
               _packed_shape(R, DA + DB),
               _packed_shape(NCH, NCH, (G,)),
               pltpu.VMEM((1, DB), F32)]
    return pl.pallas_call(
        functools.partial(_mixer_body, streaming, S, L),
        grid=grid, in_specs=in_specs, out_specs=out_specs, out_shape=out_shape, scratch_shapes=scratch,
        compiler_params=_params(len(grid)),
        name="mixer_sample" if streaming else "mixer_prompt",
    )(*args)


def _attend(q_scr, o_scr, s_scr, p_scr, row0, n_rows, keys, values, head_dim):
    n_heads = q_scr.shape[1] // head_dim
    scale = head_dim ** -0.5
    rows = pl.ds(row0, n_rows)
    n_chunk = min(n_rows, SOFTMAX_ROWS)
    heads = [slice(h * head_dim, (h + 1) * head_dim) for h in range(n_heads)]
    for h, cs in enumerate(heads):
        s_scr[h] = lax.dot_general(q_scr[rows, cs], keys(h), (((1,), (1,)), ((), ())),
                                   preferred_element_type=F32) * scale
    for h in range(n_heads):
        for r0 in range(0, n_rows, n_chunk):
            s = s_scr[h, r0:r0 + n_chunk, :]
            e = jnp.exp(s - jnp.max(s, axis=-1, keepdims=True))
            p_scr[h, r0:r0 + n_chunk, :] = e / jnp.sum(e, axis=-1, keepdims=True)
    for h, cs in enumerate(heads):
        o_scr[rows, cs] = _dot(p_scr[h], values(h))


def _project_q(x_ref, nxa_ref, wq_ref, h_scr, q_scr):
    g = nxa_ref[...]
    for r0 in range(0, x_ref.shape[0], Q_GROUP):
        for c0 in range(r0, r0 + Q_GROUP, MIX_CHUNK):
            c = slice(c0, c0 + MIX_CHUNK)
            h_scr[c, :] = _rms(x_ref[c, :], g)
        rs = slice(r0, r0 + Q_GROUP)
        q_scr[rs, :] = _dot(h_scr[rs, :], wq_ref[...])


def _attn_prompt_body(head_dim, x_ref, nxa_ref, wq_ref, k_ref, v_ref, wo_ref, o_ref,
                      h_scr, q_scr, a_scr, s_scr, p_scr):
    h_scr, q_scr, a_scr, p_scr, wq_ref, wo_ref = (_Packed(r) for r in (h_scr, q_scr, a_scr, p_scr, wq_ref, wo_ref))
    _project_q(x_ref, nxa_ref, wq_ref, h_scr, q_scr)
    head = lambda h: slice(h * head_dim, (h + 1) * head_dim)
    _attend(q_scr, a_scr, s_scr, p_scr, 0, x_ref.shape[0], lambda h: k_ref[:, head(h)],
            lambda h: v_ref[:, head(h)], head_dim)
    o_ref[...] = x_ref[...] + _dot(a_scr[...], wo_ref[...])


def _attn_sample_body(head_dim, n_streams, n_rows, x_ref, nxa_ref, wq_ref, k_hbm, v_hbm, wo_ref, o_ref,
                      h_scr, q_scr, a_scr, s_scr, p_scr, kv_buf, kv_sem):
    h_scr, q_scr, a_scr, p_scr, wq_ref, wo_ref = (_Packed(r) for r in (h_scr, q_scr, a_scr, p_scr, wq_ref, wo_ref))
    i = pl.program_id(0)
    n_steps = pl.num_programs(0)
    n_heads = q_scr.shape[1] // head_dim
    slot = lax.rem(i, 2)

    def copies(step, dst_slot):
        out = []
        for which, hbm in enumerate((k_hbm, v_hbm)):
            for s in range(n_streams):
                for h in range(n_heads):
                    out.append(pltpu.make_async_copy(hbm.at[step * n_streams + s, :, h, :],
                                                     kv_buf.at[dst_slot, which, s, h],
                                                     kv_sem.at[dst_slot, which, s, h]))
        return out

    def start_all(cps):
        for n, cp in enumerate(cps):
            cp.start(priority=n % 2)

    @pl.when(i == 0)
    def _():
        start_all(copies(0, 0))

    @pl.when(i + 1 < n_steps)
    def _():
        start_all(copies(i + 1, 1 - slot))

    @pl.when(i == 0)
    def _():
        _project_q(x_ref, nxa_ref, wq_ref, h_scr, q_scr)

    for cp in copies(i, slot):
        cp.wait()

    for s in range(n_streams):
        row0 = pl.multiple_of((i * n_streams + s) * n_rows, n_rows)
        _attend(q_scr, a_scr, s_scr.at[s], p_scr.at[s], row0, n_rows,
                lambda h: kv_buf[slot, 0, s, h].astype(BF), lambda h: kv_buf[slot, 1, s, h].astype(BF), head_dim)

    @pl.when(i == n_steps - 1)
    def _():
        o_ref[...] = x_ref[...] + _dot(a_scr[...], wo_ref[...])


def _attn_prompt(x, k, v, p, n_heads, batch):
    rows, D = x.shape
    M = k.shape[0] // batch
    R = ATT_ROWS
    nt = rows // batch // R
    row_spec = pl.BlockSpec((R, D), lambda b, t: (b * nt + t, 0))
    mem_spec = pl.BlockSpec((M, D), lambda b, t: (b, 0))
    return pl.pallas_call(
        functools.partial(_attn_prompt_body, D // n_heads),
        grid=(batch, nt),
        in_specs=[row_spec, _const_spec((1, D)), _const_spec(p["w_q"].shape), mem_spec, mem_spec,
                  _const_spec(p["w_o"].shape)],
        out_specs=row_spec,
        out_shape=jax.ShapeDtypeStruct((rows, D), F32),
        scratch_shapes=[_packed_shape(R, D), _packed_shape(R, D), _packed_shape(R, D),
                        pltpu.VMEM((n_heads, R, M), F32), _packed_shape(R, M, (n_heads,))],
        compiler_params=_params(2),
        name="attn_prompt",
    )(x, p["norm_xa"], p["w_q"], k, v, p["w_o"])


def _attn_sample(x, k, v, p):
    rows, D = x.shape
    N, M, n_heads, head_dim = k.shape
    L = rows // N
    S = SAMPLE_ATT_STREAMS
    hbm = pl.BlockSpec(memory_space=pl.ANY)
    return pl.pallas_call(
        functools.partial(_attn_sample_body, head_dim, S, L),
        grid=(N // S,),
        in_specs=[_const_spec((rows, D)), _const_spec((1, D)), _const_spec(p["w_q"].shape), hbm, hbm,
                  _const_spec(p["w_o"].shape)],
        out_specs=pl.BlockSpec((rows, D), lambda i: (0, 0)),
        out_shape=jax.ShapeDtypeStruct((rows, D), F32),
        scratch_shapes=[_packed_shape(rows, D), _packed_shape(rows, D), _packed_shape(rows, D),
                        pltpu.VMEM((S, n_heads, L, M), F32), _packed_shape(L, M, (S, n_heads)),
                        pltpu.VMEM((2, 2, S, n_heads, M, head_dim), F32),
                        pltpu.SemaphoreType.DMA((2, 2, S, n_heads))],
        compiler_params=_params(1),
        name="attn_sample",
    )(x, p["norm_xa"], p["w_q"], k, v, p["w_o"])


def _ffn_body(streaming, n_streams, n_rows, *refs):
    S, L = n_streams, n_rows
    R = S * L
    refs = list(refs)
    x_ref = refs.pop(0)
    if streaming:
        st_ref = refs.pop(0)
    (nffn_ref, wua_ref, wug_ref, cw_ref, cb_ref, wd_ref, nfin_ref) = refs[:7]
    refs = refs[7:]
    y_ref, new_ref = refs.pop(0), refs.pop(0)
    if streaming:
        casts = tuple((src, _Packed(refs.pop(0))) for src in (wua_ref, wug_ref, wd_ref))
    else:
        wua_ref, wug_ref, wd_ref = _Packed(wua_ref), _Packed(wug_ref), _Packed(wd_ref)
    h_scr, acc_scr, ya_scr, yg_scr, act_scr, carry_scr = refs
    h_scr, act_scr = _Packed(h_scr), _Packed(act_scr)
    W = cw_ref.shape[0]
    DFF = cw_ref.shape[1] // 2
    C = wua_ref.shape[1]

    if streaming:
        j = pl.program_id(1)
    else:
        t, j = pl.program_id(1), pl.program_id(2)
    nj = pl.num_programs(2 - int(streaming))

    @pl.when(j == 0)
    def _():
        g = nffn_ref[...]
        def norm(r0):
            rows = pl.ds(r0, ROWS)
            h_scr[rows, :] = _rms(x_ref[rows, :], g)
            acc_scr[rows, :] = jnp.zeros((ROWS, acc_scr.shape[1]), F32)
        _rows_loop(R, ROWS, norm, unroll=2 * LOOP_UNROLL)
        if not streaming:
            @pl.when(t == 0)
            def _():
                carry_scr[...] = jnp.zeros(carry_scr.shape, F32)

    if streaming:
        for src, dst in casts:
            dst[...] = src[...]
        (_, wua_ref), (_, wug_ref), (_, wd_ref) = casts

    tail = slice(HALO + L - (W - 1), HALO + L)
    n_chunk = min(L, FFN_CHUNK)
    for c in range(C // FFN_SUB):
        cs = slice(c * FFN_SUB, (c + 1) * FFN_SUB)
        for half, (y_scr, w_ref) in enumerate(((ya_scr, wua_ref), (yg_scr, wug_ref))):
            y_scr[:, HALO:HALO + L, cs] = _dot(h_scr[...], w_ref[:, cs]).reshape(S, L, FFN_SUB)
            if streaming:
                for s in range(S):
                    y_scr[s, HALO - (W - 1):HALO, cs] = st_ref[s, half, :, cs]
                    new_ref[s, half, :, cs] = y_scr[s, tail, cs]
            else:
                y_scr[0, 0:HALO, cs] = carry_scr[j, half, :, cs]
                carry_scr[j, half, :, cs] = y_scr[0, L:L + HALO, cs]
                out_cols = pl.ds(pl.multiple_of(j * C + c * FFN_SUB, FFN_SUB), FFN_SUB)
                new_ref[half, :, out_cols] = y_scr[0, tail, cs]

        lin_cols = pl.ds(pl.multiple_of(j * C + c * FFN_SUB, FFN_SUB), FFN_SUB)
        gated_cols = pl.ds(pl.multiple_of(DFF + j * C + c * FFN_SUB, FFN_SUB), FFN_SUB)
        taps_a = [jnp.broadcast_to(cw_ref[k:k + 1, lin_cols], (n_chunk, FFN_SUB)) for k in range(W)]
        taps_g = [jnp.broadcast_to(cw_ref[k:k + 1, gated_cols], (n_chunk, FFN_SUB)) for k in range(W)]
        bias_a = jnp.broadcast_to(cb_ref[:, lin_cols], (n_chunk, FFN_SUB))
        bias_g = jnp.broadcast_to(cb_ref[:, gated_cols], (n_chunk, FFN_SUB))
        for s in range(S):
            for r0 in range(0, L, n_chunk):
                ext = slice(r0, r0 + n_chunk + HALO)
                lin = _causal_conv(ya_scr[s, ext, cs], taps_a, bias_a)
                gated = _causal_conv(yg_scr[s, ext, cs], taps_g, bias_g)
                out0 = s * L + r0
                act_scr[out0:out0 + n_chunk, cs] = _gelu(gated) * lin

    for ks in (slice(0, C - FFN_SUB), slice(C - FFN_SUB, C)) if C > FFN_SUB else (slice(0, C),):
        acc_scr[...] += _dot(act_scr[:, ks], wd_ref[ks, :])

    @pl.when(j == nj - 1)
    def _():
        g = nfin_ref[...]
        def final(r0):
            rows = pl.ds(r0, ROWS)
            y_ref[rows, :] = _rms(x_ref[rows, :] + acc_scr[rows, :], g)
        _rows_loop(R, ROWS, final, unroll=2 * LOOP_UNROLL)


def _ffn(streaming, x, state, p, w_up_lin, w_up_gated, w_down):
    rows, D = x.shape
    DFF = w_down.shape[0] if streaming else 2 * w_down.shape[0]
    W = p["ffn_conv_w"].shape[0]
    C = FFN_SAMPLE_COLS if streaming else FFN_COLS
    nj = DFF // C
    (wua, off_a), (wug, off_g) = w_up_lin, w_up_gated
    if streaming:
        N = state.shape[0]
        L = rows // N
        R = FFN_ROWS
        S = R // L
        grid = (rows // R, nj)
        row_map = lambda i, j: (i, 0)
        colmap = lambda off: (lambda i, j: (0, j + off))
        wd_map = lambda i, j: (j, 0)
        new_spec = pl.BlockSpec((S, 2, W - 1, C), lambda i, j: (i, 0, 0, j))
        new_shape = jax.ShapeDtypeStruct((N, 2, W - 1, DFF), F32)
        in_specs = [pl.BlockSpec((R, D), row_map),
                    pl.BlockSpec((S, 2, W - 1, C), lambda i, j: (i, 0, 0, j))]
        args = [x, state]
    else:
        B, T = state
        S, L, R = 1, FFN_ROWS, FFN_ROWS
        nt = T // R
        grid = (B, nt, nj)
        row_map = lambda b, t, j: (b * nt + t, 0)
        colmap = lambda off: (lambda b, t, j: (0, j + off))
        wd_map = lambda b, t, j: (j, 0)
        new_spec = pl.BlockSpec((None, None, 2, W - 1, DFF), lambda b, t, j: (b, t, 0, 0, 0))
        new_shape = jax.ShapeDtypeStruct((B, nt, 2, W - 1, DFF), F32)
        in_specs = [pl.BlockSpec((R, D), row_map)]
        args = [x]
    in_specs += [_const_spec((1, D)),
                 pl.BlockSpec((wua.shape[0], C), colmap(off_a // C)), pl.BlockSpec((wug.shape[0], C), colmap(off_g // C)),
                 _const_spec((W, 2 * DFF)), _const_spec((1, 2 * DFF)),
                 pl.BlockSpec((C if streaming else C // 2, D), wd_map), _const_spec((1, D))]
    args += [p["norm_ffn"], wua, wug, p["ffn_conv_w"], p["ffn_conv_b"], w_down, p["norm_final"]]
    out_specs = [pl.BlockSpec((R, D), row_map), new_spec]
    out_shape = [jax.ShapeDtypeStruct((rows, D), F32), new_shape]
    if streaming:
        out_specs += [pl.BlockSpec((D // 2, C), colmap(0)), pl.BlockSpec((D // 2, C), colmap(0)),
                      pl.BlockSpec((C // 2, D), wd_map)]
        out_shape += [jax.ShapeDtypeStruct((D // 2, DFF), U32), jax.ShapeDtypeStruct((D // 2, DFF), U32),
                      jax.ShapeDtypeStruct((DFF // 2, D), U32)]
    scratch = [_packed_shape(R, D),
               pltpu.VMEM((R, D), F32),
               pltpu.VMEM((S, L + HALO, C), F32),
               pltpu.VMEM((S, L + HALO, C), F32),
               _packed_shape(R, C),
               pltpu.VMEM((nj, 2, HALO, C), F32)]
    return pl.pallas_call(
        functools.partial(_ffn_body, streaming, S, L),
        grid=grid, in_specs=in_specs, out_specs=out_specs, out_shape=out_shape,
        scratch_shapes=scratch,
        compiler_params=_params(len(grid)),
        name="ffn_sample" if streaming else "ffn_prompt",
    )(*args)


def kernel(x_prompt, x_sample, mem_prompt, cache_mem_k, cache_mem_v, state_lru_h, state_lru_conv, state_ffn_conv, norm_mix, w_in, g_v, gmlp_w, gmlp_b, lru_conv_w, lru_conv_b, lru_wa, lru_ba, lru_wx, lru_bx, lru_lam, g_a, g_b, w_out, norm_mem, w_kv, norm_xa, w_q, w_o, norm_ffn, w_up, ffn_conv_w, ffn_conv_b, w_down, norm_final):
    depth = w_in.shape[0]
    assert depth == 1, "single-layer trunk"
    B, T, D = x_prompt.shape
    N, L, _ = x_sample.shape
    M = mem_prompt.shape[1]
    n_heads, head_dim = cache_mem_k.shape[3], cache_mem_k.shape[4]
    G, NCH = gmlp_w.shape[1], gmlp_w.shape[2]
    DA, DB = g_v.shape[1], lru_lam.shape[1]
    GD = DA // G
    DFF = w_down.shape[1]
    W_LRU, W_FFN = lru_conv_w.shape[1], ffn_conv_w.shape[1]
    assert L <= CAUSAL_CHUNK and NCH % L == 0 and L == ROWS
    row = lambda a: a.reshape(1, -1)

    shared = {
        "norm_mix": row(norm_mix[0]), "g_v": row(g_v[0]),
        "lru_conv_w": lru_conv_w[0], "lru_conv_b": row(lru_conv_b[0]),
        "lru_wa": lru_wa[0], "lru_ba": row(lru_ba[0]),
        "lru_wx": lru_wx[0], "lru_bx": row(lru_bx[0]), "lru_lam": row(lru_lam[0]),
        "g_a": row(g_a[0]), "g_b": row(g_b[0]),
        "norm_xa": row(norm_xa[0]),
        "norm_ffn": row(norm_ffn[0]), "ffn_conv_w": ffn_conv_w[0],
        "ffn_conv_b": row(ffn_conv_b[0]), "norm_final": row(norm_final),
    }
    (mk, mv, mk_b, mv_b), resident = _memory_kv(mem_prompt.reshape(B * M, D), row(norm_mem[0]), w_kv[0],
                                                [w_in[0], w_out[0], w_q[0], w_o[0]], head_dim)
    shared.update(zip(("w_in", "w_out", "w_q", "w_o"), resident))
    prompt_p = dict(shared, gmlp_w_eff=gmlp_w[0],
                    gmlp_bias_rows=jnp.repeat(gmlp_b[0].T, GD, axis=1))
    reps = NCH // L
    sample_p = dict(shared, gmlp_w_eff=jnp.tile(gmlp_w[0][:, :L, :L], (1, reps, reps)),
                    gmlp_bias_rows=jnp.repeat(jnp.tile(gmlp_b[0][:, :L].T, (reps, 1)), GD, axis=1))

    xs = x_sample.reshape(N * L, D)
    pconv = jnp.pad(state_lru_conv[0], ((0, 0), (HALO - (W_LRU - 1), 0), (0, 0)))
    xs, v_s, conv_s, h_s = _mixer(True, xs, (pconv, state_lru_h[0].reshape(N, 1, DB)), sample_p)
    xs = _attn_sample(xs, cache_mem_k[0], cache_mem_v[0], sample_p)
    ffn_prev = jnp.swapaxes(state_ffn_conv[0].reshape(N, W_FFN - 1, 2, DFF), 1, 2)
    y_s, ffn_s, w_up_lin, w_up_gated, w_down_b = _ffn(True, xs, ffn_prev, sample_p,
                                                      (w_up[0], 0), (w_up[0], DFF), w_down[0])

    xp = x_prompt.reshape(B * T, D)
    xp, conv_p, h_p = _mixer(False, xp, (B, T), prompt_p)
    xp = _attn_prompt(xp, mk_b, mv_b, prompt_p, n_heads, B)
    y_p, ffn_p = _ffn(False, xp, (B, T), prompt_p, (w_up_lin, 0), (w_up_gated, 0), w_down_b)

    def ffn_state(a):
        return jnp.swapaxes(a, 1, 2).reshape(1, a.shape[0], W_FFN - 1, 2 * DFF)

    return (y_p.reshape(B, T, D), y_s.reshape(N, L, D),
            mk.reshape(1, B, M, n_heads, head_dim), mv.reshape(1, B, M, n_heads, head_dim),
            h_p.reshape(1, B, DB), conv_p.reshape(1, B, W_LRU - 1, DB), ffn_state(ffn_p[:, -1]),
            h_s.reshape(1, N, DB), conv_s.reshape(1, N, W_LRU - 1, DB), ffn_state(ffn_s),
            v_s.reshape(1, N, L, DA))
```

```python
import functools

import jax
import jax.numpy as jnp
from jax import lax
from jax.experimental import pallas as pl
from jax.experimental.pallas import tpu as pltpu

F32 = jnp.float32
BF = jnp.bfloat16
U32 = jnp.uint32

EPS = 1e-6
LRU_C = 8.0
CAUSAL_CHUNK = 64
HALO = 8
ROWS = 16
LOOP_UNROLL = 4
SOFTMAX_ROWS = 64
Q_GROUP = 128
MIX_CHUNK = 32
MIX_ORDER = ("norm", "xr", "conv", "maps", "gate", "v", "vpath", "u", "gates", "spatial", "norm_a", "out_a", "scan",
             "norm_b", "out_b")
V7X_VMEM_BYTES = 64 * 1024 * 1024
VMEM_LIMIT = V7X_VMEM_BYTES - 8 * 1024 * 1024

MIX_ROWS = 256
ATT_ROWS = 512
FFN_ROWS = 512
FFN_COLS = 1024
FFN_SAMPLE_COLS = 256
FFN_SUB = 256
FFN_CHUNK = 64
KV_COLS = 256
SAMPLE_MIX_STREAMS = 16
SAMPLE_ATT_STREAMS = 1
SAMPLE_ATT_SLOTS = 4


class _Packed:
    def __init__(self, ref):
        self.ref = ref

    @property
    def shape(self):
        return self.ref.shape[:-2] + (2 * self.ref.shape[-2], self.ref.shape[-1])

    @property
    def at(self):
        packed = self

        class _At:
            def __getitem__(self, idx):
                return _Packed(packed.ref.at[idx])
        return _At()

    def _index(self, idx):
        nd = len(self.ref.shape)
        idx = idx if isinstance(idx, tuple) else (idx,)
        if Ellipsis in idx:
            k = idx.index(Ellipsis)
            idx = idx[:k] + (slice(None),) * (nd - len(idx) + 1) + idx[k + 1:]
        idx = list(idx) + [slice(None)] * (nd - len(idx))
        rows = idx[nd - 2]
        if isinstance(rows, slice):
            half = lambda v: None if v is None else v // 2
            assert rows.step is None and (rows.start or 0) % ROWS == 0 and (rows.stop is None or rows.stop % ROWS == 0)
            idx[nd - 2] = slice(half(rows.start), half(rows.stop))
        else:
            assert rows.size % ROWS == 0
            start = rows.start // 2 if isinstance(rows.start, int) else pl.multiple_of(rows.start // 2, ROWS // 2)
            idx[nd - 2] = pl.ds(start, rows.size // 2)
        return tuple(idx)

    def __getitem__(self, idx):
        return pltpu.bitcast(self.ref[self._index(idx)], BF)

    def __setitem__(self, idx, value):
        self.ref[self._index(idx)] = pltpu.bitcast(value.astype(BF), U32)


def _packed_shape(rows, cols, lead=()):
    return pltpu.VMEM(tuple(lead) + (rows // 2, cols), U32)


def _rms(x, g):
    return x * lax.rsqrt(jnp.mean(x * x, axis=-1, keepdims=True) + EPS) * g


def _gelu(x):
    return x * (0.5 * (1.0 + jnp.tanh(0.7978845608028654 * (x + 0.044715 * (x * x * x)))))


def _sigmoid(x):
    return 0.5 + 0.5 * jnp.tanh(0.5 * x)


def _softplus(x):
    return jnp.maximum(x, 0.0) + jnp.log1p(jnp.exp(-jnp.abs(x)))


def _dot(a, b):
    return jnp.dot(a, b, preferred_element_type=F32)


def _rows_loop(n_rows, chunk, fn, unroll=LOOP_UNROLL):
    def body(i, carry):
        fn(pl.multiple_of(i * chunk, chunk))
        return carry
    trips = n_rows // chunk
    lax.fori_loop(0, trips, body, 0, unroll=min(unroll, trips))


def _causal_conv(ext, taps, bias):
    width = len(taps)
    acc = None
    for k in range(width):
        shift = width - 1 - k
        src = ext if shift == 0 else pltpu.roll(ext, shift, axis=0)
        term = src[HALO:, :] * taps[k]
        acc = term if acc is None else acc + term
    return acc + bias


def _const_spec(shape):
    nd = len(shape)
    return pl.BlockSpec(shape, lambda *_: (0,) * nd, pipeline_mode=pl.Buffered(1))


def _params(n_grid):
    return pltpu.CompilerParams(dimension_semantics=("arbitrary",) * n_grid, vmem_limit_bytes=VMEM_LIMIT)


def _kv_body(n_casts, head_dim, mem_ref, g_ref, wk_ref, wv_ref, *refs):
    cast_in, refs = refs[:n_casts], refs[n_casts:]
    k_hbm, v_hbm, kb_ref, vb_ref = refs[:4]
    cast_out = refs[4:4 + n_casts]
    h_scr, stage, sem = refs[4 + n_casts:]
    h_scr = _Packed(h_scr)
    cast_out = [_Packed(r) for r in cast_out]
    j, n_steps = pl.program_id(0), pl.num_programs(0)
    slot = lax.rem(j, 2)
    cols = wk_ref.shape[1]
    per_head = head_dim // cols

    def stores(step, slot):
        head = lax.div(step, per_head)
        c0 = pl.multiple_of(lax.rem(step, per_head) * cols, cols)
        return [pltpu.make_async_copy(stage.at[slot, which], hbm.at[:, head, pl.ds(c0, cols)], sem.at[slot, which])
                for which, hbm in enumerate((k_hbm, v_hbm))]

    @pl.when(j == 0)
    def _():
        g = g_ref[...]
        def norm(r0):
            rows = pl.ds(r0, ROWS)
            h_scr[rows, :] = _rms(mem_ref[rows, :], g)
        _rows_loop(mem_ref.shape[0], ROWS, norm)

    @pl.when(j >= 2)
    def _():
        for cp in stores(j - 2, slot):
            cp.wait()

    k = _dot(h_scr[...], wk_ref[...].astype(BF))
    stage[slot, 0] = k
    kb_ref[...] = k.astype(BF)
    v = _dot(h_scr[...], wv_ref[...].astype(BF))
    stage[slot, 1] = v
    vb_ref[...] = v.astype(BF)
    for cp in stores(j, slot):
        cp.start()
    for src, dst in zip(cast_in, cast_out):
        dst[...] = src[...]

    @pl.when(j == n_steps - 1)
    def _():
        for cp in stores(j - 1, 1 - slot) + stores(j, slot):
            cp.wait()


def _memory_kv(mem, g, w_kv, resident, head_dim):
    m, d = mem.shape
    n_steps = d // KV_COLS
    assert n_steps >= 2 and head_dim % KV_COLS == 0
    col = pl.BlockSpec((m, KV_COLS), lambda j: (0, j))
    hbm = pl.BlockSpec(memory_space=pl.ANY)
    cast_specs = [pl.BlockSpec((w.shape[0] // n_steps, w.shape[1]), lambda j: (j, 0)) for w in resident]
    packed_specs = [pl.BlockSpec((w.shape[0] // n_steps // 2, w.shape[1]), lambda j: (j, 0)) for w in resident]
    kv_shape = jax.ShapeDtypeStruct((m, d // head_dim, head_dim), F32)
    outs = pl.pallas_call(
        functools.partial(_kv_body, len(resident), head_dim),
        grid=(n_steps,),
        in_specs=[_const_spec((m, d)), _const_spec((1, d)),
                  pl.BlockSpec((d, KV_COLS), lambda j: (0, j)),
                  pl.BlockSpec((d, KV_COLS), lambda j: (0, j + n_steps))] + cast_specs,
        out_specs=[hbm, hbm, col, col] + packed_specs,
        out_shape=[kv_shape, kv_shape, jax.ShapeDtypeStruct((m, d), BF), jax.ShapeDtypeStruct((m, d), BF)]
                  + [jax.ShapeDtypeStruct((w.shape[0] // 2, w.shape[1]), U32) for w in resident],
        scratch_shapes=[_packed_shape(m, d), pltpu.VMEM((2, 2, m, KV_COLS), F32),
                        pltpu.SemaphoreType.DMA((2, 2))],
        compiler_params=_params(1),
        name="memory_kv",
    )(mem, g, w_kv, w_kv, *resident)
    return outs[:4], outs[4:]


def _scan_chunk(a, b, carry):
    n = a.shape[0]
    pos = lax.broadcasted_iota(jnp.int32, a.shape, 0)
    d = 1
    while d < n:
        keep = pos >= d
        a_prev = jnp.where(keep, pltpu.roll(a, d, axis=0), 1.0)
        b_prev = jnp.where(keep, pltpu.roll(b, d, axis=0), 0.0)
        b = a * b_prev + b
        a = a * a_prev
        d *= 2
    h = b + a * carry
    return h, h[n - 1:n, :]


def _mixer_body(streaming, n_streams, n_rows, *refs):
    S, L = n_streams, n_rows
    R = S * L
    refs = list(refs)
    x_ref = refs.pop(0)
    if streaming:
        pconv_ref, h0_ref = refs.pop(0), refs.pop(0)
    (nmix_ref, win_ref, gv_ref, gw_ref, gbias_ref, cw_ref, cb_ref, wa_ref, ba_ref, wx_ref, bx_ref,
     lam_ref, ga_ref, gb_ref, wout_ref) = refs[:15]
    refs = refs[15:]
    x1_ref = refs.pop(0)
    if streaming:
        v_ref = refs.pop(0)
    convnew_ref, hlast_ref = refs.pop(0), refs.pop(0)
    h_scr, z_scr, vb_scr, y_scr, xc_scr, xcb_scr, ra_scr, rx_scr, mix_scr, gwm_scr, carry_scr = refs
    h_scr, vb_scr, xcb_scr, mix_scr, gwm_scr = (_Packed(r) for r in (h_scr, vb_scr, xcb_scr, mix_scr, gwm_scr))
    win_ref, wout_ref = _Packed(win_ref), _Packed(wout_ref)

    DA = gv_ref.shape[1]
    DB = lam_ref.shape[1]
    G, NCH = gw_ref.shape[0], gw_ref.shape[1]
    GD = DA // G
    H, HD = wa_ref.shape[0], wa_ref.shape[1]
    W = cw_ref.shape[0]

    if streaming:
        first = pl.program_id(0) == 0
    else:
        t = pl.program_id(1)
        first = jnp.logical_and(pl.program_id(0) == 0, t == 0)

    @pl.when(first)
    def _():
        ri = lax.broadcasted_iota(jnp.int32, (NCH, NCH), 0)
        ci = lax.broadcasted_iota(jnp.int32, (NCH, NCH), 1)
        if streaming:
            sh = L.bit_length() - 1
            keep = lax.shift_right_logical(ri, sh) == lax.shift_right_logical(ci, sh)
        else:
            sh = CAUSAL_CHUNK.bit_length() - 1
            keep = lax.shift_right_logical(ri, sh) >= lax.shift_right_logical(ci, sh)
        for g in range(G):
            gwm_scr[g] = jnp.where(keep, gw_ref[g], 0.0)

    if streaming:
        y_scr[:, 0:HALO, :] = pconv_ref[...]
    else:
        @pl.when(t == 0)
        def _():
            y_scr[0, 0:HALO, :] = jnp.zeros((HALO, DB), F32)
            carry_scr[...] = jnp.zeros((1, DB), F32)

    nmix, gv, ga, gb = nmix_ref[...], gv_ref[...], ga_ref[...], gb_ref[...]
    cw, cb = [cw_ref[k:k + 1, :] for k in range(W)], cb_ref[...]
    ba, bx = ba_ref[...], bx_ref[...]
    sp_lam = _softplus(-lam_ref[...])
    chunks = [slice(c0, c0 + MIX_CHUNK) for c0 in range(0, R, MIX_CHUNK)]
    if streaming:
        conv_in = [(lambda s=s: y_scr[s], slice(s * L, (s + 1) * L)) for s in range(S)]
    else:
        conv_in = [(lambda c=c: y_scr[0, c.start:c.stop + HALO, :], c) for c in chunks]
    state = {"carry": None if streaming else carry_scr[...], "first_out": True}

    def p_norm():
        for c in chunks:
            h_scr[c, :] = _rms(x_ref[c, :], nmix)

    def p_xr():
        xr = _dot(h_scr[...], win_ref[:, 2 * DA:2 * DA + DB])
        y_scr[:, HALO:HALO + L, :] = xr.reshape(S, L, DB)

    def p_v():
        z_scr[:, DA:2 * DA] = _dot(h_scr[...], win_ref[:, DA:2 * DA])

    def p_gate():
        z_scr[:, 2 * DA:2 * DA + DB] = _dot(h_scr[...], win_ref[:, 2 * DA + DB:2 * DA + 2 * DB])

    def p_u():
        z_scr[:, 0:DA] = _dot(h_scr[...], win_ref[:, 0:DA])

    def p_conv():
        for ext, c in conv_in:
            xc = _causal_conv(ext(), cw, cb)
            xc_scr[c, :] = xc
            xcb_scr[c, :] = xc

    def p_maps():
        for hh in range(H):
            cs = slice(hh * HD, (hh + 1) * HD)
            ra_scr[:, cs] = _dot(xcb_scr[:, cs], wa_ref[hh].astype(BF))
            rx_scr[:, cs] = _dot(xcb_scr[:, cs], wx_ref[hh].astype(BF))

    def p_vpath():
        for c in chunks:
            v = _rms(_gelu(z_scr[c, DA:2 * DA]), gv)
            if streaming:
                v_ref[c, :] = v
            vb_scr[c, :] = v

    def p_gates():
        for c in chunks:
            r = _sigmoid(ra_scr[c, :] + ba)
            i = _sigmoid(rx_scr[c, :] + bx)
            a = jnp.exp(-LRU_C * r * sp_lam)
            ra_scr[c, :] = a
            rx_scr[c, :] = jnp.sqrt(1.0 - a * a) * (i * xc_scr[c, :])

    def p_spatial():
        for n0 in range(0, R, NCH):
            ns = slice(n0, n0 + NCH)
            for g in range(G):
                cs = slice(g * GD, (g + 1) * GD)
                sp = _dot(gwm_scr[g], vb_scr[ns, cs])
                z_scr[ns, cs] = _gelu(z_scr[ns, cs]) * (sp + gbias_ref[:, cs])

    def p_norm_a():
        for c in chunks:
            mix_scr[c, 0:DA] = _rms(z_scr[c, 0:DA], ga)

    def p_scan():
        if streaming:
            for s in range(S):
                c = slice(s * L, (s + 1) * L)
                h, last = _scan_chunk(ra_scr[c, :], rx_scr[c, :], h0_ref[s])
                rx_scr[c, :] = h
                hlast_ref[s] = last
        else:
            carry = state["carry"]
            for c0 in range(0, R, HALO):
                c = slice(c0, c0 + HALO)
                h, carry = _scan_chunk(ra_scr[c, :], rx_scr[c, :], carry)
                rx_scr[c, :] = h
            state["carry"] = carry

    def p_norm_b():
        for c in chunks:
            out_b = rx_scr[c, :] * _gelu(z_scr[c, 2 * DA:2 * DA + DB])
            mix_scr[c, DA:DA + DB] = _rms(out_b, gb)

    def p_out(cols):
        upd = _dot(mix_scr[:, cols], wout_ref[cols, :])
        if state["first_out"]:
            x1_ref[...] = x_ref[...] + upd
            state["first_out"] = False
        else:
            x1_ref[...] += upd

    pieces = {"norm": p_norm, "xr": p_xr, "v": p_v, "gate": p_gate, "u": p_u, "conv": p_conv, "maps": p_maps,
              "vpath": p_vpath, "gates": p_gates, "spatial": p_spatial, "norm_a": p_norm_a, "scan": p_scan,
              "norm_b": p_norm_b, "out_a": lambda: p_out(slice(0, DA)), "out_b": lambda: p_out(slice(DA, DA + DB))}
    assert sorted(MIX_ORDER) == sorted(pieces)
    for name in MIX_ORDER:
        pieces[name]()

    tail = slice(HALO + L - (W - 1), HALO + L)
    if streaming:
        for s in range(S):
            convnew_ref[s] = y_scr[s, tail, :]
    else:
        convnew_ref[...] = y_scr[0, tail, :]
        y_scr[0, 0:HALO, :] = y_scr[0, L:L + HALO, :]
        carry_scr[...] = state["carry"]
        hlast_ref[...] = state["carry"]


def _mixer(streaming, x, states, p):
    rows, D = x.shape
    DA, DB = p["g_v"].shape[1], p["lru_lam"].shape[1]
    weights = [p["norm_mix"], p["w_in"], p["g_v"], p["gmlp_w_eff"], p["gmlp_bias_rows"], p["lru_conv_w"],
               p["lru_conv_b"], p["lru_wa"], p["lru_ba"], p["lru_wx"], p["lru_bx"], p["lru_lam"], p["g_a"],
               p["g_b"], p["w_out"]]
    w_specs = [_const_spec(w.shape) for w in weights]
    W = p["lru_conv_w"].shape[0]
    if streaming:
        pconv, h0 = states
        N = h0.shape[0]
        L = rows // N
        S = SAMPLE_MIX_STREAMS
        R = S * L
        grid = (N // S,)
        row_spec = lambda c: pl.BlockSpec((R, c), lambda i: (i, 0))
        in_specs = [row_spec(D), pl.BlockSpec((S, HALO, DB), lambda i: (i, 0, 0)),
                    pl.BlockSpec((S, 1, DB), lambda i: (i, 0, 0))] + w_specs
        out_specs = [row_spec(D), row_spec(DA), pl.BlockSpec((S, W - 1, DB), lambda i: (i, 0, 0)),
                     pl.BlockSpec((S, 1, DB), lambda i: (i, 0, 0))]
        out_shape = [jax.ShapeDtypeStruct((rows, D), F32), jax.ShapeDtypeStruct((rows, DA), F32),
                     jax.ShapeDtypeStruct((N, W - 1, DB), F32), jax.ShapeDtypeStruct((N, 1, DB), F32)]
        args = [x, pconv, h0] + weights
    else:
        B, T = states
        S, L, R = 1, MIX_ROWS, MIX_ROWS
        nt = T // R
        grid = (B, nt)
        row_spec = lambda c: pl.BlockSpec((R, c), lambda b, t: (b * nt + t, 0))
        in_specs = [row_spec(D)] + w_specs
        out_specs = [row_spec(D), pl.BlockSpec((None, W - 1, DB), lambda b, t: (b, 0, 0)),
                     pl.BlockSpec((None, 1, DB), lambda b, t: (b, 0, 0))]
        out_shape = [jax.ShapeDtypeStruct((rows, D), F32), jax.ShapeDtypeStruct((B, W - 1, DB), F32),
                     jax.ShapeDtypeStruct((B, 1, DB), F32)]
        args = [x] + weights
    G, NCH = p["gmlp_w_eff"].shape[0], p["gmlp_w_eff"].shape[1]
    scratch = [_packed_shape(R, D),
               pltpu.VMEM((R, 2 * DA + DB), F32),
               _packed_shape(R, DA),
               pltpu.VMEM((S, L + HALO, DB), F32),
               pltpu.VMEM((R, DB), F32),
               _packed_shape(R, DB),
               pltpu.VMEM((R, DB), F32),
               pltpu.VMEM((R, DB), F32),
               _packed_shape(R, DA + DB),
               _packed_shape(NCH, NCH, (G,)),
               pltpu.VMEM((1, DB), F32)]
    return pl.pallas_call(
        functools.partial(_mixer_body, streaming, S, L),
        grid=grid, in_specs=in_specs, out_specs=out_specs, out_shape=out_shape, scratch_shapes=scratch,
        compiler_params=_params(len(grid)),
        name="mixer_sample" if streaming else "mixer_prompt",
    )(*args)


def _attend(q_scr, o_scr, s_scr, p_scr, row0, n_rows, keys, values, head_dim):
    n_heads = q_scr.shape[1] // head_dim
    scale = head_dim ** -0.5
    rows = pl.ds(row0, n_rows)
    n_chunk = min(n_rows, SOFTMAX_ROWS)
    heads = [slice(h * head_dim, (h + 1) * head_dim) for h in range(n_heads)]
    for h, cs in enumerate(heads):
        s_scr[h] = lax.dot_general(q_scr[rows, cs], keys(h), (((1,), (1,)), ((), ())),
                                   preferred_element_type=F32) * scale
    for h in range(n_heads):
        for r0 in range(0, n_rows, n_chunk):
            s = s_scr[h, r0:r0 + n_chunk, :]
            e = jnp.exp(s - jnp.max(s, axis=-1, keepdims=True))
            p_scr[h, r0:r0 + n_chunk, :] = e / jnp.sum(e, axis=-1, keepdims=True)
    for h, cs in enumerate(heads):
        o_scr[rows, cs] = _dot(p_scr[h], values(h))


def _project_q(x_ref, nxa_ref, wq_ref, h_scr, q_scr):
    g = nxa_ref[...]
    for r0 in range(0, x_ref.shape[0], Q_GROUP):
        for c0 in range(r0, r0 + Q_GROUP, MIX_CHUNK):
            c = slice(c0, c0 + MIX_CHUNK)
            h_scr[c, :] = _rms(x_ref[c, :], g)
        rs = slice(r0, r0 + Q_GROUP)
        q_scr[rs, :] = _dot(h_scr[rs, :], wq_ref[...])


def _attn_prompt_body(head_dim, x_ref, nxa_ref, wq_ref, k_ref, v_ref, wo_ref, o_ref,
                      h_scr, q_scr, a_scr, s_scr, p_scr):
    h_scr, q_scr, a_scr, p_scr, wq_ref, wo_ref = (_Packed(r) for r in (h_scr, q_scr, a_scr, p_scr, wq_ref, wo_ref))
    _project_q(x_ref, nxa_ref, wq_ref, h_scr, q_scr)
    head = lambda h: slice(h * head_dim, (h + 1) * head_dim)
    _attend(q_scr, a_scr, s_scr, p_scr, 0, x_ref.shape[0], lambda h: k_ref[:, head(h)],
            lambda h: v_ref[:, head(h)], head_dim)
    o_ref[...] = x_ref[...] + _dot(a_scr[...], wo_ref[...])


def _attn_sample_body(head_dim, n_streams, n_rows, x_ref, nxa_ref, wq_ref, k_hbm, v_hbm, wo_ref, o_ref,
                      h_scr, q_scr, a_scr, s_scr, p_scr, kv_buf, kv_sem):
    h_scr, q_scr, a_scr, p_scr, wq_ref, wo_ref = (_Packed(r) for r in (h_scr, q_scr, a_scr, p_scr, wq_ref, wo_ref))
    i = pl.program_id(0)
    n_steps = pl.num_programs(0)
    n_heads = q_scr.shape[1] // head_dim
    n_slots = kv_buf.shape[0]
    ahead = n_slots - 1
    slot = lax.rem(i, n_slots)

    def copies(step, dst_slot):
        out = []
        for which, hbm in enumerate((k_hbm, v_hbm)):
            for s in range(n_streams):
                for h in range(n_heads):
                    out.append(pltpu.make_async_copy(hbm.at[step * n_streams + s, :, h, :],
                                                     kv_buf.at[dst_slot, which, s, h],
                                                     kv_sem.at[dst_slot, which, s, h]))
        return out

    @pl.when(i == 0)
    def _():
        for step in range(ahead):
            for cp in copies(step, step):
                cp.start()

    @pl.when(i + ahead < n_steps)
    def _():
        for cp in copies(i + ahead, lax.rem(i + ahead, n_slots)):
            cp.start()

    @pl.when(i == 0)
    def _():
        _project_q(x_ref, nxa_ref, wq_ref, h_scr, q_scr)

    for cp in copies(i, slot):
        cp.wait()

    for s in range(n_streams):
        row0 = pl.multiple_of((i * n_streams + s) * n_rows, n_rows)
        _attend(q_scr, a_scr, s_scr.at[s], p_scr.at[s], row0, n_rows,
                lambda h: kv_buf[slot, 0, s, h].astype(BF), lambda h: kv_buf[slot, 1, s, h].astype(BF), head_dim)

    @pl.when(i == n_steps - 1)
    def _():
        o_ref[...] = x_ref[...] + _dot(a_scr[...], wo_ref[...])


def _attn_prompt(x, k, v, p, n_heads, batch):
    rows, D = x.shape
    M = k.shape[0] // batch
    R = ATT_ROWS
    nt = rows // batch // R
    row_spec = pl.BlockSpec((R, D), lambda b, t: (b * nt + t, 0))
    mem_spec = pl.BlockSpec((M, D), lambda b, t: (b, 0))
    return pl.pallas_call(
        functools.partial(_attn_prompt_body, D // n_heads),
        grid=(batch, nt),
        in_specs=[row_spec, _const_spec((1, D)), _const_spec(p["w_q"].shape), mem_spec, mem_spec,
                  _const_spec(p["w_o"].shape)],
        out_specs=row_spec,
        out_shape=jax.ShapeDtypeStruct((rows, D), F32),
        scratch_shapes=[_packed_shape(R, D), _packed_shape(R, D), _packed_shape(R, D),
                        pltpu.VMEM((n_heads, R, M), F32), _packed_shape(R, M, (n_heads,))],
        compiler_params=_params(2),
        name="attn_prompt",
    )(x, p["norm_xa"], p["w_q"], k, v, p["w_o"])


def _attn_sample(x, k, v, p):
    rows, D = x.shape
    N, M, n_heads, head_dim = k.shape
    L = rows // N
    S = SAMPLE_ATT_STREAMS
    hbm = pl.BlockSpec(memory_space=pl.ANY)
    return pl.pallas_call(
        functools.partial(_attn_sample_body, head_dim, S, L),
        grid=(N // S,),
        in_specs=[_const_spec((rows, D)), _const_spec((1, D)), _const_spec(p["w_q"].shape), hbm, hbm,
                  _const_spec(p["w_o"].shape)],
        out_specs=pl.BlockSpec((rows, D), lambda i: (0, 0)),
        out_shape=jax.ShapeDtypeStruct((rows, D), F32),
        scratch_shapes=[_packed_shape(rows, D), _packed_shape(rows, D), _packed_shape(rows, D),
                        pltpu.VMEM((S, n_heads, L, M), F32), _packed_shape(L, M, (S, n_heads)),
                        pltpu.VMEM((SAMPLE_ATT_SLOTS, 2, S, n_heads, M, head_dim), F32),
                        pltpu.SemaphoreType.DMA((SAMPLE_ATT_SLOTS, 2, S, n_heads))],
        compiler_params=_params(1),
        name="attn_sample",
    )(x, p["norm_xa"], p["w_q"], k, v, p["w_o"])


def _ffn_body(streaming, n_streams, n_rows, *refs):
    S, L = n_streams, n_rows
    R = S * L
    refs = list(refs)
    x_ref = refs.pop(0)
    if streaming:
        st_ref = refs.pop(0)
    (nffn_ref, wua_ref, wug_ref, cw_ref, cb_ref, wd_ref, nfin_ref) = refs[:7]
    refs = refs[7:]
    y_ref, new_ref = refs.pop(0), refs.pop(0)
    if streaming:
        casts = tuple((src, _Packed(refs.pop(0))) for src in (wua_ref, wug_ref, wd_ref))
    else:
        wua_ref, wug_ref, wd_ref = _Packed(wua_ref), _Packed(wug_ref), _Packed(wd_ref)
    h_scr, acc_scr, ya_scr, yg_scr, act_scr, carry_scr = refs
    h_scr, act_scr = _Packed(h_scr), _Packed(act_scr)
    W = cw_ref.shape[0]
    DFF = cw_ref.shape[1] // 2
    C = wua_ref.shape[1]

    if streaming:
        j = pl.program_id(1)
    else:
        t, j = pl.program_id(1), pl.program_id(2)
    nj = pl.num_programs(2 - int(streaming))

    @pl.when(j == 0)
    def _():
        g = nffn_ref[...]
        def norm(r0):
            rows = pl.ds(r0, ROWS)
            h_scr[rows, :] = _rms(x_ref[rows, :], g)
            acc_scr[rows, :] = jnp.zeros((ROWS, acc_scr.shape[1]), F32)
        _rows_loop(R, ROWS, norm, unroll=2 * LOOP_UNROLL)
        if not streaming:
            @pl.when(t == 0)
            def _():
                carry_scr[...] = jnp.zeros(carry_scr.shape, F32)

    if streaming:
        for src, dst in casts:
            dst[...] = src[...]
        (_, wua_ref), (_, wug_ref), (_, wd_ref) = casts

    tail = slice(HALO + L - (W - 1), HALO + L)
    n_chunk = min(L, FFN_CHUNK)
    for c in range(C // FFN_SUB):
        cs = slice(c * FFN_SUB, (c + 1) * FFN_SUB)
        for half, (y_scr, w_ref) in enumerate(((ya_scr, wua_ref), (yg_scr, wug_ref))):
            y_scr[:, HALO:HALO + L, cs] = _dot(h_scr[...], w_ref[:, cs]).reshape(S, L, FFN_SUB)
            if streaming:
                for s in range(S):
                    y_scr[s, HALO - (W - 1):HALO, cs] = st_ref[s, half, :, cs]
                    new_ref[s, half, :, cs] = y_scr[s, tail, cs]
            else:
                y_scr[0, 0:HALO, cs] = carry_scr[j, half, :, cs]
                carry_scr[j, half, :, cs] = y_scr[0, L:L + HALO, cs]
                out_cols = pl.ds(pl.multiple_of(j * C + c * FFN_SUB, FFN_SUB), FFN_SUB)
                new_ref[half, :, out_cols] = y_scr[0, tail, cs]

        lin_cols = pl.ds(pl.multiple_of(j * C + c * FFN_SUB, FFN_SUB), FFN_SUB)
        gated_cols = pl.ds(pl.multiple_of(DFF + j * C + c * FFN_SUB, FFN_SUB), FFN_SUB)
        taps_a = [jnp.broadcast_to(cw_ref[k:k + 1, lin_cols], (n_chunk, FFN_SUB)) for k in range(W)]
        taps_g = [jnp.broadcast_to(cw_ref[k:k + 1, gated_cols], (n_chunk, FFN_SUB)) for k in range(W)]
        bias_a = jnp.broadcast_to(cb_ref[:, lin_cols], (n_chunk, FFN_SUB))
        bias_g = jnp.broadcast_to(cb_ref[:, gated_cols], (n_chunk, FFN_SUB))
        for s in range(S):
            for r0 in range(0, L, n_chunk):
                ext = slice(r0, r0 + n_chunk + HALO)
                lin = _causal_conv(ya_scr[s, ext, cs], taps_a, bias_a)
                gated = _causal_conv(yg_scr[s, ext, cs], taps_g, bias_g)
                out0 = s * L + r0
                act_scr[out0:out0 + n_chunk, cs] = _gelu(gated) * lin

    for ks in (slice(0, C - FFN_SUB), slice(C - FFN_SUB, C)) if C > FFN_SUB else (slice(0, C),):
        acc_scr[...] += _dot(act_scr[:, ks], wd_ref[ks, :])

    @pl.when(j == nj - 1)
    def _():
        g = nfin_ref[...]
        def final(r0):
            rows = pl.ds(r0, ROWS)
            y_ref[rows, :] = _rms(x_ref[rows, :] + acc_scr[rows, :], g)
        _rows_loop(R, ROWS, final, unroll=2 * LOOP_UNROLL)


def _ffn(streaming, x, state, p, w_up_lin, w_up_gated, w_down):
    rows, D = x.shape
    DFF = w_down.shape[0] if streaming else 2 * w_down.shape[0]
    W = p["ffn_conv_w"].shape[0]
    C = FFN_SAMPLE_COLS if streaming else FFN_COLS
    nj = DFF // C
    (wua, off_a), (wug, off_g) = w_up_lin, w_up_gated
    if streaming:
        N = state.shape[0]
        L = rows // N
        R = FFN_ROWS
        S = R // L
        grid = (rows // R, nj)
        row_map = lambda i, j: (i, 0)
        colmap = lambda off: (lambda i, j: (0, j + off))
        wd_map = lambda i, j: (j, 0)
        new_spec = pl.BlockSpec((S, 2, W - 1, C), lambda i, j: (i, 0, 0, j))
        new_shape = jax.ShapeDtypeStruct((N, 2, W - 1, DFF), F32)
        in_specs = [pl.BlockSpec((R, D), row_map),
                    pl.BlockSpec((S, 2, W - 1, C), lambda i, j: (i, 0, 0, j))]
        args = [x, state]
    else:
        B, T = state
        S, L, R = 1, FFN_ROWS, FFN_ROWS
        nt = T // R
        grid = (B, nt, nj)
        row_map = lambda b, t, j: (b * nt + t, 0)
        colmap = lambda off: (lambda b, t, j: (0, j + off))
        wd_map = lambda b, t, j: (j, 0)
        new_spec = pl.BlockSpec((None, None, 2, W - 1, DFF), lambda b, t, j: (b, t, 0, 0, 0))
        new_shape = jax.ShapeDtypeStruct((B, nt, 2, W - 1, DFF), F32)
        in_specs = [pl.BlockSpec((R, D), row_map)]
        args = [x]
    in_specs += [_const_spec((1, D)),
                 pl.BlockSpec((wua.shape[0], C), colmap(off_a // C)), pl.BlockSpec((wug.shape[0], C), colmap(off_g // C)),
                 _const_spec((W, 2 * DFF)), _const_spec((1, 2 * DFF)),
                 pl.BlockSpec((C if streaming else C // 2, D), wd_map), _const_spec((1, D))]
    args += [p["norm_ffn"], wua, wug, p["ffn_conv_w"], p["ffn_conv_b"], w_down, p["norm_final"]]
    out_specs = [pl.BlockSpec((R, D), row_map), new_spec]
    out_shape = [jax.ShapeDtypeStruct((rows, D), F32), new_shape]
    if streaming:
        out_specs += [pl.BlockSpec((D // 2, C), colmap(0)), pl.BlockSpec((D // 2, C), colmap(0)),
                      pl.BlockSpec((C // 2, D), wd_map)]
        out_shape += [jax.ShapeDtypeStruct((D // 2, DFF), U32), jax.ShapeDtypeStruct((D // 2, DFF), U32),
                      jax.ShapeDtypeStruct((DFF // 2, D), U32)]
    scratch = [_packed_shape(R, D),
               pltpu.VMEM((R, D), F32),
               pltpu.VMEM((S, L + HALO, C), F32),
               pltpu.VMEM((S, L + HALO, C), F32),
               _packed_shape(R, C),
               pltpu.VMEM((nj, 2, HALO, C), F32)]
    return pl.pallas_call(
        functools.partial(_ffn_body, streaming, S, L),
        grid=grid, in_specs=in_specs, out_specs=out_specs, out_shape=out_shape,
        scratch_shapes=scratch,
        compiler_params=_params(len(grid)),
        name="ffn_sample" if streaming else "ffn_prompt",
    )(*args)


def kernel(x_prompt, x_sample, mem_prompt, cache_mem_k, cache_mem_v, state_lru_h, state_lru_conv, state_ffn_conv, norm_mix, w_in, g_v, gmlp_w, gmlp_b, lru_conv_w, lru_conv_b, lru_wa, lru_ba, lru_wx, lru_bx, lru_lam, g_a, g_b, w_out, norm_mem, w_kv, norm_xa, w_q, w_o, norm_ffn, w_up, ffn_conv_w, ffn_conv_b, w_down, norm_final):
    depth = w_in.shape[0]
    assert depth == 1, "single-layer trunk"
    B, T, D = x_prompt.shape
    N, L, _ = x_sample.shape
    M = mem_prompt.shape[1]
    n_heads, head_dim = cache_mem_k.shape[3], cache_mem_k.shape[4]
    G, NCH = gmlp_w.shape[1], gmlp_w.shape[2]
    DA, DB = g_v.shape[1], lru_lam.shape[1]
    GD = DA // G
    DFF = w_down.shape[1]
    W_LRU, W_FFN = lru_conv_w.shape[1], ffn_conv_w.shape[1]
    assert L <= CAUSAL_CHUNK and NCH % L == 0 and L == ROWS
    row = lambda a: a.reshape(1, -1)

    shared = {
        "norm_mix": row(norm_mix[0]), "g_v": row(g_v[0]),
        "lru_conv_w": lru_conv_w[0], "lru_conv_b": row(lru_conv_b[0]),
        "lru_wa": lru_wa[0], "lru_ba": row(lru_ba[0]),
        "lru_wx": lru_wx[0], "lru_bx": row(lru_bx[0]), "lru_lam": row(lru_lam[0]),
        "g_a": row(g_a[0]), "g_b": row(g_b[0]),
        "norm_xa": row(norm_xa[0]),
        "norm_ffn": row(norm_ffn[0]), "ffn_conv_w": ffn_conv_w[0],
        "ffn_conv_b": row(ffn_conv_b[0]), "norm_final": row(norm_final),
    }
    (mk, mv, mk_b, mv_b), resident = _memory_kv(mem_prompt.reshape(B * M, D), row(norm_mem[0]), w_kv[0],
                                                [w_in[0], w_out[0], w_q[0], w_o[0]], head_dim)
    shared.update(zip(("w_in", "w_out", "w_q", "w_o"), resident))
    prompt_p = dict(shared, gmlp_w_eff=gmlp_w[0],
                    gmlp_bias_rows=jnp.repeat(gmlp_b[0].T, GD, axis=1))
    reps = NCH // L
    sample_p = dict(shared, gmlp_w_eff=jnp.tile(gmlp_w[0][:, :L, :L], (1, reps, reps)),
                    gmlp_bias_rows=jnp.repeat(jnp.tile(gmlp_b[0][:, :L].T, (reps, 1)), GD, axis=1))

    xs = x_sample.reshape(N * L, D)
    pconv = jnp.pad(state_lru_conv[0], ((0, 0), (HALO - (W_LRU - 1), 0), (0, 0)))
    xs, v_s, conv_s, h_s = _mixer(True, xs, (pconv, state_lru_h[0].reshape(N, 1, DB)), sample_p)
    xs = _attn_sample(xs, cache_mem_k[0], cache_mem_v[0], sample_p)
    ffn_prev = jnp.swapaxes(state_ffn_conv[0].reshape(N, W_FFN - 1, 2, DFF), 1, 2)
    y_s, ffn_s, w_up_lin, w_up_gated, w_down_b = _ffn(True, xs, ffn_prev, sample_p,
                                                      (w_up[0], 0), (w_up[0], DFF), w_down[0])

    xp = x_prompt.reshape(B * T, D)
    xp, conv_p, h_p = _mixer(False, xp, (B, T), prompt_p)
    xp = _attn_prompt(xp, mk_b, mv_b, prompt_p, n_heads, B)
    y_p, ffn_p = _ffn(False, xp, (B, T), prompt_p, (w_up_lin, 0), (w_up_gated, 0), w_down_b)

    def ffn_state(a):
        return jnp.swapaxes(a, 1, 2).reshape(1, a.shape[0], W_FFN - 1, 2 * DFF)

    return (y_p.reshape(B, T, D), y_s.reshape(N, L, D),
            mk.reshape(1, B, M, n_heads, head_dim), mv.reshape(1, B, M, n_heads, head_dim),
            h_p.reshape(1, B, DB), conv_p.reshape(1, B, W_LRU - 1, DB), ffn_state(ffn_p[:, -1]),
            h_s.reshape(1, N, DB), conv_s.reshape(1, N, W_LRU - 1, DB), ffn_state(ffn_s),
            v_s.reshape(1, N, L, DA))
```
